```python
import jax, jax.numpy as jnp
from jax import lax
import numpy as np

D_MODEL = 1024
BATCH = 32
SEQ = 256
DEPTH = 1
DEC_BATCH = 8
DEC_SEQ = 2048
PAST_LEN = 512

GRID_W = 64
N_HEADS_GLA = 4
HEAD_DK = 128
HEAD_DV = 256
GLA_KDIM = N_HEADS_GLA * HEAD_DK
GLA_VDIM = N_HEADS_GLA * HEAD_DV
GATE_RANK = 16
GATE_NORM = 16.0
CHUNK = 64
N_FGROUPS = 4
FGROUP_CH = 128
FOURIER_DIM = N_FGROUPS * FGROUP_CH
ROPE_BASE = 10000.0
EPS = 1e-6
IN_DIM = 2 * GLA_KDIM + 2 * GLA_VDIM + 2 * GATE_RANK + 2 * FOURIER_DIM + 2 * D_MODEL

kernel_name = "hybrid_gla_fnet_diffusion_step"


def _rmsnorm(x, g):
    xf = x.astype(jnp.float32)
    xf = xf * lax.rsqrt(jnp.mean(xf * xf, axis=-1, keepdims=True) + EPS)
    return xf.astype(x.dtype) * g


def _grid_rope(n_tokens):
    rows = n_tokens // GRID_W
    r = jnp.repeat(jnp.arange(rows), GRID_W).astype(jnp.float32)
    col = jnp.tile(jnp.arange(GRID_W), rows).astype(jnp.float32)
    n_freq = HEAD_DK // 4
    freqs = ROPE_BASE ** (-jnp.arange(n_freq, dtype=jnp.float32) / n_freq)
    ang = jnp.concatenate([r[:, None] * freqs, col[:, None] * freqs], axis=-1)
    return jnp.cos(ang), jnp.sin(ang)


def _apply_rope(x, cos, sin):
    B, T, H, Dk = x.shape
    nf = Dk // 4
    xr = x.reshape(B, T, H, 2, 2, nf)
    x1, x2 = xr[..., 0, :], xr[..., 1, :]
    c = cos.reshape(T, 2, nf)[None, :, None].astype(x.dtype)
    s = sin.reshape(T, 2, nf)[None, :, None].astype(x.dtype)
    out = jnp.stack([x1 * c - x2 * s, x2 * c + x1 * s], axis=-2)
    return out.reshape(B, T, H, Dk)


def _gla_scan(q, k, v, g, s0):
    B, T, H, Dk = q.shape
    Dv = v.shape[-1]
    n = T // CHUNK
    f32 = jnp.float32
    qc = q.astype(f32).reshape(B, n, CHUNK, H, Dk)
    kc = k.astype(f32).reshape(B, n, CHUNK, H, Dk)
    vc = v.astype(f32).reshape(B, n, CHUNK, H, Dv)
    bc = jnp.cumsum(g.astype(f32).reshape(B, n, CHUNK, H, Dk), axis=2)
    b_last = bc[:, :, -1]
    q_dec = qc * jnp.exp(bc)
    k_inv = kc * jnp.exp(-bc)
    mask = jnp.tril(jnp.ones((CHUNK, CHUNK), dtype=bool))
    scores = jnp.where(mask, jnp.einsum('bnihd,bnjhd->bnhij', q_dec, k_inv), 0.0)
    o_intra = jnp.einsum('bnhij,bnjhe->bnihe', scores, vc)
    k_to_end = kc * jnp.exp(b_last[:, :, None] - bc)
    d_state = jnp.einsum('bnjhd,bnjhe->bnhde', k_to_end, vc)
    decay = jnp.exp(b_last)

    def step(s, inp):
        dec, ds = inp
        return dec[..., None] * s + ds, s

    s_final, s_prev = lax.scan(step, s0.astype(f32),
                               (jnp.moveaxis(decay, 1, 0), jnp.moveaxis(d_state, 1, 0)))
    s_prev = jnp.moveaxis(s_prev, 0, 1)
    o_inter = jnp.einsum('bnihd,bnhde->bnihe', q_dec, s_prev)
    o = (o_intra + o_inter).reshape(B, T, H, Dv)
    return o, s_final


def _layer(x, cond, s_f0, s_b0, rope, w_ada, b_ada, norm_g, w_in, w_af, b_af, w_ab, b_ab,
           gla_norm_g, w_four, w_proj_a, w_proj_b, w_out):
    B, T, _ = x.shape
    ada = jax.nn.silu(cond) @ w_ada + b_ada
    shift, scale, gate = jnp.split(ada, 3, axis=-1)
    h = _rmsnorm(x, norm_g) * (1 + scale[:, None]) + shift[:, None]
    z = h @ w_in
    sizes = (GLA_KDIM, GLA_KDIM, GLA_VDIM, GLA_VDIM, GATE_RANK, GATE_RANK,
             FOURIER_DIM, FOURIER_DIM, D_MODEL, D_MODEL)
    idx = [int(i) for i in np.cumsum(sizes)[:-1]]
    q, k, v, gate_a, r_f, r_b, u, gate_b, mg_a, mg_b = jnp.split(z, idx, axis=-1)

    q = q.reshape(B, T, N_HEADS_GLA, HEAD_DK)
    k = k.reshape(B, T, N_HEADS_GLA, HEAD_DK)
    v = v.reshape(B, T, N_HEADS_GLA, HEAD_DV)
    if rope is not None:
        q = _apply_rope(q, rope[0], rope[1])
        k = _apply_rope(k, rope[0], rope[1])
    q = q * (HEAD_DK ** -0.5)
    g_f = (jax.nn.log_sigmoid((r_f @ w_af + b_af).astype(jnp.float32)) / GATE_NORM).reshape(B, T, N_HEADS_GLA, HEAD_DK)
    g_b = (jax.nn.log_sigmoid((r_b @ w_ab + b_ab).astype(jnp.float32)) / GATE_NORM).reshape(B, T, N_HEADS_GLA, HEAD_DK)
    o_f, s_f = _gla_scan(q, k, v, g_f, s_f0)
    o_b, s_b = _gla_scan(q[:, ::-1], k[:, ::-1], v[:, ::-1], g_b[:, ::-1], s_b0)
    o = (o_f + o_b[:, ::-1]).astype(x.dtype)
    o = _rmsnorm(o, gla_norm_g).reshape(B, T, GLA_VDIM) * jax.nn.silu(gate_a)
    y_a = o @ w_proj_a

    uf = u.reshape(B, T, N_FGROUPS, FGROUP_CH).astype(jnp.float32)
    four = jnp.real(jnp.fft.fft2(uf, axes=(1, 3), norm="ortho")).astype(x.dtype)
    four = jnp.einsum('btgc,gcd->btgd', four, w_four).reshape(B, T, FOURIER_DIM) * jax.nn.silu(gate_b)
    y_b = four @ w_proj_b

    merged = jax.nn.sigmoid(mg_a) * y_a + jax.nn.sigmoid(mg_b) * y_b
    x = x + gate[:, None] * (merged @ w_out)
    return x, s_f, s_b


def setup_inputs(seed: int = 0) -> dict:
    key = jax.random.key(seed)
    ks = jax.random.split(key, 24)
    f32 = jnp.float32
    nrm = lambda k, shape, s: jax.random.normal(k, shape, f32) * s
    st_shape = (DEC_BATCH, DEPTH, N_HEADS_GLA, HEAD_DK, HEAD_DV)
    return {
        "x_prompt": nrm(ks[0], (BATCH, SEQ, D_MODEL), 1.0),
        "x_sample": nrm(ks[1], (DEC_BATCH, DEC_SEQ, D_MODEL), 1.0),
        "state_gla_fwd": nrm(ks[2], st_shape, 1.0),
        "state_gla_bwd": nrm(ks[3], st_shape, 1.0),
        "c": nrm(ks[4], (DEC_BATCH, D_MODEL), 1.0),
        "c_ctx": nrm(ks[5], (D_MODEL,), 1.0),
        "w_ada": nrm(ks[6], (DEPTH, D_MODEL, 3 * D_MODEL), 0.5 * D_MODEL ** -0.5),
        "b_ada": nrm(ks[7], (DEPTH, 3 * D_MODEL), 0.02),
        "norm_g": 1.0 + nrm(ks[8], (DEPTH, D_MODEL), 0.02),
        "w_in": nrm(ks[9], (DEPTH, D_MODEL, IN_DIM), D_MODEL ** -0.5),
        "w_alpha_fwd": nrm(ks[10], (DEPTH, GATE_RANK, GLA_KDIM), GATE_RANK ** -0.5),
        "b_alpha_fwd": nrm(ks[11], (DEPTH, GLA_KDIM), 0.02),
        "w_alpha_bwd": nrm(ks[12], (DEPTH, GATE_RANK, GLA_KDIM), GATE_RANK ** -0.5),
        "b_alpha_bwd": nrm(ks[13], (DEPTH, GLA_KDIM), 0.02),
        "gla_norm_g": 1.0 + nrm(ks[14], (DEPTH, HEAD_DV), 0.02),
        "w_four": nrm(ks[15], (DEPTH, N_FGROUPS, FGROUP_CH, FGROUP_CH), FGROUP_CH ** -0.5),
        "w_proj_a": nrm(ks[16], (DEPTH, GLA_VDIM, D_MODEL), GLA_VDIM ** -0.5),
        "w_proj_b": nrm(ks[17], (DEPTH, FOURIER_DIM, D_MODEL), FOURIER_DIM ** -0.5),
        "w_out": nrm(ks[18], (DEPTH, D_MODEL, D_MODEL), D_MODEL ** -0.5),
        "final_norm_g": 1.0 + nrm(ks[19], (D_MODEL,), 0.02),
    }


def reference(x_prompt, x_sample, state_gla_fwd, state_gla_bwd, c, c_ctx, w_ada, b_ada, norm_g,
              w_in, w_alpha_fwd, b_alpha_fwd, w_alpha_bwd, b_alpha_bwd, gla_norm_g, w_four,
              w_proj_a, w_proj_b, w_out, final_norm_g):
    b_p = x_prompt.shape[0]
    rope = _grid_rope(x_sample.shape[1])
    hp, hs = x_prompt, x_sample
    new_f, new_b = [], []
    for l in range(DEPTH):
        params = (w_ada[l], b_ada[l], norm_g[l], w_in[l], w_alpha_fwd[l], b_alpha_fwd[l],
                  w_alpha_bwd[l], b_alpha_bwd[l], gla_norm_g[l], w_four[l], w_proj_a[l],
                  w_proj_b[l], w_out[l])
        zeros = jnp.zeros((b_p, N_HEADS_GLA, HEAD_DK, HEAD_DV), jnp.float32)
        hp, sf, sb = _layer(hp, c_ctx[None, :], zeros, zeros, None, *params)
        new_f.append(sf.astype(x_prompt.dtype))
        new_b.append(sb.astype(x_prompt.dtype))
        hs, _, _ = _layer(hs, c, state_gla_fwd[:, l], state_gla_bwd[:, l], rope, *params)
    y_prompt = _rmsnorm(hp, final_norm_g)
    y_sample = _rmsnorm(hs, final_norm_g)
    new_state_fwd = jnp.stack(new_f, axis=1)
    new_state_bwd = jnp.stack(new_b, axis=1)
    return (y_prompt, y_sample, new_state_fwd, new_state_bwd)
```

```python
import functools

import numpy as np
import jax
import jax.numpy as jnp
from jax import lax
from jax.experimental import pallas as pl
from jax.experimental.pallas import tpu as pltpu

F32 = jnp.float32
BF16 = jnp.bfloat16

D_MODEL = 1024
N_HEADS = 4
HEAD_DK = 128
HEAD_DV = 256
KDIM = N_HEADS * HEAD_DK
VDIM = N_HEADS * HEAD_DV
GATE_RANK = 16
GATE_NORM = 16.0
CHUNK = 64
N_FGROUPS = 4
FGROUP_CH = 128
FOURIER_DIM = N_FGROUPS * FGROUP_CH
GRID_W = 64
ROPE_BASE = 10000.0
EPS = 1e-6

LANES = 128
TOKEN_TILE = 256
CHUNKS_PER_TILE = TOKEN_TILE // CHUNK
VMEM_LIMIT_BYTES = 56 * 1024 * 1024

_OFF_QK = 0
_OFF_V = _OFF_QK + 2 * KDIM
_OFF_GA = _OFF_V + VDIM
_OFF_U = _OFF_GA + VDIM
_OFF_GB = _OFF_U + FOURIER_DIM
_OFF_MA = _OFF_GB + FOURIER_DIM
_OFF_MB = _OFF_MA + D_MODEL
_W_MAIN_COLS = _OFF_MB + D_MODEL


def _mm(a, b):
    return jnp.dot(a, b, preferred_element_type=F32)


def _mm_ta(a, b):
    return lax.dot_general(a, b, (((0,), (0,)), ((), ())), preferred_element_type=F32)


def _mm_tb(a, b):
    return lax.dot_general(a, b, (((1,), (1,)), ((), ())), preferred_element_type=F32)


def _split_bf16(x):
    hi = x.astype(BF16)
    lo = (x - hi.astype(F32)).astype(BF16)
    return hi, lo


def _const_spec(shape):
    nd = len(shape)
    return pl.BlockSpec(shape, lambda *_: (0,) * nd)


def _ada_kernel(c_ref, w_ref, b_ref, o_ref):
    c = c_ref[...]
    s = c * jax.nn.sigmoid(c)
    o_ref[...] = _mm(s.astype(BF16), w_ref[...]) + b_ref[...]


def _ada(cond, w_bf, b):
    rows = cond.shape[0]
    return pl.pallas_call(
        _ada_kernel,
        out_shape=jax.ShapeDtypeStruct((rows, 3 * D_MODEL), F32),
        compiler_params=pltpu.CompilerParams(vmem_limit_bytes=VMEM_LIMIT_BYTES),
        name="ada",
    )(cond, w_bf, b)


def _in_proj_kernel(use_rope, *refs):
    if use_rope:
        (x_ref, sc_ref, sh_ref, ng_ref, w_ref, wr_ref, wab_ref, bab_ref, cs_ref, cos_ref, sin_ref,
         qdf_ref, kif_ref, kef_ref, qdb_ref, kib_ref, keb_ref, dtf_ref, dtb_ref,
         v_ref, ga_ref, a_ref, b_ref, gb_ref, sa_ref, sb_ref) = refs
    else:
        (x_ref, sc_ref, sh_ref, ng_ref, w_ref, wr_ref, wab_ref, bab_ref, cs_ref,
         qdf_ref, kif_ref, kef_ref, qdb_ref, kib_ref, keb_ref, dtf_ref, dtb_ref,
         v_ref, ga_ref, a_ref, b_ref, gb_ref, sa_ref, sb_ref) = refs

    x = x_ref[...]
    xn = x * lax.rsqrt(jnp.mean(x * x, axis=-1, keepdims=True) + EPS)
    h = (xn * ng_ref[...]) * (1.0 + sc_ref[...]) + sh_ref[...]
    hb = h.astype(BF16)

    v_ref[...] = _mm(hb, w_ref[:, _OFF_V:_OFF_V + VDIM]).astype(BF16)
    ga = _mm(hb, w_ref[:, _OFF_GA:_OFF_GA + VDIM])
    ga_ref[...] = (ga * jax.nn.sigmoid(ga)).astype(BF16)
    gb = _mm(hb, w_ref[:, _OFF_GB:_OFF_GB + FOURIER_DIM])
    gb_ref[...] = (gb * jax.nn.sigmoid(gb)).astype(BF16)
    sa_ref[...] = jax.nn.sigmoid(_mm(hb, w_ref[:, _OFF_MA:_OFF_MA + D_MODEL])).astype(BF16)
    sb_ref[...] = jax.nn.sigmoid(_mm(hb, w_ref[:, _OFF_MB:_OFF_MB + D_MODEL])).astype(BF16)

    ub = _mm(hb, w_ref[:, _OFF_U:_OFF_U + FOURIER_DIM]).astype(BF16)
    for g in range(N_FGROUPS):
        ab = _mm(ub[:, g * FGROUP_CH:(g + 1) * FGROUP_CH], cs_ref[...])
        a_ref[:, g * FGROUP_CH:(g + 1) * FGROUP_CH] = ab[:, :FGROUP_CH].astype(BF16)
        b_ref[:, g * FGROUP_CH:(g + 1) * FGROUP_CH] = ab[:, FGROUP_CH:].astype(BF16)

    qk = _mm(hb, w_ref[:, _OFF_QK:_OFF_QK + 2 * KDIM])
    q = qk[:, :KDIM] * (HEAD_DK ** -0.5)
    k = qk[:, KDIM:]
    if use_rope:
        cos = cos_ref[...]
        sin = sin_ref[...]
        lane = lax.broadcasted_iota(jnp.int32, (TOKEN_TILE, HEAD_DK), 1)
        first_half = (lane // (HEAD_DK // 4)) % 2 == 0

        def rope(t):
            outs = []
            for hh in range(N_HEADS):
                th = t[:, hh * HEAD_DK:(hh + 1) * HEAD_DK]
                partner = jnp.where(first_half,
                                    pltpu.roll(th, HEAD_DK - HEAD_DK // 4, axis=1),
                                    pltpu.roll(th, HEAD_DK // 4, axis=1))
                outs.append(th * cos + partner * sin)
            return jnp.concatenate(outs, axis=1)

        q = rope(q)
        k = rope(k)

    r = _mm(hb, wr_ref[...])
    xg = _mm(r.astype(BF16), wab_ref[...]) + bab_ref[...]
    g_all = (jnp.minimum(xg, 0.0) - jnp.log1p(jnp.exp(-jnp.abs(xg)))) * (1.0 / GATE_NORM)

    row = lax.broadcasted_iota(jnp.int32, (TOKEN_TILE, TOKEN_TILE), 0)
    col = lax.broadcasted_iota(jnp.int32, (TOKEN_TILE, TOKEN_TILE), 1)
    same_chunk = (row // CHUNK) == (col // CHUNK)
    ones_bd = jnp.where(same_chunk, 1.0, 0.0).astype(BF16)
    tok = lax.broadcasted_iota(jnp.int32, (TOKEN_TILE, LANES), 0)
    cidx = lax.broadcasted_iota(jnp.int32, (TOKEN_TILE, LANES), 1)
    chunk_ind = jnp.where(tok // CHUNK == cidx, 1.0, 0.0).astype(BF16)

    for direction, (qd_ref, ki_ref, ke_ref, dt_ref) in enumerate(
            ((qdf_ref, kif_ref, kef_ref, dtf_ref), (qdb_ref, kib_ref, keb_ref, dtb_ref))):
        g = g_all[:, direction * KDIM:(direction + 1) * KDIM]
        g_hi, g_lo = _split_bf16(g)
        tri = (col <= row) if direction == 0 else (col >= row)
        tri_bd = jnp.where(same_chunk & tri, 1.0, 0.0).astype(BF16)
        bc = _mm(tri_bd, g_hi) + _mm(tri_bd, g_lo)
        bl = _mm(ones_bd, g_hi) + _mm(ones_bd, g_lo)
        qd_ref[...] = (q * jnp.exp(bc)).astype(BF16)
        ki_ref[...] = (k * jnp.exp(-bc)).astype(BF16)
        ke_ref[...] = (k * jnp.exp(bl - bc)).astype(BF16)
        dt_ref[...] = jnp.exp(_mm_ta(g_hi, chunk_ind) + _mm_ta(g_lo, chunk_ind))


def _in_proj(x, scale, shift, norm_g, w_main, w_r, w_ab, b_ab, cs_tab, rope):
    B, T, _ = x.shape
    nt = T // TOKEN_TILE
    use_rope = rope is not None
    per_batch_mod = scale.shape[0] > 1

    def tok_spec(cols):
        return pl.BlockSpec((None, TOKEN_TILE, cols), lambda b, t: (b, t, 0))

    mod_spec = pl.BlockSpec((None, 1, D_MODEL), (lambda b, t: (b, 0, 0)) if per_batch_mod else (lambda b, t: (0, 0, 0)))
    in_specs = [
        tok_spec(D_MODEL), mod_spec, mod_spec, _const_spec((1, D_MODEL)),
        _const_spec((D_MODEL, _W_MAIN_COLS)), _const_spec((D_MODEL, LANES)),
        _const_spec((LANES, 2 * KDIM)), _const_spec((1, 2 * KDIM)),
        _const_spec((FGROUP_CH, 2 * FGROUP_CH)),
    ]
    args = [x, scale, shift, norm_g, w_main, w_r, w_ab, b_ab, cs_tab]
    if use_rope:
        in_specs += [pl.BlockSpec((TOKEN_TILE, HEAD_DK), lambda b, t: (t, 0))] * 2
        args += list(rope)

    dt_spec = pl.BlockSpec((None, None, KDIM, LANES), lambda b, t: (b, t, 0, 0))
    tok_bf = lambda cols: jax.ShapeDtypeStruct((B, T, cols), BF16)
    dt_shape = jax.ShapeDtypeStruct((B, nt, KDIM, LANES), F32)
    out_specs = [tok_spec(KDIM)] * 6 + [dt_spec] * 2 + [
        tok_spec(VDIM), tok_spec(VDIM), tok_spec(FOURIER_DIM), tok_spec(FOURIER_DIM),
        tok_spec(FOURIER_DIM), tok_spec(D_MODEL), tok_spec(D_MODEL)]
    out_shape = [tok_bf(KDIM)] * 6 + [dt_shape] * 2 + [
        tok_bf(VDIM), tok_bf(VDIM), tok_bf(FOURIER_DIM), tok_bf(FOURIER_DIM),
        tok_bf(FOURIER_DIM), tok_bf(D_MODEL), tok_bf(D_MODEL)]
    return pl.pallas_call(
        functools.partial(_in_proj_kernel, use_rope),
        grid=(B, nt),
        in_specs=in_specs,
        out_specs=out_specs,
        out_shape=out_shape,
        compiler_params=pltpu.CompilerParams(
            dimension_semantics=("arbitrary", "arbitrary"), vmem_limit_bytes=VMEM_LIMIT_BYTES),
        name="in_proj_rope" if use_rope else "in_proj",
    )(*args)


def _gla_kernel(heads, n_tiles,
                qdf_ref, kif_ref, kef_ref, qdb_ref, kib_ref, keb_ref, dtf_ref, dtb_ref,
                v_ref, ga_ref, s0f_ref, s0b_ref, gn_ref,
                og_ref, sf_ref, sb_ref,
                state, of_acc, ob_acc):
    state[0] = s0f_ref[...]
    state[1] = s0b_ref[...]
    row = lax.broadcasted_iota(jnp.int32, (CHUNK, CHUNK), 0)
    col = lax.broadcasted_iota(jnp.int32, (CHUNK, CHUNK), 1)
    causal = (col <= row, col >= row)
    dir_refs = ((qdf_ref, kif_ref, kef_ref, dtf_ref, of_acc), (qdb_ref, kib_ref, keb_ref, dtb_ref, ob_acc))

    def tile_body(j, carry):
        for direction in range(2):
            qd_ref, ki_ref, ke_ref, dt_ref, o_acc = dir_refs[direction]
            t = j if direction == 0 else n_tiles - 1 - j
            base = pl.multiple_of(t * TOKEN_TILE, TOKEN_TILE)
            chunk_order = range(CHUNKS_PER_TILE) if direction == 0 else range(CHUNKS_PER_TILE - 1, -1, -1)
            for c in chunk_order:
                rows = pl.ds(base + c * CHUNK, CHUNK)
                for hh in range(heads):
                    kc = slice(hh * HEAD_DK, (hh + 1) * HEAD_DK)
                    vc = slice(hh * HEAD_DV, (hh + 1) * HEAD_DV)
                    qd = qd_ref[rows, kc]
                    vv = v_ref[rows, vc]
                    s = state[direction, hh]
                    scores = jnp.where(causal[direction], _mm_tb(qd, ki_ref[rows, kc]), 0.0).astype(BF16)
                    o_acc[rows, vc] = _mm(scores, vv) + _mm(qd, s.astype(BF16))
                    dec = dt_ref[t, kc, c:c + 1]
                    state[direction, hh] = dec * s + _mm_ta(ke_ref[rows, kc], vv)
        return carry

    lax.fori_loop(0, n_tiles, tile_body, 0)

    def finish(j, carry):
        rows = pl.ds(pl.multiple_of(j * TOKEN_TILE, TOKEN_TILE), TOKEN_TILE)
        for hh in range(heads):
            vc = slice(hh * HEAD_DV, (hh + 1) * HEAD_DV)
            o = of_acc[rows, vc] + ob_acc[rows, vc]
            on = o * lax.rsqrt(jnp.mean(o * o, axis=-1, keepdims=True) + EPS) * gn_ref[...]
            og_ref[rows, vc] = (on * ga_ref[rows, vc].astype(F32)).astype(BF16)
        return carry

    lax.fori_loop(0, n_tiles, finish, 0)
    sf_ref[...] = state[0]
    sb_ref[...] = state[1]


def _gla(qdf, kif, kef, qdb, kib, keb, dtf, dtb, v, ga, s0f, s0b, gla_norm_g, heads):
    B, T, _ = v.shape
    nt = T // TOKEN_TILE
    hg = N_HEADS // heads
    k_spec = pl.BlockSpec((None, T, heads * HEAD_DK), lambda b, h: (b, 0, h))
    v_spec = pl.BlockSpec((None, T, heads * HEAD_DV), lambda b, h: (b, 0, h))
    dt_spec = pl.BlockSpec((None, nt, heads * HEAD_DK, LANES), lambda b, h: (b, 0, h, 0))
    s_spec = pl.BlockSpec((None, heads, HEAD_DK, HEAD_DV), lambda b, h: (b, h, 0, 0))
    s_shape = jax.ShapeDtypeStruct((B, N_HEADS, HEAD_DK, HEAD_DV), F32)
    return pl.pallas_call(
        functools.partial(_gla_kernel, heads, nt),
        grid=(B, hg),
        in_specs=[k_spec] * 6 + [dt_spec] * 2 + [v_spec, v_spec, s_spec, s_spec, _const_spec((1, HEAD_DV))],
        out_specs=[v_spec, s_spec, s_spec],
        out_shape=[jax.ShapeDtypeStruct((B, T, VDIM), BF16), s_shape, s_shape],
        scratch_shapes=[pltpu.VMEM((2, heads, HEAD_DK, HEAD_DV), F32),
                        pltpu.VMEM((T, heads * HEAD_DV), F32),
                        pltpu.VMEM((T, heads * HEAD_DV), F32)],
        compiler_params=pltpu.CompilerParams(
            dimension_semantics=("arbitrary", "arbitrary"), vmem_limit_bytes=VMEM_LIMIT_BYTES),
        name="gla",
    )(qdf, kif, kef, qdb, kib, keb, dtf, dtb, v, ga, s0f, s0b, gla_norm_g)


def _fourier_kernel(ct_ref, st_ref, a_ref, b_ref, gb_ref, wf_ref, o_ref):
    f = (_mm(ct_ref[...], a_ref[...]) + _mm(st_ref[...], b_ref[...])).astype(BF16)
    for g in range(N_FGROUPS):
        cols = slice(g * FGROUP_CH, (g + 1) * FGROUP_CH)
        o_ref[:, cols] = (_mm(f[:, cols], wf_ref[g]) * gb_ref[:, cols].astype(F32)).astype(BF16)


def _fourier(ct, stn, a, bm, gb, wf):
    B, T, _ = a.shape
    tf = min(T, 512)
    tab_spec = pl.BlockSpec((tf, T), lambda b, t: (t, 0))
    seq_spec = pl.BlockSpec((None, T, FOURIER_DIM), lambda b, t: (b, 0, 0))
    tile_spec = pl.BlockSpec((None, tf, FOURIER_DIM), lambda b, t: (b, t, 0))
    return pl.pallas_call(
        _fourier_kernel,
        grid=(B, T // tf),
        in_specs=[tab_spec, tab_spec, seq_spec, seq_spec, tile_spec,
                  _const_spec((N_FGROUPS, FGROUP_CH, FGROUP_CH))],
        out_specs=tile_spec,
        out_shape=jax.ShapeDtypeStruct((B, T, FOURIER_DIM), BF16),
        compiler_params=pltpu.CompilerParams(
            dimension_semantics=("arbitrary", "arbitrary"), vmem_limit_bytes=VMEM_LIMIT_BYTES),
        name="fourier",
    )(ct, stn, a, bm, gb, wf)


def _out_proj_kernel(og_ref, fg_ref, sa_ref, sb_ref, x_ref, gate_ref, fng_ref, wpa_ref, wpb_ref, wo_ref, y_ref):
    ya = _mm(og_ref[...], wpa_ref[...])
    yb = _mm(fg_ref[...], wpb_ref[...])
    merged = sa_ref[...].astype(F32) * ya + sb_ref[...].astype(F32) * yb
    xo = x_ref[...] + gate_ref[...] * _mm(merged.astype(BF16), wo_ref[...])
    y_ref[...] = xo * lax.rsqrt(jnp.mean(xo * xo, axis=-1, keepdims=True) + EPS) * fng_ref[...]


def _out_proj(og, fg, sa, sb, x, gate, final_norm_g, wpa, wpb, wo):
    B, T, _ = x.shape
    per_batch_mod = gate.shape[0] > 1

    def tok_spec(cols):
        return pl.BlockSpec((None, TOKEN_TILE, cols), lambda b, t: (b, t, 0))

    mod_spec = pl.BlockSpec((None, 1, D_MODEL), (lambda b, t: (b, 0, 0)) if per_batch_mod else (lambda b, t: (0, 0, 0)))
    return pl.pallas_call(
        _out_proj_kernel,
        grid=(B, T // TOKEN_TILE),
        in_specs=[tok_spec(VDIM), tok_spec(FOURIER_DIM), tok_spec(D_MODEL), tok_spec(D_MODEL), tok_spec(D_MODEL),
                  mod_spec, _const_spec((1, D_MODEL)),
                  _const_spec((VDIM, D_MODEL)), _const_spec((FOURIER_DIM, D_MODEL)), _const_spec((D_MODEL, D_MODEL))],
        out_specs=tok_spec(D_MODEL),
        out_shape=jax.ShapeDtypeStruct((B, T, D_MODEL), F32),
        compiler_params=pltpu.CompilerParams(
            dimension_semantics=("arbitrary", "arbitrary"), vmem_limit_bytes=VMEM_LIMIT_BYTES),
        name="out_proj",
    )(og, fg, sa, sb, x, gate, final_norm_g, wpa, wpb, wo)


def _channel_dft_table():
    n = np.arange(FGROUP_CH)
    ang = 2.0 * np.pi * ((n[:, None] * n[None, :]) % FGROUP_CH) / FGROUP_CH
    tab = np.concatenate([np.cos(ang), np.sin(ang)], axis=1) / np.sqrt(FGROUP_CH)
    return jnp.asarray(tab, F32).astype(BF16)


def _time_dft_tables(T):
    lo = 32
    hi = T // lo
    k = np.arange(T)
    ang_hi = 2.0 * np.pi * ((k[:, None] * (np.arange(hi)[None, :] * lo)) % T) / T
    ang_lo = 2.0 * np.pi * ((k[:, None] * np.arange(lo)[None, :]) % T) / T
    scale = 1.0 / np.sqrt(T)
    ch = jnp.asarray(np.cos(ang_hi) * scale, F32)[:, :, None]
    sh = jnp.asarray(np.sin(ang_hi) * scale, F32)[:, :, None]
    cl = jnp.asarray(np.cos(ang_lo), F32)[:, None, :]
    sl = jnp.asarray(np.sin(ang_lo), F32)[:, None, :]
    ct = (ch * cl - sh * sl).reshape(T, T).astype(BF16)
    stn = (-(sh * cl + ch * sl)).reshape(T, T).astype(BF16)
    return ct, stn


def _rope_tables(T):
    rows = T // GRID_W
    r = jnp.repeat(jnp.arange(rows), GRID_W).astype(F32)
    c = jnp.tile(jnp.arange(GRID_W), rows).astype(F32)
    n_freq = HEAD_DK // 4
    freqs = ROPE_BASE ** (-jnp.arange(n_freq, dtype=F32) / n_freq)
    ang_r = r[:, None] * freqs
    ang_c = c[:, None] * freqs
    cos = jnp.concatenate([jnp.cos(ang_r), jnp.cos(ang_r), jnp.cos(ang_c), jnp.cos(ang_c)], axis=-1)
    sin = jnp.concatenate([-jnp.sin(ang_r), jnp.sin(ang_r), -jnp.sin(ang_c), jnp.sin(ang_c)], axis=-1)
    return cos, sin


def _path(x, scale, shift, gate, s0f, s0b, rope, heads, wts):
    T = x.shape[1]
    (qdf, kif, kef, qdb, kib, keb, dtf, dtb, v, ga, a, bm, gb, sa, sb) = _in_proj(
        x, scale, shift, wts["norm_g"], wts["w_main"], wts["w_r"], wts["w_ab"], wts["b_ab"], wts["cs_tab"], rope)
    og, sf, sbw = _gla(qdf, kif, kef, qdb, kib, keb, dtf, dtb, v, ga, s0f, s0b, wts["gla_norm_g"], heads)
    ct, stn = _time_dft_tables(T)
    fg = _fourier(ct, stn, a, bm, gb, wts["w_four"])
    y = _out_proj(og, fg, sa, sb, x, gate, wts["final_norm_g"], wts["w_proj_a"], wts["w_proj_b"], wts["w_out"])
    return y, sf, sbw


def kernel(x_prompt, x_sample, state_gla_fwd, state_gla_bwd, c, c_ctx, w_ada, b_ada, norm_g, w_in,
           w_alpha_fwd, b_alpha_fwd, w_alpha_bwd, b_alpha_bwd, gla_norm_g, w_four, w_proj_a, w_proj_b,
           w_out, final_norm_g):
    depth = w_in.shape[0]
    assert depth == 1, "single trunk layer"
    bp = x_prompt.shape[0]
    bs = x_sample.shape[0]

    n_cond = bs + 1
    cond_rows = -(-n_cond // 8) * 8
    cond = jnp.concatenate([c, c_ctx[None, :], jnp.zeros((cond_rows - n_cond, D_MODEL), F32)], axis=0)
    ada = _ada(cond, w_ada[0].astype(BF16), b_ada[0][None, :])
    shift, scale, gate = ada[:, :D_MODEL], ada[:, D_MODEL:2 * D_MODEL], ada[:, 2 * D_MODEL:]
    mod = lambda m, lo, hi: m[lo:hi][:, None, :]

    wi = w_in[0]
    o_q, o_k, o_v, o_ga = 0, KDIM, 2 * KDIM, 2 * KDIM + VDIM
    o_r = o_ga + VDIM
    o_u = o_r + 2 * GATE_RANK
    o_gb = o_u + FOURIER_DIM
    o_ma = o_gb + FOURIER_DIM
    w_main = jnp.concatenate([wi[:, o_q:o_r], wi[:, o_u:]], axis=1).astype(BF16)
    w_r = jnp.pad(wi[:, o_r:o_u], ((0, 0), (0, LANES - 2 * GATE_RANK))).astype(BF16)
    w_ab = jnp.zeros((LANES, 2 * KDIM), F32)
    w_ab = w_ab.at[:GATE_RANK, :KDIM].set(w_alpha_fwd[0]).at[GATE_RANK:2 * GATE_RANK, KDIM:].set(w_alpha_bwd[0])
    wts = dict(
        norm_g=norm_g[0][None, :], w_main=w_main, w_r=w_r, w_ab=w_ab.astype(BF16),
        b_ab=jnp.concatenate([b_alpha_fwd[0], b_alpha_bwd[0]])[None, :], cs_tab=_channel_dft_table(),
        gla_norm_g=gla_norm_g[0][None, :], w_four=w_four[0].astype(BF16),
        w_proj_a=w_proj_a[0].astype(BF16), w_proj_b=w_proj_b[0].astype(BF16), w_out=w_out[0].astype(BF16),
        final_norm_g=final_norm_g[None, :])
    del o_q, o_k, o_v, o_ma

    zeros = jnp.zeros((bp, N_HEADS, HEAD_DK, HEAD_DV), F32)
    y_prompt, sf, sb = _path(x_prompt, mod(scale, bs, bs + 1), mod(shift, bs, bs + 1), mod(gate, bs, bs + 1),
                             zeros, zeros, None, N_HEADS, wts)
    y_sample, _, _ = _path(x_sample, mod(scale, 0, bs), mod(shift, 0, bs), mod(gate, 0, bs),
                           state_gla_fwd[:, 0], state_gla_bwd[:, 0], _rope_tables(x_sample.shape[1]), 2, wts)
    return (y_prompt, y_sample, sf[:, None].astype(x_prompt.dtype), sb[:, None].astype(x_prompt.dtype))
```

```python
import functools

import numpy as np
import jax
import jax.numpy as jnp
from jax import lax
from jax.experimental import pallas as pl
from jax.experimental.pallas import tpu as pltpu

F32 = jnp.float32
BF16 = jnp.bfloat16

D_MODEL = 1024
N_HEADS = 4
HEAD_DK = 128
HEAD_DV = 256
KDIM = N_HEADS * HEAD_DK
VDIM = N_HEADS * HEAD_DV
GATE_RANK = 16
GATE_NORM = 16.0
CHUNK = 64
N_FGROUPS = 4
FGROUP_CH = 128
FOURIER_DIM = N_FGROUPS * FGROUP_CH
GRID_W = 64
ROPE_BASE = 10000.0
EPS = 1e-6

LANES = 128
TOKEN_TILE = 256
CHUNKS_PER_TILE = TOKEN_TILE // CHUNK
VMEM_LIMIT_BYTES = 56 * 1024 * 1024

_OFF_QK = 0
_OFF_V = _OFF_QK + 2 * KDIM
_OFF_GA = _OFF_V + VDIM
_OFF_U = _OFF_GA + VDIM
_OFF_GB = _OFF_U + FOURIER_DIM
_OFF_MA = _OFF_GB + FOURIER_DIM
_OFF_MB = _OFF_MA + D_MODEL
_W_MAIN_COLS = _OFF_MB + D_MODEL


def _mm(a, b):
    return jnp.dot(a, b, preferred_element_type=F32)


def _mm_ta(a, b):
    return lax.dot_general(a, b, (((0,), (0,)), ((), ())), preferred_element_type=F32)


def _mm_tb(a, b):
    return lax.dot_general(a, b, (((1,), (1,)), ((), ())), preferred_element_type=F32)


def _split_bf16(x):
    hi = x.astype(BF16)
    lo = (x - hi.astype(F32)).astype(BF16)
    return hi, lo


def _const_spec(shape):
    nd = len(shape)
    return pl.BlockSpec(shape, lambda *_: (0,) * nd)


def _ada_kernel(c_ref, w_ref, b_ref, o_ref):
    c = c_ref[...]
    s = c * jax.nn.sigmoid(c)
    o_ref[...] = _mm(s.astype(BF16), w_ref[...]) + b_ref[...]


def _ada(cond, w_bf, b):
    rows = cond.shape[0]
    return pl.pallas_call(
        _ada_kernel,
        out_shape=jax.ShapeDtypeStruct((rows, 3 * D_MODEL), F32),
        compiler_params=pltpu.CompilerParams(vmem_limit_bytes=VMEM_LIMIT_BYTES),
        name="ada",
    )(cond, w_bf, b)


def _in_proj_kernel(use_rope, *refs):
    if use_rope:
        (x_ref, sc_ref, sh_ref, ng_ref, w_ref, wr_ref, wab_ref, bab_ref, cs_ref, cos_ref, sin_ref,
         qdf_ref, kif_ref, kef_ref, qdb_ref, kib_ref, keb_ref, dtf_ref, dtb_ref,
         v_ref, ga_ref, a_ref, b_ref, gb_ref, sa_ref, sb_ref) = refs
    else:
        (x_ref, sc_ref, sh_ref, ng_ref, w_ref, wr_ref, wab_ref, bab_ref, cs_ref,
         qdf_ref, kif_ref, kef_ref, qdb_ref, kib_ref, keb_ref, dtf_ref, dtb_ref,
         v_ref, ga_ref, a_ref, b_ref, gb_ref, sa_ref, sb_ref) = refs

    x = x_ref[...]
    xn = x * lax.rsqrt(jnp.mean(x * x, axis=-1, keepdims=True) + EPS)
    h = (xn * ng_ref[...]) * (1.0 + sc_ref[...]) + sh_ref[...]
    hb = h.astype(BF16)

    v_ref[...] = _mm(hb, w_ref[:, _OFF_V:_OFF_V + VDIM]).astype(BF16)
    ga = _mm(hb, w_ref[:, _OFF_GA:_OFF_GA + VDIM])
    ga_ref[...] = (ga * jax.nn.sigmoid(ga)).astype(BF16)
    gb = _mm(hb, w_ref[:, _OFF_GB:_OFF_GB + FOURIER_DIM])
    gb_ref[...] = (gb * jax.nn.sigmoid(gb)).astype(BF16)
    sa_ref[...] = jax.nn.sigmoid(_mm(hb, w_ref[:, _OFF_MA:_OFF_MA + D_MODEL])).astype(BF16)
    sb_ref[...] = jax.nn.sigmoid(_mm(hb, w_ref[:, _OFF_MB:_OFF_MB + D_MODEL])).astype(BF16)

    ub = _mm(hb, w_ref[:, _OFF_U:_OFF_U + FOURIER_DIM]).astype(BF16)
    for g in range(N_FGROUPS):
        ab = _mm(ub[:, g * FGROUP_CH:(g + 1) * FGROUP_CH], cs_ref[...])
        a_ref[:, g * FGROUP_CH:(g + 1) * FGROUP_CH] = ab[:, :FGROUP_CH].astype(BF16)
        b_ref[:, g * FGROUP_CH:(g + 1) * FGROUP_CH] = ab[:, FGROUP_CH:].astype(BF16)

    qk = _mm(hb, w_ref[:, _OFF_QK:_OFF_QK + 2 * KDIM])
    q = qk[:, :KDIM] * (HEAD_DK ** -0.5)
    k = qk[:, KDIM:]
    if use_rope:
        cos = cos_ref[...]
        sin = sin_ref[...]
        lane = lax.broadcasted_iota(jnp.int32, (TOKEN_TILE, HEAD_DK), 1)
        first_half = (lane // (HEAD_DK // 4)) % 2 == 0

        def rope(t):
            outs = []
            for hh in range(N_HEADS):
                th = t[:, hh * HEAD_DK:(hh + 1) * HEAD_DK]
                partner = jnp.where(first_half,
                                    pltpu.roll(th, HEAD_DK - HEAD_DK // 4, axis=1),
                                    pltpu.roll(th, HEAD_DK // 4, axis=1))
                outs.append(th * cos + partner * sin)
            return jnp.concatenate(outs, axis=1)

        q = rope(q)
        k = rope(k)

    r = _mm(hb, wr_ref[...])
    xg = _mm(r.astype(BF16), wab_ref[...]) + bab_ref[...]
    g_all = (jnp.minimum(xg, 0.0) - jnp.log1p(jnp.exp(-jnp.abs(xg)))) * (1.0 / GATE_NORM)

    row = lax.broadcasted_iota(jnp.int32, (TOKEN_TILE, TOKEN_TILE), 0)
    col = lax.broadcasted_iota(jnp.int32, (TOKEN_TILE, TOKEN_TILE), 1)
    same_chunk = (row // CHUNK) == (col // CHUNK)
    ones_bd = jnp.where(same_chunk, 1.0, 0.0).astype(BF16)
    tok = lax.broadcasted_iota(jnp.int32, (TOKEN_TILE, LANES), 0)
    cidx = lax.broadcasted_iota(jnp.int32, (TOKEN_TILE, LANES), 1)
    chunk_ind = jnp.where(tok // CHUNK == cidx, 1.0, 0.0).astype(BF16)

    for direction, (qd_ref, ki_ref, ke_ref, dt_ref) in enumerate(
            ((qdf_ref, kif_ref, kef_ref, dtf_ref), (qdb_ref, kib_ref, keb_ref, dtb_ref))):
        g = g_all[:, direction * KDIM:(direction + 1) * KDIM]
        g_hi, g_lo = _split_bf16(g)
        tri = (col <= row) if direction == 0 else (col >= row)
        tri_bd = jnp.where(same_chunk & tri, 1.0, 0.0).astype(BF16)
        bc = _mm(tri_bd, g_hi) + _mm(tri_bd, g_lo)
        bl = _mm(ones_bd, g_hi) + _mm(ones_bd, g_lo)
        qd_ref[...] = (q * jnp.exp(bc)).astype(BF16)
        ki_ref[...] = (k * jnp.exp(-bc)).astype(BF16)
        ke_ref[...] = (k * jnp.exp(bl - bc)).astype(BF16)
        dt_ref[...] = jnp.exp(_mm_ta(g_hi, chunk_ind) + _mm_ta(g_lo, chunk_ind))


def _in_proj(x, scale, shift, norm_g, w_main, w_r, w_ab, b_ab, cs_tab, rope):
    B, T, _ = x.shape
    nt = T // TOKEN_TILE
    use_rope = rope is not None
    per_batch_mod = scale.shape[0] > 1

    def tok_spec(cols):
        return pl.BlockSpec((None, TOKEN_TILE, cols), lambda b, t: (b, t, 0))

    mod_spec = pl.BlockSpec((None, 1, D_MODEL), (lambda b, t: (b, 0, 0)) if per_batch_mod else (lambda b, t: (0, 0, 0)))
    in_specs = [
        tok_spec(D_MODEL), mod_spec, mod_spec, _const_spec((1, D_MODEL)),
        _const_spec((D_MODEL, _W_MAIN_COLS)), _const_spec((D_MODEL, LANES)),
        _const_spec((LANES, 2 * KDIM)), _const_spec((1, 2 * KDIM)),
        _const_spec((FGROUP_CH, 2 * FGROUP_CH)),
    ]
    args = [x, scale, shift, norm_g, w_main, w_r, w_ab, b_ab, cs_tab]
    if use_rope:
        in_specs += [pl.BlockSpec((TOKEN_TILE, HEAD_DK), lambda b, t: (t, 0))] * 2
        args += list(rope)

    dt_spec = pl.BlockSpec((None, None, KDIM, LANES), lambda b, t: (b, t, 0, 0))
    tok_bf = lambda cols: jax.ShapeDtypeStruct((B, T, cols), BF16)
    dt_shape = jax.ShapeDtypeStruct((B, nt, KDIM, LANES), F32)
    out_specs = [tok_spec(KDIM)] * 6 + [dt_spec] * 2 + [
        tok_spec(VDIM), tok_spec(VDIM), tok_spec(FOURIER_DIM), tok_spec(FOURIER_DIM),
        tok_spec(FOURIER_DIM), tok_spec(D_MODEL), tok_spec(D_MODEL)]
    out_shape = [tok_bf(KDIM)] * 6 + [dt_shape] * 2 + [
        tok_bf(VDIM), tok_bf(VDIM), tok_bf(FOURIER_DIM), tok_bf(FOURIER_DIM),
        tok_bf(FOURIER_DIM), tok_bf(D_MODEL), tok_bf(D_MODEL)]
    return pl.pallas_call(
        functools.partial(_in_proj_kernel, use_rope),
        grid=(B, nt),
        in_specs=in_specs,
        out_specs=out_specs,
        out_shape=out_shape,
        compiler_params=pltpu.CompilerParams(
            dimension_semantics=("arbitrary", "arbitrary"), vmem_limit_bytes=VMEM_LIMIT_BYTES),
        name="in_proj_rope" if use_rope else "in_proj",
    )(*args)


def _gla_kernel(heads, n_tiles,
                qdf_ref, kif_ref, kef_ref, qdb_ref, kib_ref, keb_ref, dtf_ref, dtb_ref,
                v_ref, ga_ref, s0f_ref, s0b_ref, gn_ref,
                og_ref, sf_ref, sb_ref,
                state, d_state, of_acc, ob_acc):
    state[0] = s0f_ref[...]
    state[1] = s0b_ref[...]
    dir_refs = ((qdf_ref, kif_ref, kef_ref, dtf_ref, of_acc), (qdb_ref, kib_ref, keb_ref, dtb_ref, ob_acc))

    def tile_body(j, carry):
        row = lax.broadcasted_iota(jnp.int32, (TOKEN_TILE, TOKEN_TILE), 0)
        col = lax.broadcasted_iota(jnp.int32, (TOKEN_TILE, TOKEN_TILE), 1)
        same_chunk = (row // CHUNK) == (col // CHUNK)
        causal = (same_chunk & (col <= row), same_chunk & (col >= row))
        tiles = (j, n_tiles - 1 - j)
        bases = tuple(pl.multiple_of(t * TOKEN_TILE, TOKEN_TILE) for t in tiles)

        for direction in range(2):
            qd_ref, ki_ref, ke_ref, _, o_acc = dir_refs[direction]
            trows = pl.ds(bases[direction], TOKEN_TILE)
            for hh in range(heads):
                kc = slice(hh * HEAD_DK, (hh + 1) * HEAD_DK)
                vc = slice(hh * HEAD_DV, (hh + 1) * HEAD_DV)
                scores = jnp.where(causal[direction], _mm_tb(qd_ref[trows, kc], ki_ref[trows, kc]), 0.0)
                o_acc[trows, vc] = _mm(scores.astype(BF16), v_ref[trows, vc])
                for c in range(CHUNKS_PER_TILE):
                    rows = pl.ds(bases[direction] + c * CHUNK, CHUNK)
                    d_state[direction, hh, c] = _mm_ta(ke_ref[rows, kc], v_ref[rows, vc])

        for step in range(CHUNKS_PER_TILE):
            for direction in range(2):
                qd_ref, _, _, dt_ref, o_acc = dir_refs[direction]
                c = step if direction == 0 else CHUNKS_PER_TILE - 1 - step
                rows = pl.ds(bases[direction] + c * CHUNK, CHUNK)
                for hh in range(heads):
                    kc = slice(hh * HEAD_DK, (hh + 1) * HEAD_DK)
                    vc = slice(hh * HEAD_DV, (hh + 1) * HEAD_DV)
                    s = state[direction, hh]
                    o_acc[rows, vc] += _mm(qd_ref[rows, kc], s.astype(BF16))
                    dec = dt_ref[tiles[direction], kc, c:c + 1]
                    state[direction, hh] = dec * s + d_state[direction, hh, c]
        return carry

    lax.fori_loop(0, n_tiles, tile_body, 0)

    def finish(j, carry):
        rows = pl.ds(pl.multiple_of(j * TOKEN_TILE, TOKEN_TILE), TOKEN_TILE)
        for hh in range(heads):
            vc = slice(hh * HEAD_DV, (hh + 1) * HEAD_DV)
            o = of_acc[rows, vc] + ob_acc[rows, vc]
            on = o * lax.rsqrt(jnp.mean(o * o, axis=-1, keepdims=True) + EPS) * gn_ref[...]
            og_ref[rows, vc] = (on * ga_ref[rows, vc].astype(F32)).astype(BF16)
        return carry

    lax.fori_loop(0, n_tiles, finish, 0)
    sf_ref[...] = state[0]
    sb_ref[...] = state[1]


def _gla(qdf, kif, kef, qdb, kib, keb, dtf, dtb, v, ga, s0f, s0b, gla_norm_g, heads):
    B, T, _ = v.shape
    nt = T // TOKEN_TILE
    hg = N_HEADS // heads
    k_spec = pl.BlockSpec((None, T, heads * HEAD_DK), lambda b, h: (b, 0, h))
    v_spec = pl.BlockSpec((None, T, heads * HEAD_DV), lambda b, h: (b, 0, h))
    dt_spec = pl.BlockSpec((None, nt, heads * HEAD_DK, LANES), lambda b, h: (b, 0, h, 0))
    s_spec = pl.BlockSpec((None, heads, HEAD_DK, HEAD_DV), lambda b, h: (b, h, 0, 0))
    s_shape = jax.ShapeDtypeStruct((B, N_HEADS, HEAD_DK, HEAD_DV), F32)
    return pl.pallas_call(
        functools.partial(_gla_kernel, heads, nt),
        grid=(B, hg),
        in_specs=[k_spec] * 6 + [dt_spec] * 2 + [v_spec, v_spec, s_spec, s_spec, _const_spec((1, HEAD_DV))],
        out_specs=[v_spec, s_spec, s_spec],
        out_shape=[jax.ShapeDtypeStruct((B, T, VDIM), BF16), s_shape, s_shape],
        scratch_shapes=[pltpu.VMEM((2, heads, HEAD_DK, HEAD_DV), F32),
                        pltpu.VMEM((2, heads, CHUNKS_PER_TILE, HEAD_DK, HEAD_DV), F32),
                        pltpu.VMEM((T, heads * HEAD_DV), F32),
                        pltpu.VMEM((T, heads * HEAD_DV), F32)],
        compiler_params=pltpu.CompilerParams(
            dimension_semantics=("arbitrary", "arbitrary"), vmem_limit_bytes=VMEM_LIMIT_BYTES),
        name="gla",
    )(qdf, kif, kef, qdb, kib, keb, dtf, dtb, v, ga, s0f, s0b, gla_norm_g)


def _fourier_kernel(ct_ref, st_ref, a_ref, b_ref, gb_ref, wf_ref, o_ref):
    f = (_mm(ct_ref[...], a_ref[...]) + _mm(st_ref[...], b_ref[...])).astype(BF16)
    for g in range(N_FGROUPS):
        cols = slice(g * FGROUP_CH, (g + 1) * FGROUP_CH)
        o_ref[:, cols] = (_mm(f[:, cols], wf_ref[g]) * gb_ref[:, cols].astype(F32)).astype(BF16)


def _fourier(ct, stn, a, bm, gb, wf):
    B, T, _ = a.shape
    tf = min(T, 512)
    tab_spec = pl.BlockSpec((tf, T), lambda b, t: (t, 0))
    seq_spec = pl.BlockSpec((None, T, FOURIER_DIM), lambda b, t: (b, 0, 0))
    tile_spec = pl.BlockSpec((None, tf, FOURIER_DIM), lambda b, t: (b, t, 0))
    return pl.pallas_call(
        _fourier_kernel,
        grid=(B, T // tf),
        in_specs=[tab_spec, tab_spec, seq_spec, seq_spec, tile_spec,
                  _const_spec((N_FGROUPS, FGROUP_CH, FGROUP_CH))],
        out_specs=tile_spec,
        out_shape=jax.ShapeDtypeStruct((B, T, FOURIER_DIM), BF16),
        compiler_params=pltpu.CompilerParams(
            dimension_semantics=("arbitrary", "arbitrary"), vmem_limit_bytes=VMEM_LIMIT_BYTES),
        name="fourier",
    )(ct, stn, a, bm, gb, wf)


def _out_proj_kernel(og_ref, fg_ref, sa_ref, sb_ref, x_ref, gate_ref, fng_ref, wpa_ref, wpb_ref, wo_ref, y_ref):
    ya = _mm(og_ref[...], wpa_ref[...])
    yb = _mm(fg_ref[...], wpb_ref[...])
    merged = sa_ref[...].astype(F32) * ya + sb_ref[...].astype(F32) * yb
    xo = x_ref[...] + gate_ref[...] * _mm(merged.astype(BF16), wo_ref[...])
    y_ref[...] = xo * lax.rsqrt(jnp.mean(xo * xo, axis=-1, keepdims=True) + EPS) * fng_ref[...]


def _out_proj(og, fg, sa, sb, x, gate, final_norm_g, wpa, wpb, wo):
    B, T, _ = x.shape
    per_batch_mod = gate.shape[0] > 1

    def tok_spec(cols):
        return pl.BlockSpec((None, TOKEN_TILE, cols), lambda b, t: (b, t, 0))

    mod_spec = pl.BlockSpec((None, 1, D_MODEL), (lambda b, t: (b, 0, 0)) if per_batch_mod else (lambda b, t: (0, 0, 0)))
    return pl.pallas_call(
        _out_proj_kernel,
        grid=(B, T // TOKEN_TILE),
        in_specs=[tok_spec(VDIM), tok_spec(FOURIER_DIM), tok_spec(D_MODEL), tok_spec(D_MODEL), tok_spec(D_MODEL),
                  mod_spec, _const_spec((1, D_MODEL)),
                  _const_spec((VDIM, D_MODEL)), _const_spec((FOURIER_DIM, D_MODEL)), _const_spec((D_MODEL, D_MODEL))],
        out_specs=tok_spec(D_MODEL),
        out_shape=jax.ShapeDtypeStruct((B, T, D_MODEL), F32),
        compiler_params=pltpu.CompilerParams(
            dimension_semantics=("arbitrary", "arbitrary"), vmem_limit_bytes=VMEM_LIMIT_BYTES),
        name="out_proj",
    )(og, fg, sa, sb, x, gate, final_norm_g, wpa, wpb, wo)


def _channel_dft_table():
    n = np.arange(FGROUP_CH)
    ang = 2.0 * np.pi * ((n[:, None] * n[None, :]) % FGROUP_CH) / FGROUP_CH
    tab = np.concatenate([np.cos(ang), np.sin(ang)], axis=1) / np.sqrt(FGROUP_CH)
    return jnp.asarray(tab, F32).astype(BF16)


def _time_dft_tables(T):
    lo = 32
    hi = T // lo
    k = np.arange(T)
    ang_hi = 2.0 * np.pi * ((k[:, None] * (np.arange(hi)[None, :] * lo)) % T) / T
    ang_lo = 2.0 * np.pi * ((k[:, None] * np.arange(lo)[None, :]) % T) / T
    scale = 1.0 / np.sqrt(T)
    ch = jnp.asarray(np.cos(ang_hi) * scale, F32)[:, :, None]
    sh = jnp.asarray(np.sin(ang_hi) * scale, F32)[:, :, None]
    cl = jnp.asarray(np.cos(ang_lo), F32)[:, None, :]
    sl = jnp.asarray(np.sin(ang_lo), F32)[:, None, :]
    ct = (ch * cl - sh * sl).reshape(T, T).astype(BF16)
    stn = (-(sh * cl + ch * sl)).reshape(T, T).astype(BF16)
    return ct, stn


def _rope_tables(T):
    rows = T // GRID_W
    r = jnp.repeat(jnp.arange(rows), GRID_W).astype(F32)
    c = jnp.tile(jnp.arange(GRID_W), rows).astype(F32)
    n_freq = HEAD_DK // 4
    freqs = ROPE_BASE ** (-jnp.arange(n_freq, dtype=F32) / n_freq)
    ang_r = r[:, None] * freqs
    ang_c = c[:, None] * freqs
    cos = jnp.concatenate([jnp.cos(ang_r), jnp.cos(ang_r), jnp.cos(ang_c), jnp.cos(ang_c)], axis=-1)
    sin = jnp.concatenate([-jnp.sin(ang_r), jnp.sin(ang_r), -jnp.sin(ang_c), jnp.sin(ang_c)], axis=-1)
    return cos, sin


def _path(x, scale, shift, gate, s0f, s0b, rope, heads, wts):
    T = x.shape[1]
    (qdf, kif, kef, qdb, kib, keb, dtf, dtb, v, ga, a, bm, gb, sa, sb) = _in_proj(
        x, scale, shift, wts["norm_g"], wts["w_main"], wts["w_r"], wts["w_ab"], wts["b_ab"], wts["cs_tab"], rope)
    og, sf, sbw = _gla(qdf, kif, kef, qdb, kib, keb, dtf, dtb, v, ga, s0f, s0b, wts["gla_norm_g"], heads)
    ct, stn = _time_dft_tables(T)
    fg = _fourier(ct, stn, a, bm, gb, wts["w_four"])
    y = _out_proj(og, fg, sa, sb, x, gate, wts["final_norm_g"], wts["w_proj_a"], wts["w_proj_b"], wts["w_out"])
    return y, sf, sbw


def kernel(x_prompt, x_sample, state_gla_fwd, state_gla_bwd, c, c_ctx, w_ada, b_ada, norm_g, w_in,
           w_alpha_fwd, b_alpha_fwd, w_alpha_bwd, b_alpha_bwd, gla_norm_g, w_four, w_proj_a, w_proj_b,
           w_out, final_norm_g):
    depth = w_in.shape[0]
    assert depth == 1, "single trunk layer"
    bp = x_prompt.shape[0]
    bs = x_sample.shape[0]

    n_cond = bs + 1
    cond_rows = -(-n_cond // 8) * 8
    cond = jnp.concatenate([c, c_ctx[None, :], jnp.zeros((cond_rows - n_cond, D_MODEL), F32)], axis=0)
    ada = _ada(cond, w_ada[0].astype(BF16), b_ada[0][None, :])
    shift, scale, gate = ada[:, :D_MODEL], ada[:, D_MODEL:2 * D_MODEL], ada[:, 2 * D_MODEL:]
    mod = lambda m, lo, hi: m[lo:hi][:, None, :]

    wi = w_in[0]
    o_q, o_k, o_v, o_ga = 0, KDIM, 2 * KDIM, 2 * KDIM + VDIM
    o_r = o_ga + VDIM
    o_u = o_r + 2 * GATE_RANK
    o_gb = o_u + FOURIER_DIM
    o_ma = o_gb + FOURIER_DIM
    w_main = jnp.concatenate([wi[:, o_q:o_r], wi[:, o_u:]], axis=1).astype(BF16)
    w_r = jnp.pad(wi[:, o_r:o_u], ((0, 0), (0, LANES - 2 * GATE_RANK))).astype(BF16)
    w_ab = jnp.zeros((LANES, 2 * KDIM), F32)
    w_ab = w_ab.at[:GATE_RANK, :KDIM].set(w_alpha_fwd[0]).at[GATE_RANK:2 * GATE_RANK, KDIM:].set(w_alpha_bwd[0])
    wts = dict(
        norm_g=norm_g[0][None, :], w_main=w_main, w_r=w_r, w_ab=w_ab.astype(BF16),
        b_ab=jnp.concatenate([b_alpha_fwd[0], b_alpha_bwd[0]])[None, :], cs_tab=_channel_dft_table(),
        gla_norm_g=gla_norm_g[0][None, :], w_four=w_four[0].astype(BF16),
        w_proj_a=w_proj_a[0].astype(BF16), w_proj_b=w_proj_b[0].astype(BF16), w_out=w_out[0].astype(BF16),
        final_norm_g=final_norm_g[None, :])
    del o_q, o_k, o_v, o_ma

    zeros = jnp.zeros((bp, N_HEADS, HEAD_DK, HEAD_DV), F32)
    y_prompt, sf, sb = _path(x_prompt, mod(scale, bs, bs + 1), mod(shift, bs, bs + 1), mod(gate, bs, bs + 1),
                             zeros, zeros, None, N_HEADS, wts)
    y_sample, _, _ = _path(x_sample, mod(scale, 0, bs), mod(shift, 0, bs), mod(gate, 0, bs),
                           state_gla_fwd[:, 0], state_gla_bwd[:, 0], _rope_tables(x_sample.shape[1]), 2, wts)
    return (y_prompt, y_sample, sf[:, None].astype(x_prompt.dtype), sb[:, None].astype(x_prompt.dtype))
```

```python
import functools

import numpy as np
import jax
import jax.numpy as jnp
from jax import lax
from jax.experimental import pallas as pl
from jax.experimental.pallas import tpu as pltpu

F32 = jnp.float32
BF16 = jnp.bfloat16

D_MODEL = 1024
N_HEADS = 4
HEAD_DK = 128
HEAD_DV = 256
KDIM = N_HEADS * HEAD_DK
VDIM = N_HEADS * HEAD_DV
GATE_RANK = 16
GATE_NORM = 16.0
CHUNK = 64
N_FGROUPS = 4
FGROUP_CH = 128
FOURIER_DIM = N_FGROUPS * FGROUP_CH
GRID_W = 64
ROPE_BASE = 10000.0
EPS = 1e-6

LANES = 128
TOKEN_TILE = 256
CHUNKS_PER_TILE = TOKEN_TILE // CHUNK
VMEM_LIMIT_BYTES = 56 * 1024 * 1024

_OFF_QK = 0
_OFF_V = _OFF_QK + 2 * KDIM
_OFF_GA = _OFF_V + VDIM
_W_HEAD_COLS = _OFF_GA + VDIM
_OFF_U = 0
_OFF_GB = _OFF_U + FOURIER_DIM
_OFF_MA = _OFF_GB + FOURIER_DIM
_OFF_MB = _OFF_MA + D_MODEL
_W_TAIL_COLS = _OFF_MB + D_MODEL


def _mm(a, b):
    return jnp.dot(a, b, preferred_element_type=F32)


def _mm_ta(a, b):
    return lax.dot_general(a, b, (((0,), (0,)), ((), ())), preferred_element_type=F32)


def _mm_tb(a, b):
    return lax.dot_general(a, b, (((1,), (1,)), ((), ())), preferred_element_type=F32)


def _split_bf16(x):
    hi = x.astype(BF16)
    lo = (x - hi.astype(F32)).astype(BF16)
    return hi, lo


def _const_spec(shape):
    nd = len(shape)
    return pl.BlockSpec(shape, lambda *_: (0,) * nd)


def _ada_kernel(c_ref, w_ref, b_ref, o_ref):
    c = c_ref[...]
    s = c * jax.nn.sigmoid(c)
    o_ref[...] = _mm(s.astype(BF16), w_ref[...]) + b_ref[...]


def _ada(cond, w_bf, b):
    rows = cond.shape[0]
    return pl.pallas_call(
        _ada_kernel,
        out_shape=jax.ShapeDtypeStruct((rows, 3 * D_MODEL), F32),
        compiler_params=pltpu.CompilerParams(vmem_limit_bytes=VMEM_LIMIT_BYTES),
        name="ada",
    )(cond, w_bf, b)


def _in_proj_kernel(use_rope, *refs):
    if use_rope:
        (x_ref, sc_ref, sh_ref, ng_ref, wa_ref, wb_ref, wr_ref, wab_ref, bab_ref, cs_ref, cos_ref, sin_ref,
         qdf_ref, kif_ref, kef_ref, qdb_ref, kib_ref, keb_ref, dtf_ref, dtb_ref,
         v_ref, ga_ref, a_ref, b_ref, gb_ref, sa_ref, sb_ref) = refs
    else:
        (x_ref, sc_ref, sh_ref, ng_ref, wa_ref, wb_ref, wr_ref, wab_ref, bab_ref, cs_ref,
         qdf_ref, kif_ref, kef_ref, qdb_ref, kib_ref, keb_ref, dtf_ref, dtb_ref,
         v_ref, ga_ref, a_ref, b_ref, gb_ref, sa_ref, sb_ref) = refs

    x = x_ref[...]
    xn = x * lax.rsqrt(jnp.mean(x * x, axis=-1, keepdims=True) + EPS)
    h = (xn * ng_ref[...]) * (1.0 + sc_ref[...]) + sh_ref[...]
    hb = h.astype(BF16)

    r = _mm(hb, wr_ref[...])
    qk = _mm(hb, wa_ref[:, _OFF_QK:_OFF_QK + 2 * KDIM])
    xg = _mm(r.astype(BF16), wab_ref[...]) + bab_ref[...]
    v_ref[...] = _mm(hb, wa_ref[:, _OFF_V:_OFF_V + VDIM]).astype(BF16)
    g_all = (jnp.minimum(xg, 0.0) - jnp.log1p(jnp.exp(-jnp.abs(xg)))) * (1.0 / GATE_NORM)

    q = qk[:, :KDIM] * (HEAD_DK ** -0.5)
    k = qk[:, KDIM:]
    if use_rope:
        cos = cos_ref[...]
        sin = sin_ref[...]
        lane = lax.broadcasted_iota(jnp.int32, (TOKEN_TILE, HEAD_DK), 1)
        first_half = (lane // (HEAD_DK // 4)) % 2 == 0

        def rope(t):
            outs = []
            for hh in range(N_HEADS):
                th = t[:, hh * HEAD_DK:(hh + 1) * HEAD_DK]
                partner = jnp.where(first_half,
                                    pltpu.roll(th, HEAD_DK - HEAD_DK // 4, axis=1),
                                    pltpu.roll(th, HEAD_DK // 4, axis=1))
                outs.append(th * cos + partner * sin)
            return jnp.concatenate(outs, axis=1)

        q = rope(q)
        k = rope(k)

    ub = _mm(hb, wb_ref[:, _OFF_U:_OFF_U + FOURIER_DIM]).astype(BF16)

    row = lax.broadcasted_iota(jnp.int32, (TOKEN_TILE, TOKEN_TILE), 0)
    col = lax.broadcasted_iota(jnp.int32, (TOKEN_TILE, TOKEN_TILE), 1)
    same_chunk = (row // CHUNK) == (col // CHUNK)
    bcs = []
    for direction in range(2):
        g_hi, g_lo = _split_bf16(g_all[:, direction * KDIM:(direction + 1) * KDIM])
        tri = (col <= row) if direction == 0 else (col >= row)
        tri_bd = jnp.where(same_chunk & tri, 1.0, 0.0).astype(BF16)
        bcs.append(_mm(tri_bd, g_hi) + _mm(tri_bd, g_lo))

    ga = _mm(hb, wa_ref[:, _OFF_GA:_OFF_GA + VDIM])
    ga_ref[...] = (ga * jax.nn.sigmoid(ga)).astype(BF16)

    n_parts = 4 * CHUNKS_PER_TILE
    sel_r = lax.broadcasted_iota(jnp.int32, (n_parts, LANES), 0)
    sel_c = lax.broadcasted_iota(jnp.int32, (n_parts, LANES), 1)
    part_sel = jnp.where((sel_r % CHUNKS_PER_TILE == sel_c) & (sel_r < 3 * CHUNKS_PER_TILE), 1.0, 0.0).astype(BF16)
    for direction, (qd_ref, ki_ref, ke_ref, dt_ref) in enumerate(
            ((qdf_ref, kif_ref, kef_ref, dtf_ref), (qdb_ref, kib_ref, keb_ref, dtb_ref))):
        bc = bcs[direction]
        edge = CHUNK - 1 if direction == 0 else 0
        bl_rows = bc.reshape(CHUNKS_PER_TILE, CHUNK, KDIM)[:, edge, :]
        bl = jnp.broadcast_to(bl_rows[:, None, :], (CHUNKS_PER_TILE, CHUNK, KDIM)).reshape(TOKEN_TILE, KDIM)
        qd_ref[...] = (q * jnp.exp(bc)).astype(BF16)
        ki_ref[...] = (k * jnp.exp(-bc)).astype(BF16)
        ke_ref[...] = (k * jnp.exp(bl - bc)).astype(BF16)
        p0 = bl_rows.astype(BF16).astype(F32)
        p1 = (bl_rows - p0).astype(BF16).astype(F32)
        p2 = (bl_rows - p0 - p1).astype(BF16).astype(F32)
        parts = jnp.concatenate([p0, p1, p2, jnp.zeros_like(p0)], axis=0).astype(BF16)
        dt_ref[...] = jnp.exp(_mm_ta(parts, part_sel))

    for g in range(N_FGROUPS):
        ab = _mm(ub[:, g * FGROUP_CH:(g + 1) * FGROUP_CH], cs_ref[...])
        a_ref[:, g * FGROUP_CH:(g + 1) * FGROUP_CH] = ab[:, :FGROUP_CH].astype(BF16)
        b_ref[:, g * FGROUP_CH:(g + 1) * FGROUP_CH] = ab[:, FGROUP_CH:].astype(BF16)

    gb = _mm(hb, wb_ref[:, _OFF_GB:_OFF_GB + FOURIER_DIM])
    gb_ref[...] = (gb * jax.nn.sigmoid(gb)).astype(BF16)
    sa_ref[...] = jax.nn.sigmoid(_mm(hb, wb_ref[:, _OFF_MA:_OFF_MA + D_MODEL])).astype(BF16)
    sb_ref[...] = jax.nn.sigmoid(_mm(hb, wb_ref[:, _OFF_MB:_OFF_MB + D_MODEL])).astype(BF16)


def _in_proj(x, scale, shift, norm_g, w_head, w_tail, w_r, w_ab, b_ab, cs_tab, rope):
    B, T, _ = x.shape
    nt = T // TOKEN_TILE
    use_rope = rope is not None
    per_batch_mod = scale.shape[0] > 1

    def tok_spec(cols):
        return pl.BlockSpec((None, TOKEN_TILE, cols), lambda b, t: (b, t, 0))

    mod_spec = pl.BlockSpec((None, 1, D_MODEL), (lambda b, t: (b, 0, 0)) if per_batch_mod else (lambda b, t: (0, 0, 0)))
    in_specs = [
        tok_spec(D_MODEL), mod_spec, mod_spec, _const_spec((1, D_MODEL)),
        _const_spec((D_MODEL, _W_HEAD_COLS)), _const_spec((D_MODEL, _W_TAIL_COLS)), _const_spec((D_MODEL, LANES)),
        _const_spec((LANES, 2 * KDIM)), _const_spec((1, 2 * KDIM)),
        _const_spec((FGROUP_CH, 2 * FGROUP_CH)),
    ]
    args = [x, scale, shift, norm_g, w_head, w_tail, w_r, w_ab, b_ab, cs_tab]
    if use_rope:
        in_specs += [pl.BlockSpec((TOKEN_TILE, HEAD_DK), lambda b, t: (t, 0))] * 2
        args += list(rope)

    dt_spec = pl.BlockSpec((None, None, KDIM, LANES), lambda b, t: (b, t, 0, 0))
    tok_bf = lambda cols: jax.ShapeDtypeStruct((B, T, cols), BF16)
    dt_shape = jax.ShapeDtypeStruct((B, nt, KDIM, LANES), F32)
    out_specs = [tok_spec(KDIM)] * 6 + [dt_spec] * 2 + [
        tok_spec(VDIM), tok_spec(VDIM), tok_spec(FOURIER_DIM), tok_spec(FOURIER_DIM),
        tok_spec(FOURIER_DIM), tok_spec(D_MODEL), tok_spec(D_MODEL)]
    out_shape = [tok_bf(KDIM)] * 6 + [dt_shape] * 2 + [
        tok_bf(VDIM), tok_bf(VDIM), tok_bf(FOURIER_DIM), tok_bf(FOURIER_DIM),
        tok_bf(FOURIER_DIM), tok_bf(D_MODEL), tok_bf(D_MODEL)]
    return pl.pallas_call(
        functools.partial(_in_proj_kernel, use_rope),
        grid=(B, nt),
        in_specs=in_specs,
        out_specs=out_specs,
        out_shape=out_shape,
        compiler_params=pltpu.CompilerParams(
            dimension_semantics=("arbitrary", "arbitrary"), vmem_limit_bytes=VMEM_LIMIT_BYTES),
        name="in_proj_rope" if use_rope else "in_proj",
    )(*args)


def _gla_kernel(heads, n_tiles, has_init, *refs):
    (qdf_ref, kif_ref, kef_ref, qdb_ref, kib_ref, keb_ref, dtf_ref, dtb_ref, v_ref, ga_ref) = refs[:10]
    refs = refs[10:]
    if has_init:
        s0f_ref, s0b_ref = refs[:2]
        refs = refs[2:]
    gn_ref, og_ref, sf_ref, sb_ref, state, d_state, of_acc, ob_acc = refs
    if has_init:
        state[0] = s0f_ref[...]
        state[1] = s0b_ref[...]
    else:
        state[...] = jnp.zeros_like(state)
    dir_refs = ((qdf_ref, kif_ref, kef_ref, dtf_ref, of_acc), (qdb_ref, kib_ref, keb_ref, dtb_ref, ob_acc))

    def tile_body(j, carry):
        row = lax.broadcasted_iota(jnp.int32, (TOKEN_TILE, TOKEN_TILE), 0)
        col = lax.broadcasted_iota(jnp.int32, (TOKEN_TILE, TOKEN_TILE), 1)
        same_chunk = (row // CHUNK) == (col // CHUNK)
        causal = (same_chunk & (col <= row), same_chunk & (col >= row))
        tiles = (j, n_tiles - 1 - j)
        bases = tuple(pl.multiple_of(t * TOKEN_TILE, TOKEN_TILE) for t in tiles)

        for direction in range(2):
            qd_ref, ki_ref, ke_ref, _, o_acc = dir_refs[direction]
            trows = pl.ds(bases[direction], TOKEN_TILE)
            for hh in range(heads):
                kc = slice(hh * HEAD_DK, (hh + 1) * HEAD_DK)
                vc = slice(hh * HEAD_DV, (hh + 1) * HEAD_DV)
                scores = jnp.where(causal[direction], _mm_tb(qd_ref[trows, kc], ki_ref[trows, kc]), 0.0)
                o_acc[trows, vc] = _mm(scores.astype(BF16), v_ref[trows, vc])
                for c in range(CHUNKS_PER_TILE):
                    rows = pl.ds(bases[direction] + c * CHUNK, CHUNK)
                    d_state[direction, hh, c] = _mm_ta(ke_ref[rows, kc], v_ref[rows, vc])

        for step in range(CHUNKS_PER_TILE):
            for direction in range(2):
                qd_ref, _, _, dt_ref, o_acc = dir_refs[direction]
                c = step if direction == 0 else CHUNKS_PER_TILE - 1 - step
                rows = pl.ds(bases[direction] + c * CHUNK, CHUNK)
                for hh in range(heads):
                    kc = slice(hh * HEAD_DK, (hh + 1) * HEAD_DK)
                    vc = slice(hh * HEAD_DV, (hh + 1) * HEAD_DV)
                    s = state[direction, hh]
                    o_acc[rows, vc] += _mm(qd_ref[rows, kc], s.astype(BF16))
                    dec = dt_ref[tiles[direction], kc, c:c + 1]
                    state[direction, hh] = dec * s + d_state[direction, hh, c]
        return carry

    lax.fori_loop(0, n_tiles, tile_body, 0)

    def finish(j, carry):
        rows = pl.ds(pl.multiple_of(j * TOKEN_TILE, TOKEN_TILE), TOKEN_TILE)
        for hh in range(heads):
            vc = slice(hh * HEAD_DV, (hh + 1) * HEAD_DV)
            o = of_acc[rows, vc] + ob_acc[rows, vc]
            on = o * lax.rsqrt(jnp.mean(o * o, axis=-1, keepdims=True) + EPS) * gn_ref[...]
            og_ref[rows, vc] = (on * ga_ref[rows, vc].astype(F32)).astype(BF16)
        return carry

    lax.fori_loop(0, n_tiles, finish, 0)
    sf_ref[...] = state[0]
    sb_ref[...] = state[1]


def _gla(qdf, kif, kef, qdb, kib, keb, dtf, dtb, v, ga, init_states, gla_norm_g, heads):
    B, T, _ = v.shape
    nt = T // TOKEN_TILE
    hg = N_HEADS // heads
    has_init = init_states is not None
    k_spec = pl.BlockSpec((None, T, heads * HEAD_DK), lambda b, h: (b, 0, h))
    v_spec = pl.BlockSpec((None, T, heads * HEAD_DV), lambda b, h: (b, 0, h))
    dt_spec = pl.BlockSpec((None, nt, heads * HEAD_DK, LANES), lambda b, h: (b, 0, h, 0))
    s_spec = pl.BlockSpec((None, heads, HEAD_DK, HEAD_DV), lambda b, h: (b, h, 0, 0))
    s_shape = jax.ShapeDtypeStruct((B, N_HEADS, HEAD_DK, HEAD_DV), F32)
    return pl.pallas_call(
        functools.partial(_gla_kernel, heads, nt, has_init),
        grid=(B, hg),
        in_specs=([k_spec] * 6 + [dt_spec] * 2 + [v_spec, v_spec] + [s_spec] * (2 if has_init else 0)
                  + [_const_spec((1, HEAD_DV))]),
        out_specs=[v_spec, s_spec, s_spec],
        out_shape=[jax.ShapeDtypeStruct((B, T, VDIM), BF16), s_shape, s_shape],
        scratch_shapes=[pltpu.VMEM((2, heads, HEAD_DK, HEAD_DV), F32),
                        pltpu.VMEM((2, heads, CHUNKS_PER_TILE, HEAD_DK, HEAD_DV), F32),
                        pltpu.VMEM((T, heads * HEAD_DV), F32),
                        pltpu.VMEM((T, heads * HEAD_DV), F32)],
        compiler_params=pltpu.CompilerParams(
            dimension_semantics=("arbitrary", "arbitrary"), vmem_limit_bytes=VMEM_LIMIT_BYTES),
        name="gla",
    )(qdf, kif, kef, qdb, kib, keb, dtf, dtb, v, ga, *(init_states or ()), gla_norm_g)


def _fourier_kernel(ct_ref, st_ref, a_ref, b_ref, gb_ref, wf_ref, o_ref):
    f = (_mm(ct_ref[...], a_ref[...]) + _mm(st_ref[...], b_ref[...])).astype(BF16)
    for g in range(N_FGROUPS):
        cols = slice(g * FGROUP_CH, (g + 1) * FGROUP_CH)
        o_ref[:, cols] = (_mm(f[:, cols], wf_ref[g]) * gb_ref[:, cols].astype(F32)).astype(BF16)


def _fourier(ct, stn, a, bm, gb, wf):
    B, T, _ = a.shape
    tf = min(T, 512)
    tab_spec = pl.BlockSpec((tf, T), lambda b, t: (t, 0))
    seq_spec = pl.BlockSpec((None, T, FOURIER_DIM), lambda b, t: (b, 0, 0))
    tile_spec = pl.BlockSpec((None, tf, FOURIER_DIM), lambda b, t: (b, t, 0))
    return pl.pallas_call(
        _fourier_kernel,
        grid=(B, T // tf),
        in_specs=[tab_spec, tab_spec, seq_spec, seq_spec, tile_spec,
                  _const_spec((N_FGROUPS, FGROUP_CH, FGROUP_CH))],
        out_specs=tile_spec,
        out_shape=jax.ShapeDtypeStruct((B, T, FOURIER_DIM), BF16),
        compiler_params=pltpu.CompilerParams(
            dimension_semantics=("arbitrary", "arbitrary"), vmem_limit_bytes=VMEM_LIMIT_BYTES),
        name="fourier",
    )(ct, stn, a, bm, gb, wf)


def _out_proj_kernel(og_ref, fg_ref, sa_ref, sb_ref, x_ref, gate_ref, fng_ref, wpa_ref, wpb_ref, wo_ref, y_ref):
    ya = _mm(og_ref[...], wpa_ref[...])
    yb = _mm(fg_ref[...], wpb_ref[...])
    merged = sa_ref[...].astype(F32) * ya + sb_ref[...].astype(F32) * yb
    xo = x_ref[...] + gate_ref[...] * _mm(merged.astype(BF16), wo_ref[...])
    y_ref[...] = xo * lax.rsqrt(jnp.mean(xo * xo, axis=-1, keepdims=True) + EPS) * fng_ref[...]


def _out_proj(og, fg, sa, sb, x, gate, final_norm_g, wpa, wpb, wo):
    B, T, _ = x.shape
    per_batch_mod = gate.shape[0] > 1

    def tok_spec(cols):
        return pl.BlockSpec((None, TOKEN_TILE, cols), lambda b, t: (b, t, 0))

    mod_spec = pl.BlockSpec((None, 1, D_MODEL), (lambda b, t: (b, 0, 0)) if per_batch_mod else (lambda b, t: (0, 0, 0)))
    return pl.pallas_call(
        _out_proj_kernel,
        grid=(B, T // TOKEN_TILE),
        in_specs=[tok_spec(VDIM), tok_spec(FOURIER_DIM), tok_spec(D_MODEL), tok_spec(D_MODEL), tok_spec(D_MODEL),
                  mod_spec, _const_spec((1, D_MODEL)),
                  _const_spec((VDIM, D_MODEL)), _const_spec((FOURIER_DIM, D_MODEL)), _const_spec((D_MODEL, D_MODEL))],
        out_specs=tok_spec(D_MODEL),
        out_shape=jax.ShapeDtypeStruct((B, T, D_MODEL), F32),
        compiler_params=pltpu.CompilerParams(
            dimension_semantics=("arbitrary", "arbitrary"), vmem_limit_bytes=VMEM_LIMIT_BYTES),
        name="out_proj",
    )(og, fg, sa, sb, x, gate, final_norm_g, wpa, wpb, wo)


def _channel_dft_table():
    n = np.arange(FGROUP_CH)
    ang = 2.0 * np.pi * ((n[:, None] * n[None, :]) % FGROUP_CH) / FGROUP_CH
    tab = np.concatenate([np.cos(ang), np.sin(ang)], axis=1) / np.sqrt(FGROUP_CH)
    return jnp.asarray(tab, F32).astype(BF16)


def _time_dft_tables(T):
    lo = 32
    hi = T // lo
    n = np.arange(T)
    ang_hi = 2.0 * np.pi * (((np.arange(hi)[:, None] * lo) * n[None, :]) % T) / T
    ang_lo = 2.0 * np.pi * ((np.arange(lo)[:, None] * n[None, :]) % T) / T
    scale = 1.0 / np.sqrt(T)
    ch = jnp.asarray(np.cos(ang_hi) * scale, F32)[:, None, :]
    sh = jnp.asarray(np.sin(ang_hi) * scale, F32)[:, None, :]
    cl = jnp.asarray(np.cos(ang_lo), F32)[None, :, :]
    sl = jnp.asarray(np.sin(ang_lo), F32)[None, :, :]
    ct = (ch * cl - sh * sl).astype(BF16).reshape(T, T)
    stn = (-(sh * cl + ch * sl)).astype(BF16).reshape(T, T)
    return ct, stn


def _rope_tables(T):
    rows = T // GRID_W
    r = jnp.repeat(jnp.arange(rows), GRID_W).astype(F32)
    c = jnp.tile(jnp.arange(GRID_W), rows).astype(F32)
    n_freq = HEAD_DK // 4
    freqs = ROPE_BASE ** (-jnp.arange(n_freq, dtype=F32) / n_freq)
    ang_r = r[:, None] * freqs
    ang_c = c[:, None] * freqs
    cos = jnp.concatenate([jnp.cos(ang_r), jnp.cos(ang_r), jnp.cos(ang_c), jnp.cos(ang_c)], axis=-1)
    sin = jnp.concatenate([-jnp.sin(ang_r), jnp.sin(ang_r), -jnp.sin(ang_c), jnp.sin(ang_c)], axis=-1)
    return cos, sin


def _path(x, scale, shift, gate, init_states, rope, heads, wts):
    T = x.shape[1]
    (qdf, kif, kef, qdb, kib, keb, dtf, dtb, v, ga, a, bm, gb, sa, sb) = _in_proj(
        x, scale, shift, wts["norm_g"], wts["w_head"], wts["w_tail"], wts["w_r"], wts["w_ab"], wts["b_ab"],
        wts["cs_tab"], rope)
    og, sf, sbw = _gla(qdf, kif, kef, qdb, kib, keb, dtf, dtb, v, ga, init_states, wts["gla_norm_g"], heads)
    ct, stn = _time_dft_tables(T)
    fg = _fourier(ct, stn, a, bm, gb, wts["w_four"])
    y = _out_proj(og, fg, sa, sb, x, gate, wts["final_norm_g"], wts["w_proj_a"], wts["w_proj_b"], wts["w_out"])
    return y, sf, sbw


def kernel(x_prompt, x_sample, state_gla_fwd, state_gla_bwd, c, c_ctx, w_ada, b_ada, norm_g, w_in,
           w_alpha_fwd, b_alpha_fwd, w_alpha_bwd, b_alpha_bwd, gla_norm_g, w_four, w_proj_a, w_proj_b,
           w_out, final_norm_g):
    depth = w_in.shape[0]
    assert depth == 1, "single trunk layer"
    bs = x_sample.shape[0]

    n_cond = bs + 1
    cond_rows = -(-n_cond // 8) * 8
    cond = jnp.concatenate([c, c_ctx[None, :], jnp.zeros((cond_rows - n_cond, D_MODEL), F32)], axis=0)
    ada = _ada(cond, w_ada[0].astype(BF16), b_ada[0][None, :])
    shift, scale, gate = ada[:, :D_MODEL], ada[:, D_MODEL:2 * D_MODEL], ada[:, 2 * D_MODEL:]
    mod = lambda m, lo, hi: m[lo:hi][:, None, :]

    wi = w_in[0]
    o_r = _W_HEAD_COLS
    o_u = o_r + 2 * GATE_RANK
    w_r = jnp.pad(wi[:, o_r:o_u], ((0, 0), (0, LANES - 2 * GATE_RANK))).astype(BF16)
    w_ab = jnp.zeros((LANES, 2 * KDIM), F32)
    w_ab = w_ab.at[:GATE_RANK, :KDIM].set(w_alpha_fwd[0]).at[GATE_RANK:2 * GATE_RANK, KDIM:].set(w_alpha_bwd[0])
    wts = dict(
        norm_g=norm_g[0][None, :], w_head=wi[:, :o_r].astype(BF16), w_tail=wi[:, o_u:].astype(BF16),
        w_r=w_r, w_ab=w_ab.astype(BF16),
        b_ab=jnp.concatenate([b_alpha_fwd[0], b_alpha_bwd[0]])[None, :], cs_tab=_channel_dft_table(),
        gla_norm_g=gla_norm_g[0][None, :], w_four=w_four[0].astype(BF16),
        w_proj_a=w_proj_a[0].astype(BF16), w_proj_b=w_proj_b[0].astype(BF16), w_out=w_out[0].astype(BF16),
        final_norm_g=final_norm_g[None, :])

    y_prompt, sf, sb = _path(x_prompt, mod(scale, bs, bs + 1), mod(shift, bs, bs + 1), mod(gate, bs, bs + 1),
                             None, None, N_HEADS, wts)
    y_sample, _, _ = _path(x_sample, mod(scale, 0, bs), mod(shift, 0, bs), mod(gate, 0, bs),
                           (state_gla_fwd[:, 0], state_gla_bwd[:, 0]), _rope_tables(x_sample.shape[1]), 2, wts)
    return (y_prompt, y_sample, sf[:, None].astype(x_prompt.dtype), sb[:, None].astype(x_prompt.dtype))
```

```python
import functools

import numpy as np
import jax
import jax.numpy as jnp
from jax import lax
from jax.experimental import pallas as pl
from jax.experimental.pallas import tpu as pltpu

F32 = jnp.float32
BF16 = jnp.bfloat16

D_MODEL = 1024
N_HEADS = 4
HEAD_DK = 128
HEAD_DV = 256
KDIM = N_HEADS * HEAD_DK
VDIM = N_HEADS * HEAD_DV
GATE_RANK = 16
GATE_NORM = 16.0
CHUNK = 64
N_FGROUPS = 4
FGROUP_CH = 128
FOURIER_DIM = N_FGROUPS * FGROUP_CH
GRID_W = 64
ROPE_BASE = 10000.0
EPS = 1e-6

LANES = 128
TOKEN_TILE = 256
CHUNKS_PER_TILE = TOKEN_TILE // CHUNK
OUT_TILES_PER_STEP = 2
VMEM_LIMIT_BYTES = 56 * 1024 * 1024

_OFF_QK = 0
_OFF_V = _OFF_QK + 2 * KDIM
_OFF_GA = _OFF_V + VDIM
_W_HEAD_COLS = _OFF_GA + VDIM
_OFF_U = 0
_OFF_GB = _OFF_U + FOURIER_DIM
_OFF_MA = _OFF_GB + FOURIER_DIM
_OFF_MB = _OFF_MA + D_MODEL
_W_TAIL_COLS = _OFF_MB + D_MODEL


def _mm(a, b):
    return jnp.dot(a, b, preferred_element_type=F32)


def _mm_ta(a, b):
    return lax.dot_general(a, b, (((0,), (0,)), ((), ())), preferred_element_type=F32)


def _mm_tb(a, b):
    return lax.dot_general(a, b, (((1,), (1,)), ((), ())), preferred_element_type=F32)


def _split_bf16(x):
    hi = x.astype(BF16)
    lo = (x - hi.astype(F32)).astype(BF16)
    return hi, lo


def _const_spec(shape):
    nd = len(shape)
    return pl.BlockSpec(shape, lambda *_: (0,) * nd)


def _ada_kernel(c_ref, w_ref, b_ref, o_ref):
    c = c_ref[...]
    s = c * jax.nn.sigmoid(c)
    o_ref[...] = _mm(s.astype(BF16), w_ref[...]) + b_ref[...]


def _ada(cond, w_bf, b):
    rows = cond.shape[0]
    return pl.pallas_call(
        _ada_kernel,
        out_shape=jax.ShapeDtypeStruct((rows, 3 * D_MODEL), F32),
        compiler_params=pltpu.CompilerParams(vmem_limit_bytes=VMEM_LIMIT_BYTES),
        name="ada",
    )(cond, w_bf, b)


def _in_proj_kernel(use_rope, *refs):
    if use_rope:
        (x_ref, sc_ref, sh_ref, ng_ref, wa_ref, wb_ref, wr_ref, wab_ref, bab_ref, cs_ref, cos_ref, sin_ref,
         qdf_ref, kif_ref, kef_ref, qdb_ref, kib_ref, keb_ref, dtf_ref, dtb_ref,
         v_ref, ga_ref, a_ref, b_ref, gb_ref, sa_ref, sb_ref) = refs
    else:
        (x_ref, sc_ref, sh_ref, ng_ref, wa_ref, wb_ref, wr_ref, wab_ref, bab_ref, cs_ref,
         qdf_ref, kif_ref, kef_ref, qdb_ref, kib_ref, keb_ref, dtf_ref, dtb_ref,
         v_ref, ga_ref, a_ref, b_ref, gb_ref, sa_ref, sb_ref) = refs

    x = x_ref[...]
    xn = x * lax.rsqrt(jnp.mean(x * x, axis=-1, keepdims=True) + EPS)
    h = (xn * ng_ref[...]) * (1.0 + sc_ref[...]) + sh_ref[...]
    hb = h.astype(BF16)

    r = _mm(hb, wr_ref[...])
    qk = _mm(hb, wa_ref[:, _OFF_QK:_OFF_QK + 2 * KDIM])
    xg = _mm(r.astype(BF16), wab_ref[...]) + bab_ref[...]
    v_ref[...] = _mm(hb, wa_ref[:, _OFF_V:_OFF_V + VDIM]).astype(BF16)
    g_all = (jnp.minimum(xg, 0.0) - jnp.log1p(jnp.exp(-jnp.abs(xg)))) * (1.0 / GATE_NORM)

    q = qk[:, :KDIM] * (HEAD_DK ** -0.5)
    k = qk[:, KDIM:]
    if use_rope:
        cos = cos_ref[...]
        sin = sin_ref[...]
        lane = lax.broadcasted_iota(jnp.int32, (TOKEN_TILE, HEAD_DK), 1)
        first_half = (lane // (HEAD_DK // 4)) % 2 == 0

        def rope(t):
            outs = []
            for hh in range(N_HEADS):
                th = t[:, hh * HEAD_DK:(hh + 1) * HEAD_DK]
                partner = jnp.where(first_half,
                                    pltpu.roll(th, HEAD_DK - HEAD_DK // 4, axis=1),
                                    pltpu.roll(th, HEAD_DK // 4, axis=1))
                outs.append(th * cos + partner * sin)
            return jnp.concatenate(outs, axis=1)

        q = rope(q)
        k = rope(k)

    ub = _mm(hb, wb_ref[:, _OFF_U:_OFF_U + FOURIER_DIM]).astype(BF16)

    row = lax.broadcasted_iota(jnp.int32, (TOKEN_TILE, TOKEN_TILE), 0)
    col = lax.broadcasted_iota(jnp.int32, (TOKEN_TILE, TOKEN_TILE), 1)
    same_chunk = (row // CHUNK) == (col // CHUNK)
    bcs = []
    for direction in range(2):
        g_hi, g_lo = _split_bf16(g_all[:, direction * KDIM:(direction + 1) * KDIM])
        tri = (col <= row) if direction == 0 else (col >= row)
        tri_bd = jnp.where(same_chunk & tri, 1.0, 0.0).astype(BF16)
        bcs.append(_mm(tri_bd, g_hi) + _mm(tri_bd, g_lo))

    ga = _mm(hb, wa_ref[:, _OFF_GA:_OFF_GA + VDIM])
    ga_ref[...] = (ga * jax.nn.sigmoid(ga)).astype(BF16)

    n_parts = 4 * CHUNKS_PER_TILE
    sel_r = lax.broadcasted_iota(jnp.int32, (n_parts, LANES), 0)
    sel_c = lax.broadcasted_iota(jnp.int32, (n_parts, LANES), 1)
    part_sel = jnp.where((sel_r % CHUNKS_PER_TILE == sel_c) & (sel_r < 3 * CHUNKS_PER_TILE), 1.0, 0.0).astype(BF16)
    for direction, (qd_ref, ki_ref, ke_ref, dt_ref) in enumerate(
            ((qdf_ref, kif_ref, kef_ref, dtf_ref), (qdb_ref, kib_ref, keb_ref, dtb_ref))):
        bc = bcs[direction]
        edge = CHUNK - 1 if direction == 0 else 0
        bl_rows = bc.reshape(CHUNKS_PER_TILE, CHUNK, KDIM)[:, edge, :]
        bl = jnp.broadcast_to(bl_rows[:, None, :], (CHUNKS_PER_TILE, CHUNK, KDIM)).reshape(TOKEN_TILE, KDIM)
        qd_ref[...] = (q * jnp.exp(bc)).astype(BF16)
        ki_ref[...] = (k * jnp.exp(-bc)).astype(BF16)
        ke_ref[...] = (k * jnp.exp(bl - bc)).astype(BF16)
        p0 = bl_rows.astype(BF16).astype(F32)
        p1 = (bl_rows - p0).astype(BF16).astype(F32)
        p2 = (bl_rows - p0 - p1).astype(BF16).astype(F32)
        parts = jnp.concatenate([p0, p1, p2, jnp.zeros_like(p0)], axis=0).astype(BF16)
        dt_ref[...] = jnp.exp(_mm_ta(parts, part_sel))

    for g in range(N_FGROUPS):
        ab = _mm(ub[:, g * FGROUP_CH:(g + 1) * FGROUP_CH], cs_ref[...])
        a_ref[:, g * FGROUP_CH:(g + 1) * FGROUP_CH] = ab[:, :FGROUP_CH].astype(BF16)
        b_ref[:, g * FGROUP_CH:(g + 1) * FGROUP_CH] = ab[:, FGROUP_CH:].astype(BF16)

    gb = _mm(hb, wb_ref[:, _OFF_GB:_OFF_GB + FOURIER_DIM])
    gb_ref[...] = (gb * jax.nn.sigmoid(gb)).astype(BF16)
    sa_ref[...] = jax.nn.sigmoid(_mm(hb, wb_ref[:, _OFF_MA:_OFF_MA + D_MODEL])).astype(BF16)
    sb_ref[...] = jax.nn.sigmoid(_mm(hb, wb_ref[:, _OFF_MB:_OFF_MB + D_MODEL])).astype(BF16)


def _in_proj(x, scale, shift, norm_g, w_head, w_tail, w_r, w_ab, b_ab, cs_tab, rope):
    B, T, _ = x.shape
    nt = T // TOKEN_TILE
    use_rope = rope is not None
    per_batch_mod = scale.shape[0] > 1

    def tok_spec(cols):
        return pl.BlockSpec((None, TOKEN_TILE, cols), lambda b, t: (b, t, 0))

    mod_spec = pl.BlockSpec((None, 1, D_MODEL), (lambda b, t: (b, 0, 0)) if per_batch_mod else (lambda b, t: (0, 0, 0)))
    in_specs = [
        tok_spec(D_MODEL), mod_spec, mod_spec, _const_spec((1, D_MODEL)),
        _const_spec((D_MODEL, _W_HEAD_COLS)), _const_spec((D_MODEL, _W_TAIL_COLS)), _const_spec((D_MODEL, LANES)),
        _const_spec((LANES, 2 * KDIM)), _const_spec((1, 2 * KDIM)),
        _const_spec((FGROUP_CH, 2 * FGROUP_CH)),
    ]
    args = [x, scale, shift, norm_g, w_head, w_tail, w_r, w_ab, b_ab, cs_tab]
    if use_rope:
        in_specs += [pl.BlockSpec((TOKEN_TILE, HEAD_DK), lambda b, t: (t, 0))] * 2
        args += list(rope)

    dt_spec = pl.BlockSpec((None, None, KDIM, LANES), lambda b, t: (b, t, 0, 0))
    tok_bf = lambda cols: jax.ShapeDtypeStruct((B, T, cols), BF16)
    dt_shape = jax.ShapeDtypeStruct((B, nt, KDIM, LANES), F32)
    out_specs = [tok_spec(KDIM)] * 6 + [dt_spec] * 2 + [
        tok_spec(VDIM), tok_spec(VDIM), tok_spec(FOURIER_DIM), tok_spec(FOURIER_DIM),
        tok_spec(FOURIER_DIM), tok_spec(D_MODEL), tok_spec(D_MODEL)]
    out_shape = [tok_bf(KDIM)] * 6 + [dt_shape] * 2 + [
        tok_bf(VDIM), tok_bf(VDIM), tok_bf(FOURIER_DIM), tok_bf(FOURIER_DIM),
        tok_bf(FOURIER_DIM), tok_bf(D_MODEL), tok_bf(D_MODEL)]
    return pl.pallas_call(
        functools.partial(_in_proj_kernel, use_rope),
        grid=(B, nt),
        in_specs=in_specs,
        out_specs=out_specs,
        out_shape=out_shape,
        compiler_params=pltpu.CompilerParams(
            dimension_semantics=("arbitrary", "arbitrary"), vmem_limit_bytes=VMEM_LIMIT_BYTES),
        name="in_proj_rope" if use_rope else "in_proj",
    )(*args)


def _gla_kernel(heads, n_tiles, has_init, *refs):
    (qdf_ref, kif_ref, kef_ref, qdb_ref, kib_ref, keb_ref, dtf_ref, dtb_ref, v_ref, ga_ref) = refs[:10]
    refs = refs[10:]
    if has_init:
        s0f_ref, s0b_ref = refs[:2]
        refs = refs[2:]
    gn_ref, og_ref, sf_ref, sb_ref, state, d_state, of_acc, ob_acc = refs
    if has_init:
        state[0] = s0f_ref[...]
        state[1] = s0b_ref[...]
    else:
        state[...] = jnp.zeros_like(state)
    dir_refs = ((qdf_ref, kif_ref, kef_ref, dtf_ref, of_acc), (qdb_ref, kib_ref, keb_ref, dtb_ref, ob_acc))

    def tile_body(finalize, j, carry):
        row = lax.broadcasted_iota(jnp.int32, (TOKEN_TILE, TOKEN_TILE), 0)
        col = lax.broadcasted_iota(jnp.int32, (TOKEN_TILE, TOKEN_TILE), 1)
        same_chunk = (row // CHUNK) == (col // CHUNK)
        causal = (same_chunk & (col <= row), same_chunk & (col >= row))
        tiles = (j, n_tiles - 1 - j)
        bases = tuple(pl.multiple_of(t * TOKEN_TILE, TOKEN_TILE) for t in tiles)

        for direction in range(2):
            qd_ref, ki_ref, ke_ref, _, o_acc = dir_refs[direction]
            trows = pl.ds(bases[direction], TOKEN_TILE)
            for hh in range(heads):
                kc = slice(hh * HEAD_DK, (hh + 1) * HEAD_DK)
                vc = slice(hh * HEAD_DV, (hh + 1) * HEAD_DV)
                scores = jnp.where(causal[direction], _mm_tb(qd_ref[trows, kc], ki_ref[trows, kc]), 0.0)
                o_acc[trows, vc] = _mm(scores.astype(BF16), v_ref[trows, vc])
                for c in range(CHUNKS_PER_TILE):
                    rows = pl.ds(bases[direction] + c * CHUNK, CHUNK)
                    d_state[direction, hh, c] = _mm_ta(ke_ref[rows, kc], v_ref[rows, vc])

        steps = [(step, direction) for step in range(CHUNKS_PER_TILE) for direction in range(2)]
        if finalize[0] != finalize[1]:
            steps.sort(key=lambda sd: finalize[sd[1]])
        for step, direction in steps:
            qd_ref, _, _, dt_ref, o_acc = dir_refs[direction]
            other_acc = dir_refs[1 - direction][4]
            c = step if direction == 0 else CHUNKS_PER_TILE - 1 - step
            rows = pl.ds(bases[direction] + c * CHUNK, CHUNK)
            for hh in range(heads):
                kc = slice(hh * HEAD_DK, (hh + 1) * HEAD_DK)
                vc = slice(hh * HEAD_DV, (hh + 1) * HEAD_DV)
                s = state[direction, hh]
                o = o_acc[rows, vc] + _mm(qd_ref[rows, kc], s.astype(BF16))
                if finalize[direction]:
                    o = o + other_acc[rows, vc]
                    on = o * lax.rsqrt(jnp.mean(o * o, axis=-1, keepdims=True) + EPS) * gn_ref[...]
                    og_ref[rows, vc] = (on * ga_ref[rows, vc].astype(F32)).astype(BF16)
                else:
                    o_acc[rows, vc] = o
                dec = dt_ref[tiles[direction], kc, c:c + 1]
                state[direction, hh] = dec * s + d_state[direction, hh, c]
        return carry

    first_half = n_tiles // 2
    unroll = 4 if first_half % 4 == 0 else 1
    lax.fori_loop(0, first_half, functools.partial(tile_body, (False, False)), 0, unroll=unroll)
    if n_tiles % 2:
        tile_body((False, True), first_half, 0)
    lax.fori_loop(n_tiles - first_half, n_tiles, functools.partial(tile_body, (True, True)), 0, unroll=unroll)
    sf_ref[...] = state[0]
    sb_ref[...] = state[1]


def _gla(qdf, kif, kef, qdb, kib, keb, dtf, dtb, v, ga, init_states, gla_norm_g, heads):
    B, T, _ = v.shape
    nt = T // TOKEN_TILE
    hg = N_HEADS // heads
    has_init = init_states is not None
    k_spec = pl.BlockSpec((None, T, heads * HEAD_DK), lambda b, h: (b, 0, h))
    v_spec = pl.BlockSpec((None, T, heads * HEAD_DV), lambda b, h: (b, 0, h))
    dt_spec = pl.BlockSpec((None, nt, heads * HEAD_DK, LANES), lambda b, h: (b, 0, h, 0))
    s_spec = pl.BlockSpec((None, heads, HEAD_DK, HEAD_DV), lambda b, h: (b, h, 0, 0))
    s_shape = jax.ShapeDtypeStruct((B, N_HEADS, HEAD_DK, HEAD_DV), F32)
    return pl.pallas_call(
        functools.partial(_gla_kernel, heads, nt, has_init),
        grid=(B, hg),
        in_specs=([k_spec] * 6 + [dt_spec] * 2 + [v_spec, v_spec] + [s_spec] * (2 if has_init else 0)
                  + [_const_spec((1, HEAD_DV))]),
        out_specs=[v_spec, s_spec, s_spec],
        out_shape=[jax.ShapeDtypeStruct((B, T, VDIM), BF16), s_shape, s_shape],
        scratch_shapes=[pltpu.VMEM((2, heads, HEAD_DK, HEAD_DV), F32),
                        pltpu.VMEM((2, heads, CHUNKS_PER_TILE, HEAD_DK, HEAD_DV), F32),
                        pltpu.VMEM((T, heads * HEAD_DV), F32),
                        pltpu.VMEM((T, heads * HEAD_DV), F32)],
        compiler_params=pltpu.CompilerParams(
            dimension_semantics=("arbitrary", "arbitrary"), vmem_limit_bytes=VMEM_LIMIT_BYTES),
        name="gla",
    )(qdf, kif, kef, qdb, kib, keb, dtf, dtb, v, ga, *(init_states or ()), gla_norm_g)


def _fourier_kernel(ct_ref, st_ref, a_ref, b_ref, gb_ref, wf_ref, o_ref):
    f = (_mm(ct_ref[...], a_ref[...]) + _mm(st_ref[...], b_ref[...])).astype(BF16)
    for g in range(N_FGROUPS):
        cols = slice(g * FGROUP_CH, (g + 1) * FGROUP_CH)
        o_ref[:, cols] = (_mm(f[:, cols], wf_ref[g]) * gb_ref[:, cols].astype(F32)).astype(BF16)


def _fourier(ct, stn, a, bm, gb, wf):
    B, T, _ = a.shape
    tf = min(T, 512)
    tab_spec = pl.BlockSpec((tf, T), lambda b, t: (t, 0))
    seq_spec = pl.BlockSpec((None, T, FOURIER_DIM), lambda b, t: (b, 0, 0))
    tile_spec = pl.BlockSpec((None, tf, FOURIER_DIM), lambda b, t: (b, t, 0))
    return pl.pallas_call(
        _fourier_kernel,
        grid=(B, T // tf),
        in_specs=[tab_spec, tab_spec, seq_spec, seq_spec, tile_spec,
                  _const_spec((N_FGROUPS, FGROUP_CH, FGROUP_CH))],
        out_specs=tile_spec,
        out_shape=jax.ShapeDtypeStruct((B, T, FOURIER_DIM), BF16),
        compiler_params=pltpu.CompilerParams(
            dimension_semantics=("arbitrary", "arbitrary"), vmem_limit_bytes=VMEM_LIMIT_BYTES),
        name="fourier",
    )(ct, stn, a, bm, gb, wf)


def _out_proj_kernel(og_ref, fg_ref, sa_ref, sb_ref, x_ref, gate_ref, fng_ref, wpa_ref, wpb_ref, wo_ref, y_ref):
    for i in range(OUT_TILES_PER_STEP):
        rows = slice(i * TOKEN_TILE, (i + 1) * TOKEN_TILE)
        ya = _mm(og_ref[rows, :], wpa_ref[...])
        yb = _mm(fg_ref[rows, :], wpb_ref[...])
        merged = sa_ref[rows, :].astype(F32) * ya + sb_ref[rows, :].astype(F32) * yb
        xo = x_ref[rows, :] + gate_ref[...] * _mm(merged.astype(BF16), wo_ref[...])
        y_ref[rows, :] = xo * lax.rsqrt(jnp.mean(xo * xo, axis=-1, keepdims=True) + EPS) * fng_ref[...]


def _out_proj(og, fg, sa, sb, x, gate, final_norm_g, wpa, wpb, wo):
    B, T, _ = x.shape
    per_batch_mod = gate.shape[0] > 1
    step_rows = OUT_TILES_PER_STEP * TOKEN_TILE
    if T < step_rows:
        assert not per_batch_mod and (B * T) % step_rows == 0
        fold = lambda a: a.reshape(B * T // step_rows, step_rows, a.shape[-1])
        y = _out_proj(fold(og), fold(fg), fold(sa), fold(sb), fold(x), gate, final_norm_g, wpa, wpb, wo)
        return y.reshape(B, T, D_MODEL)

    def tok_spec(cols):
        return pl.BlockSpec((None, step_rows, cols), lambda b, t: (b, t, 0))

    mod_spec = pl.BlockSpec((None, 1, D_MODEL), (lambda b, t: (b, 0, 0)) if per_batch_mod else (lambda b, t: (0, 0, 0)))
    return pl.pallas_call(
        _out_proj_kernel,
        grid=(B, T // step_rows),
        in_specs=[tok_spec(VDIM), tok_spec(FOURIER_DIM), tok_spec(D_MODEL), tok_spec(D_MODEL), tok_spec(D_MODEL),
                  mod_spec, _const_spec((1, D_MODEL)),
                  _const_spec((VDIM, D_MODEL)), _const_spec((FOURIER_DIM, D_MODEL)), _const_spec((D_MODEL, D_MODEL))],
        out_specs=tok_spec(D_MODEL),
        out_shape=jax.ShapeDtypeStruct((B, T, D_MODEL), F32),
        compiler_params=pltpu.CompilerParams(
            dimension_semantics=("arbitrary", "arbitrary"), vmem_limit_bytes=VMEM_LIMIT_BYTES),
        name="out_proj",
    )(og, fg, sa, sb, x, gate, final_norm_g, wpa, wpb, wo)


def _channel_dft_table():
    n = np.arange(FGROUP_CH)
    ang = 2.0 * np.pi * ((n[:, None] * n[None, :]) % FGROUP_CH) / FGROUP_CH
    tab = np.concatenate([np.cos(ang), np.sin(ang)], axis=1) / np.sqrt(FGROUP_CH)
    return jnp.asarray(tab, F32).astype(BF16)


def _time_dft_tables(T):
    lo = 32
    hi = T // lo
    n = np.arange(T)
    ang_hi = 2.0 * np.pi * (((np.arange(hi)[:, None] * lo) * n[None, :]) % T) / T
    ang_lo = 2.0 * np.pi * ((np.arange(lo)[:, None] * n[None, :]) % T) / T
    scale = 1.0 / np.sqrt(T)
    ch = jnp.asarray(np.cos(ang_hi) * scale, F32)[:, None, :]
    sh = jnp.asarray(np.sin(ang_hi) * scale, F32)[:, None, :]
    cl = jnp.asarray(np.cos(ang_lo), F32)[None, :, :]
    sl = jnp.asarray(np.sin(ang_lo), F32)[None, :, :]
    ct = (ch * cl - sh * sl).astype(BF16).reshape(T, T)
    stn = (-(sh * cl + ch * sl)).astype(BF16).reshape(T, T)
    return ct, stn


def _rope_tables(T):
    rows = T // GRID_W
    r = jnp.repeat(jnp.arange(rows), GRID_W).astype(F32)
    c = jnp.tile(jnp.arange(GRID_W), rows).astype(F32)
    n_freq = HEAD_DK // 4
    freqs = ROPE_BASE ** (-jnp.arange(n_freq, dtype=F32) / n_freq)
    ang_r = r[:, None] * freqs
    ang_c = c[:, None] * freqs
    cos = jnp.concatenate([jnp.cos(ang_r), jnp.cos(ang_r), jnp.cos(ang_c), jnp.cos(ang_c)], axis=-1)
    sin = jnp.concatenate([-jnp.sin(ang_r), jnp.sin(ang_r), -jnp.sin(ang_c), jnp.sin(ang_c)], axis=-1)
    return cos, sin


def _path(x, scale, shift, gate, init_states, rope, heads, wts):
    T = x.shape[1]
    (qdf, kif, kef, qdb, kib, keb, dtf, dtb, v, ga, a, bm, gb, sa, sb) = _in_proj(
        x, scale, shift, wts["norm_g"], wts["w_head"], wts["w_tail"], wts["w_r"], wts["w_ab"], wts["b_ab"],
        wts["cs_tab"], rope)
    og, sf, sbw = _gla(qdf, kif, kef, qdb, kib, keb, dtf, dtb, v, ga, init_states, wts["gla_norm_g"], heads)
    ct, stn = _time_dft_tables(T)
    fg = _fourier(ct, stn, a, bm, gb, wts["w_four"])
    y = _out_proj(og, fg, sa, sb, x, gate, wts["final_norm_g"], wts["w_proj_a"], wts["w_proj_b"], wts["w_out"])
    return y, sf, sbw


def kernel(x_prompt, x_sample, state_gla_fwd, state_gla_bwd, c, c_ctx, w_ada, b_ada, norm_g, w_in,
           w_alpha_fwd, b_alpha_fwd, w_alpha_bwd, b_alpha_bwd, gla_norm_g, w_four, w_proj_a, w_proj_b,
           w_out, final_norm_g):
    depth = w_in.shape[0]
    assert depth == 1, "single trunk layer"
    bs = x_sample.shape[0]

    n_cond = bs + 1
    cond_rows = -(-n_cond // 8) * 8
    cond = jnp.concatenate([c, c_ctx[None, :], jnp.zeros((cond_rows - n_cond, D_MODEL), F32)], axis=0)
    ada = _ada(cond, w_ada[0].astype(BF16), b_ada[0][None, :])
    shift, scale, gate = ada[:, :D_MODEL], ada[:, D_MODEL:2 * D_MODEL], ada[:, 2 * D_MODEL:]
    mod = lambda m, lo, hi: m[lo:hi][:, None, :]

    wi = w_in[0]
    o_r = _W_HEAD_COLS
    o_u = o_r + 2 * GATE_RANK
    w_r = jnp.pad(wi[:, o_r:o_u], ((0, 0), (0, LANES - 2 * GATE_RANK))).astype(BF16)
    w_ab = jnp.zeros((LANES, 2 * KDIM), F32)
    w_ab = w_ab.at[:GATE_RANK, :KDIM].set(w_alpha_fwd[0]).at[GATE_RANK:2 * GATE_RANK, KDIM:].set(w_alpha_bwd[0])
    wts = dict(
        norm_g=norm_g[0][None, :], w_head=wi[:, :o_r].astype(BF16), w_tail=wi[:, o_u:].astype(BF16),
        w_r=w_r, w_ab=w_ab.astype(BF16),
        b_ab=jnp.concatenate([b_alpha_fwd[0], b_alpha_bwd[0]])[None, :], cs_tab=_channel_dft_table(),
        gla_norm_g=gla_norm_g[0][None, :], w_four=w_four[0].astype(BF16),
        w_proj_a=w_proj_a[0].astype(BF16), w_proj_b=w_proj_b[0].astype(BF16), w_out=w_out[0].astype(BF16),
        final_norm_g=final_norm_g[None, :])

    y_prompt, sf, sb = _path(x_prompt, mod(scale, bs, bs + 1), mod(shift, bs, bs + 1), mod(gate, bs, bs + 1),
                             None, None, N_HEADS, wts)
    y_sample, _, _ = _path(x_sample, mod(scale, 0, bs), mod(shift, 0, bs), mod(gate, 0, bs),
                           (state_gla_fwd[:, 0], state_gla_bwd[:, 0]), _rope_tables(x_sample.shape[1]), 2, wts)
    return (y_prompt, y_sample, sf[:, None].astype(x_prompt.dtype), sb[:, None].astype(x_prompt.dtype))
```

```python
import functools

import numpy as np
import jax
import jax.numpy as jnp
from jax import lax
from jax.experimental import pallas as pl
from jax.experimental.pallas import tpu as pltpu

F32 = jnp.float32
BF16 = jnp.bfloat16

D_MODEL = 1024
N_HEADS = 4
HEAD_DK = 128
HEAD_DV = 256
KDIM = N_HEADS * HEAD_DK
VDIM = N_HEADS * HEAD_DV
GATE_RANK = 16
GATE_NORM = 16.0
CHUNK = 64
N_FGROUPS = 4
FGROUP_CH = 128
FOURIER_DIM = N_FGROUPS * FGROUP_CH
GRID_W = 64
ROPE_BASE = 10000.0
EPS = 1e-6

LANES = 128
MXU_COLS = 256
TOKEN_TILE = 256
CHUNKS_PER_TILE = TOKEN_TILE // CHUNK
IN_TILES_PER_STEP = 2
OUT_TILES_PER_STEP = 2
GLA_SHORT_SEQS_PER_STEP = 4
FOURIER_ROWS_PER_STEP = 512
FOURIER_SHORT_SEQS_PER_STEP = 8
VMEM_LIMIT_BYTES = 56 * 1024 * 1024

_OFF_QK = 0
_OFF_V = _OFF_QK + 2 * KDIM
_OFF_GA = _OFF_V + VDIM
_W_HEAD_COLS = _OFF_GA + VDIM
_OFF_U = 0
_OFF_GB = _OFF_U + FOURIER_DIM
_OFF_MA = _OFF_GB + FOURIER_DIM
_OFF_MB = _OFF_MA + D_MODEL
_W_TAIL_COLS = _OFF_MB + D_MODEL


def _mm(a, b):
    return jnp.dot(a, b, preferred_element_type=F32)


def _mm_ta(a, b):
    return lax.dot_general(a, b, (((0,), (0,)), ((), ())), preferred_element_type=F32)


def _mm_tb(a, b):
    return lax.dot_general(a, b, (((1,), (1,)), ((), ())), preferred_element_type=F32)


def _split_bf16(x):
    hi = x.astype(BF16)
    lo = (x - hi.astype(F32)).astype(BF16)
    return hi, lo


def _const_spec(shape):
    nd = len(shape)
    return pl.BlockSpec(shape, lambda *_: (0,) * nd)


def _ada_kernel(c_ref, w_ref, b_ref, o_ref):
    c = c_ref[...]
    s = c * jax.nn.sigmoid(c)
    o_ref[...] = _mm(s.astype(BF16), w_ref[...].astype(BF16)) + b_ref[...]


def _ada(cond, w, b):
    rows = cond.shape[0]
    return pl.pallas_call(
        _ada_kernel,
        grid=(3,),
        in_specs=[pl.BlockSpec((rows, D_MODEL), lambda n: (0, 0)),
                  pl.BlockSpec((D_MODEL, D_MODEL), lambda n: (0, n)),
                  pl.BlockSpec((1, D_MODEL), lambda n: (0, n))],
        out_specs=pl.BlockSpec((rows, D_MODEL), lambda n: (0, n)),
        out_shape=jax.ShapeDtypeStruct((rows, 3 * D_MODEL), F32),
        compiler_params=pltpu.CompilerParams(
            dimension_semantics=("arbitrary",), vmem_limit_bytes=VMEM_LIMIT_BYTES),
        name="ada",
    )(cond, w, b)


def _in_proj_kernel(use_rope, *refs):
    n_shared = 9
    n_in = 1 + n_shared + (2 if use_rope else 0)
    dt_slots = (n_in + 6, n_in + 7)
    for i in range(IN_TILES_PER_STEP):
        rows = pl.ds(i * TOKEN_TILE, TOKEN_TILE)
        tile_refs = []
        for idx, ref in enumerate(refs):
            if 1 <= idx <= n_shared:
                tile_refs.append(ref)
            elif idx in dt_slots:
                tile_refs.append(ref.at[i])
            else:
                tile_refs.append(ref.at[rows])
        _in_proj_tile(use_rope, *tile_refs)


def _project(hb, w_ref, off, width, act, out_ref):
    for n in range(0, width, MXU_COLS):
        z = _mm(hb, w_ref[:, off + n:off + n + MXU_COLS])
        out_ref[:, n:n + MXU_COLS] = act(z).astype(BF16)


def _in_proj_tile(use_rope, *refs):
    if use_rope:
        (x_ref, sc_ref, sh_ref, ng_ref, wa_ref, wb_ref, wr_ref, wab_ref, bab_ref, cs_ref, cos_ref, sin_ref,
         qdf_ref, kif_ref, kef_ref, qdb_ref, kib_ref, keb_ref, dtf_ref, dtb_ref,
         v_ref, ga_ref, a_ref, b_ref, gb_ref, sa_ref, sb_ref) = refs
    else:
        (x_ref, sc_ref, sh_ref, ng_ref, wa_ref, wb_ref, wr_ref, wab_ref, bab_ref, cs_ref,
         qdf_ref, kif_ref, kef_ref, qdb_ref, kib_ref, keb_ref, dtf_ref, dtb_ref,
         v_ref, ga_ref, a_ref, b_ref, gb_ref, sa_ref, sb_ref) = refs

    x = x_ref[...]
    xn = x * lax.rsqrt(jnp.mean(x * x, axis=-1, keepdims=True) + EPS)
    h = (xn * ng_ref[...]) * (1.0 + sc_ref[...]) + sh_ref[...]
    hb = h.astype(BF16)

    r = _mm(hb, wr_ref[...])
    qk = _mm(hb, wa_ref[:, _OFF_QK:_OFF_QK + 2 * KDIM])
    xg = _mm(r.astype(BF16), wab_ref[...]) + bab_ref[...]
    _project(hb, wa_ref, _OFF_V, VDIM, lambda z: z, v_ref)
    g_all = (jnp.minimum(xg, 0.0) - jnp.log1p(jnp.exp(-jnp.abs(xg)))) * (1.0 / GATE_NORM)

    q = qk[:, :KDIM] * (HEAD_DK ** -0.5)
    k = qk[:, KDIM:]
    if use_rope:
        cos = cos_ref[...]
        sin = sin_ref[...]
        lane = lax.broadcasted_iota(jnp.int32, (TOKEN_TILE, HEAD_DK), 1)
        first_half = (lane // (HEAD_DK // 4)) % 2 == 0

        def rope(t):
            outs = []
            for hh in range(N_HEADS):
                th = t[:, hh * HEAD_DK:(hh + 1) * HEAD_DK]
                partner = jnp.where(first_half,
                                    pltpu.roll(th, HEAD_DK - HEAD_DK // 4, axis=1),
                                    pltpu.roll(th, HEAD_DK // 4, axis=1))
                outs.append(th * cos + partner * sin)
            return jnp.concatenate(outs, axis=1)

        q = rope(q)
        k = rope(k)

    ub = _mm(hb, wb_ref[:, _OFF_U:_OFF_U + FOURIER_DIM]).astype(BF16)

    row = lax.broadcasted_iota(jnp.int32, (CHUNK, 2 * CHUNK), 0)
    col = lax.broadcasted_iota(jnp.int32, (CHUNK, 2 * CHUNK), 1) % CHUNK
    bcs = []
    for direction in range(2):
        g_hi, g_lo = _split_bf16(g_all[:, direction * KDIM:(direction + 1) * KDIM])
        tri = jnp.where((col <= row) if direction == 0 else (col >= row), 1.0, 0.0).astype(BF16)
        bc_chunks = []
        for c in range(CHUNKS_PER_TILE):
            cr = slice(c * CHUNK, (c + 1) * CHUNK)
            bc_chunks.append(_mm(tri, jnp.concatenate([g_hi[cr], g_lo[cr]], axis=0)))
        bcs.append(jnp.concatenate(bc_chunks, axis=0))

    _project(hb, wa_ref, _OFF_GA, VDIM, jax.nn.silu, ga_ref)

    n_parts = 4 * CHUNKS_PER_TILE
    sel_r = lax.broadcasted_iota(jnp.int32, (n_parts, LANES), 0)
    sel_c = lax.broadcasted_iota(jnp.int32, (n_parts, LANES), 1)
    part_sel = jnp.where((sel_r % CHUNKS_PER_TILE == sel_c) & (sel_r < 3 * CHUNKS_PER_TILE), 1.0, 0.0).astype(BF16)
    for direction, (qd_ref, ki_ref, ke_ref, dt_ref) in enumerate(
            ((qdf_ref, kif_ref, kef_ref, dtf_ref), (qdb_ref, kib_ref, keb_ref, dtb_ref))):
        bc = bcs[direction]
        edge = CHUNK - 1 if direction == 0 else 0
        bl_rows = bc.reshape(CHUNKS_PER_TILE, CHUNK, KDIM)[:, edge, :]
        bl = jnp.broadcast_to(bl_rows[:, None, :], (CHUNKS_PER_TILE, CHUNK, KDIM)).reshape(TOKEN_TILE, KDIM)
        qd_ref[...] = (q * jnp.exp(bc)).astype(BF16)
        ki_ref[...] = (k * jnp.exp(-bc)).astype(BF16)
        ke_ref[...] = (k * jnp.exp(bl - bc)).astype(BF16)
        p0 = bl_rows.astype(BF16).astype(F32)
        p1 = (bl_rows - p0).astype(BF16).astype(F32)
        p2 = (bl_rows - p0 - p1).astype(BF16).astype(F32)
        parts = jnp.concatenate([p0, p1, p2, jnp.zeros_like(p0)], axis=0).astype(BF16)
        dt_ref[...] = jnp.exp(_mm_ta(parts, part_sel))

    for g in range(N_FGROUPS):
        ab = _mm(ub[:, g * FGROUP_CH:(g + 1) * FGROUP_CH], cs_ref[...])
        a_ref[:, g * FGROUP_CH:(g + 1) * FGROUP_CH] = ab[:, :FGROUP_CH].astype(BF16)
        b_ref[:, g * FGROUP_CH:(g + 1) * FGROUP_CH] = ab[:, FGROUP_CH:].astype(BF16)

    _project(hb, wb_ref, _OFF_GB, FOURIER_DIM, jax.nn.silu, gb_ref)
    _project(hb, wb_ref, _OFF_MA, D_MODEL, jax.nn.sigmoid, sa_ref)
    _project(hb, wb_ref, _OFF_MB, D_MODEL, jax.nn.sigmoid, sb_ref)


def _in_proj(x, scale, shift, norm_g, w_head, w_tail, w_r, w_ab, b_ab, cs_tab, rope):
    B, T, _ = x.shape
    use_rope = rope is not None
    per_batch_mod = scale.shape[0] > 1
    step_rows = IN_TILES_PER_STEP * TOKEN_TILE
    if T < step_rows:
        assert not per_batch_mod and not use_rope and (B * T) % step_rows == 0
        outs = _in_proj(x.reshape(B * T // step_rows, step_rows, D_MODEL), scale, shift, norm_g,
                        w_head, w_tail, w_r, w_ab, b_ab, cs_tab, rope)
        return [o.reshape((B, T // TOKEN_TILE) + o.shape[2:]) if o.ndim == 4 else o.reshape(B, T, o.shape[-1])
                for o in outs]
    nt = T // TOKEN_TILE

    def tok_spec(cols):
        return pl.BlockSpec((None, step_rows, cols), lambda b, t: (b, t, 0))

    mod_spec = pl.BlockSpec((None, 1, D_MODEL), (lambda b, t: (b, 0, 0)) if per_batch_mod else (lambda b, t: (0, 0, 0)))
    in_specs = [
        tok_spec(D_MODEL), mod_spec, mod_spec, _const_spec((1, D_MODEL)),
        _const_spec((D_MODEL, _W_HEAD_COLS)), _const_spec((D_MODEL, _W_TAIL_COLS)), _const_spec((D_MODEL, LANES)),
        _const_spec((LANES, 2 * KDIM)), _const_spec((1, 2 * KDIM)),
        _const_spec((FGROUP_CH, 2 * FGROUP_CH)),
    ]
    args = [x, scale, shift, norm_g, w_head, w_tail, w_r, w_ab, b_ab, cs_tab]
    if use_rope:
        in_specs += [pl.BlockSpec((step_rows, HEAD_DK), lambda b, t: (t, 0))] * 2
        args += list(rope)

    dt_spec = pl.BlockSpec((None, IN_TILES_PER_STEP, KDIM, LANES), lambda b, t: (b, t, 0, 0))
    tok_bf = lambda cols: jax.ShapeDtypeStruct((B, T, cols), BF16)
    dt_shape = jax.ShapeDtypeStruct((B, nt, KDIM, LANES), F32)
    out_specs = [tok_spec(KDIM)] * 6 + [dt_spec] * 2 + [
        tok_spec(VDIM), tok_spec(VDIM), tok_spec(FOURIER_DIM), tok_spec(FOURIER_DIM),
        tok_spec(FOURIER_DIM), tok_spec(D_MODEL), tok_spec(D_MODEL)]
    out_shape = [tok_bf(KDIM)] * 6 + [dt_shape] * 2 + [
        tok_bf(VDIM), tok_bf(VDIM), tok_bf(FOURIER_DIM), tok_bf(FOURIER_DIM),
        tok_bf(FOURIER_DIM), tok_bf(D_MODEL), tok_bf(D_MODEL)]
    return pl.pallas_call(
        functools.partial(_in_proj_kernel, use_rope),
        grid=(B, T // step_rows),
        in_specs=in_specs,
        out_specs=out_specs,
        out_shape=out_shape,
        compiler_params=pltpu.CompilerParams(
            dimension_semantics=("arbitrary", "arbitrary"), vmem_limit_bytes=VMEM_LIMIT_BYTES),
        name="in_proj_rope" if use_rope else "in_proj",
    )(*args)


def _gla_kernel(heads, n_tiles, has_init, seqs, *refs):
    gain_pos = 10 + (2 if has_init else 0)
    for bb in range(seqs):
        _gla_seq(heads, n_tiles, has_init, *[r if i == gain_pos else r.at[bb] for i, r in enumerate(refs)])


def _gla_seq(heads, n_tiles, has_init, *refs):
    (qdf_ref, kif_ref, kef_ref, qdb_ref, kib_ref, keb_ref, dtf_ref, dtb_ref, v_ref, ga_ref) = refs[:10]
    refs = refs[10:]
    if has_init:
        s0f_ref, s0b_ref = refs[:2]
        refs = refs[2:]
    gn_ref, og_ref, sf_ref, sb_ref, state, d_state, of_acc, ob_acc = refs
    if has_init:
        state[0] = s0f_ref[...]
        state[1] = s0b_ref[...]
    else:
        state[...] = jnp.zeros_like(state)
    dir_refs = ((qdf_ref, kif_ref, kef_ref, dtf_ref, of_acc), (qdb_ref, kib_ref, keb_ref, dtb_ref, ob_acc))

    def tile_body(finalize, j, carry):
        row = lax.broadcasted_iota(jnp.int32, (TOKEN_TILE, TOKEN_TILE), 0)
        col = lax.broadcasted_iota(jnp.int32, (TOKEN_TILE, TOKEN_TILE), 1)
        same_chunk = (row // CHUNK) == (col // CHUNK)
        causal = (same_chunk & (col <= row), same_chunk & (col >= row))
        tiles = (j, n_tiles - 1 - j)
        bases = tuple(pl.multiple_of(t * TOKEN_TILE, TOKEN_TILE) for t in tiles)

        for direction in range(2):
            qd_ref, ki_ref, ke_ref, _, o_acc = dir_refs[direction]
            trows = pl.ds(bases[direction], TOKEN_TILE)
            for hh in range(heads):
                kc = slice(hh * HEAD_DK, (hh + 1) * HEAD_DK)
                vc = slice(hh * HEAD_DV, (hh + 1) * HEAD_DV)
                scores = jnp.where(causal[direction], _mm_tb(qd_ref[trows, kc], ki_ref[trows, kc]), 0.0)
                o_acc[trows, vc] = _mm(scores.astype(BF16), v_ref[trows, vc])
                for c in range(CHUNKS_PER_TILE):
                    rows = pl.ds(bases[direction] + c * CHUNK, CHUNK)
                    d_state[direction, hh, c] = _mm_ta(ke_ref[rows, kc], v_ref[rows, vc])

        steps = [(step, direction) for step in range(CHUNKS_PER_TILE) for direction in range(2)]
        if finalize[0] != finalize[1]:
            steps.sort(key=lambda sd: finalize[sd[1]])
        for step, direction in steps:
            qd_ref, _, _, dt_ref, o_acc = dir_refs[direction]
            other_acc = dir_refs[1 - direction][4]
            c = step if direction == 0 else CHUNKS_PER_TILE - 1 - step
            rows = pl.ds(bases[direction] + c * CHUNK, CHUNK)
            for hh in range(heads):
                kc = slice(hh * HEAD_DK, (hh + 1) * HEAD_DK)
                vc = slice(hh * HEAD_DV, (hh + 1) * HEAD_DV)
                s = state[direction, hh]
                o = o_acc[rows, vc] + _mm(qd_ref[rows, kc], s.astype(BF16))
                if finalize[direction]:
                    o = o + other_acc[rows, vc]
                    on = o * lax.rsqrt(jnp.mean(o * o, axis=-1, keepdims=True) + EPS) * gn_ref[...]
                    og_ref[rows, vc] = (on * ga_ref[rows, vc].astype(F32)).astype(BF16)
                else:
                    o_acc[rows, vc] = o
                dec = dt_ref[tiles[direction], kc, c:c + 1]
                state[direction, hh] = dec * s + d_state[direction, hh, c]
        return carry

    first_half = n_tiles // 2
    unroll = 4 if first_half % 4 == 0 else 1
    lax.fori_loop(0, first_half, functools.partial(tile_body, (False, False)), 0, unroll=unroll)
    if n_tiles % 2:
        tile_body((False, True), first_half, 0)
    lax.fori_loop(n_tiles - first_half, n_tiles, functools.partial(tile_body, (True, True)), 0, unroll=unroll)
    sf_ref[...] = state[0]
    sb_ref[...] = state[1]


def _gla(qdf, kif, kef, qdb, kib, keb, dtf, dtb, v, ga, init_states, gla_norm_g, heads):
    B, T, _ = v.shape
    nt = T // TOKEN_TILE
    hg = N_HEADS // heads
    has_init = init_states is not None
    seqs = GLA_SHORT_SEQS_PER_STEP if nt == 1 else 1
    assert B % seqs == 0
    k_spec = pl.BlockSpec((seqs, T, heads * HEAD_DK), lambda b, h: (b, 0, h))
    v_spec = pl.BlockSpec((seqs, T, heads * HEAD_DV), lambda b, h: (b, 0, h))
    dt_spec = pl.BlockSpec((seqs, nt, heads * HEAD_DK, LANES), lambda b, h: (b, 0, h, 0))
    s_spec = pl.BlockSpec((seqs, heads, HEAD_DK, HEAD_DV), lambda b, h: (b, h, 0, 0))
    s_shape = jax.ShapeDtypeStruct((B, N_HEADS, HEAD_DK, HEAD_DV), F32)
    return pl.pallas_call(
        functools.partial(_gla_kernel, heads, nt, has_init, seqs),
        grid=(B // seqs, hg),
        in_specs=([k_spec] * 6 + [dt_spec] * 2 + [v_spec, v_spec] + [s_spec] * (2 if has_init else 0)
                  + [_const_spec((1, HEAD_DV))]),
        out_specs=[v_spec, s_spec, s_spec],
        out_shape=[jax.ShapeDtypeStruct((B, T, VDIM), BF16), s_shape, s_shape],
        scratch_shapes=[pltpu.VMEM((seqs, 2, heads, HEAD_DK, HEAD_DV), F32),
                        pltpu.VMEM((seqs, 2, heads, CHUNKS_PER_TILE, HEAD_DK, HEAD_DV), F32),
                        pltpu.VMEM((seqs, T, heads * HEAD_DV), F32),
                        pltpu.VMEM((seqs, T, heads * HEAD_DV), F32)],
        compiler_params=pltpu.CompilerParams(
            dimension_semantics=("arbitrary", "arbitrary"), vmem_limit_bytes=VMEM_LIMIT_BYTES),
        name="gla",
    )(qdf, kif, kef, qdb, kib, keb, dtf, dtb, v, ga, *(init_states or ()), gla_norm_g)


def _fourier_kernel(seqs, ct_ref, st_ref, a_ref, b_ref, gb_ref, wf_ref, o_ref):
    for bb in range(seqs):
        f = (_mm(ct_ref[...], a_ref[bb]) + _mm(st_ref[...], b_ref[bb])).astype(BF16)
        for g in range(N_FGROUPS):
            cols = slice(g * FGROUP_CH, (g + 1) * FGROUP_CH)
            o_ref[bb, :, cols] = (_mm(f[:, cols], wf_ref[g]) * gb_ref[bb, :, cols].astype(F32)).astype(BF16)


def _fourier(ct, stn, a, bm, gb, wf):
    B, T, _ = a.shape
    tf = min(T, FOURIER_ROWS_PER_STEP)
    seqs = FOURIER_SHORT_SEQS_PER_STEP if T < FOURIER_ROWS_PER_STEP else 1
    assert B % seqs == 0
    tab_spec = pl.BlockSpec((tf, T), lambda b, t: (t, 0))
    seq_spec = pl.BlockSpec((seqs, T, FOURIER_DIM), lambda b, t: (b, 0, 0))
    tile_spec = pl.BlockSpec((seqs, tf, FOURIER_DIM), lambda b, t: (b, t, 0))
    return pl.pallas_call(
        functools.partial(_fourier_kernel, seqs),
        grid=(B // seqs, T // tf),
        in_specs=[tab_spec, tab_spec, seq_spec, seq_spec, tile_spec,
                  _const_spec((N_FGROUPS, FGROUP_CH, FGROUP_CH))],
        out_specs=tile_spec,
        out_shape=jax.ShapeDtypeStruct((B, T, FOURIER_DIM), BF16),
        compiler_params=pltpu.CompilerParams(
            dimension_semantics=("arbitrary", "arbitrary"), vmem_limit_bytes=VMEM_LIMIT_BYTES),
        name="fourier",
    )(ct, stn, a, bm, gb, wf)


def _out_proj_kernel(og_ref, fg_ref, sa_ref, sb_ref, x_ref, gate_ref, fng_ref, wpa_ref, wpb_ref, wo_ref, y_ref):
    for i in range(OUT_TILES_PER_STEP):
        rows = slice(i * TOKEN_TILE, (i + 1) * TOKEN_TILE)
        ya = _mm(og_ref[rows, :], wpa_ref[...])
        yb = _mm(fg_ref[rows, :], wpb_ref[...])
        merged = sa_ref[rows, :].astype(F32) * ya + sb_ref[rows, :].astype(F32) * yb
        xo = x_ref[rows, :] + gate_ref[...] * _mm(merged.astype(BF16), wo_ref[...])
        y_ref[rows, :] = xo * lax.rsqrt(jnp.mean(xo * xo, axis=-1, keepdims=True) + EPS) * fng_ref[...]


def _out_proj(og, fg, sa, sb, x, gate, final_norm_g, wpa, wpb, wo):
    B, T, _ = x.shape
    per_batch_mod = gate.shape[0] > 1
    step_rows = OUT_TILES_PER_STEP * TOKEN_TILE
    if T < step_rows:
        assert not per_batch_mod and (B * T) % step_rows == 0
        fold = lambda a: a.reshape(B * T // step_rows, step_rows, a.shape[-1])
        y = _out_proj(fold(og), fold(fg), fold(sa), fold(sb), fold(x), gate, final_norm_g, wpa, wpb, wo)
        return y.reshape(B, T, D_MODEL)

    def tok_spec(cols):
        return pl.BlockSpec((None, step_rows, cols), lambda b, t: (b, t, 0))

    mod_spec = pl.BlockSpec((None, 1, D_MODEL), (lambda b, t: (b, 0, 0)) if per_batch_mod else (lambda b, t: (0, 0, 0)))
    return pl.pallas_call(
        _out_proj_kernel,
        grid=(B, T // step_rows),
        in_specs=[tok_spec(VDIM), tok_spec(FOURIER_DIM), tok_spec(D_MODEL), tok_spec(D_MODEL), tok_spec(D_MODEL),
                  mod_spec, _const_spec((1, D_MODEL)),
                  _const_spec((VDIM, D_MODEL)), _const_spec((FOURIER_DIM, D_MODEL)), _const_spec((D_MODEL, D_MODEL))],
        out_specs=tok_spec(D_MODEL),
        out_shape=jax.ShapeDtypeStruct((B, T, D_MODEL), F32),
        compiler_params=pltpu.CompilerParams(
            dimension_semantics=("arbitrary", "arbitrary"), vmem_limit_bytes=VMEM_LIMIT_BYTES),
        name="out_proj",
    )(og, fg, sa, sb, x, gate, final_norm_g, wpa, wpb, wo)


def _channel_dft_table():
    n = np.arange(FGROUP_CH)
    ang = 2.0 * np.pi * ((n[:, None] * n[None, :]) % FGROUP_CH) / FGROUP_CH
    tab = np.concatenate([np.cos(ang), np.sin(ang)], axis=1) / np.sqrt(FGROUP_CH)
    return jnp.asarray(tab, F32).astype(BF16)


def _time_dft_tables(T):
    lo = 32
    hi = T // lo
    n = np.arange(T)
    ang_hi = 2.0 * np.pi * (((np.arange(hi)[:, None] * lo) * n[None, :]) % T) / T
    ang_lo = 2.0 * np.pi * ((np.arange(lo)[:, None] * n[None, :]) % T) / T
    scale = 1.0 / np.sqrt(T)
    ch = jnp.asarray(np.cos(ang_hi) * scale, F32)[:, None, :]
    sh = jnp.asarray(np.sin(ang_hi) * scale, F32)[:, None, :]
    cl = jnp.asarray(np.cos(ang_lo), F32)[None, :, :]
    sl = jnp.asarray(np.sin(ang_lo), F32)[None, :, :]
    ct = (ch * cl - sh * sl).astype(BF16).reshape(T, T)
    stn = (-(sh * cl + ch * sl)).astype(BF16).reshape(T, T)
    return ct, stn


def _rope_tables(T):
    rows = T // GRID_W
    r = jnp.repeat(jnp.arange(rows), GRID_W).astype(F32)
    c = jnp.tile(jnp.arange(GRID_W), rows).astype(F32)
    n_freq = HEAD_DK // 4
    freqs = ROPE_BASE ** (-jnp.arange(n_freq, dtype=F32) / n_freq)
    ang_r = r[:, None] * freqs
    ang_c = c[:, None] * freqs
    cos = jnp.concatenate([jnp.cos(ang_r), jnp.cos(ang_r), jnp.cos(ang_c), jnp.cos(ang_c)], axis=-1)
    sin = jnp.concatenate([-jnp.sin(ang_r), jnp.sin(ang_r), -jnp.sin(ang_c), jnp.sin(ang_c)], axis=-1)
    return cos, sin


def _path(x, scale, shift, gate, init_states, rope, heads, wts):
    T = x.shape[1]
    (qdf, kif, kef, qdb, kib, keb, dtf, dtb, v, ga, a, bm, gb, sa, sb) = _in_proj(
        x, scale, shift, wts["norm_g"], wts["w_head"], wts["w_tail"], wts["w_r"], wts["w_ab"], wts["b_ab"],
        wts["cs_tab"], rope)
    og, sf, sbw = _gla(qdf, kif, kef, qdb, kib, keb, dtf, dtb, v, ga, init_states, wts["gla_norm_g"], heads)
    ct, stn = _time_dft_tables(T)
    fg = _fourier(ct, stn, a, bm, gb, wts["w_four"])
    y = _out_proj(og, fg, sa, sb, x, gate, wts["final_norm_g"], wts["w_proj_a"], wts["w_proj_b"], wts["w_out"])
    return y, sf, sbw


def kernel(x_prompt, x_sample, state_gla_fwd, state_gla_bwd, c, c_ctx, w_ada, b_ada, norm_g, w_in,
           w_alpha_fwd, b_alpha_fwd, w_alpha_bwd, b_alpha_bwd, gla_norm_g, w_four, w_proj_a, w_proj_b,
           w_out, final_norm_g):
    depth = w_in.shape[0]
    assert depth == 1, "single trunk layer"
    bs = x_sample.shape[0]

    n_cond = bs + 1
    cond_rows = -(-n_cond // 8) * 8
    cond = jnp.concatenate([c, c_ctx[None, :], jnp.zeros((cond_rows - n_cond, D_MODEL), F32)], axis=0)
    ada = _ada(cond, w_ada[0], b_ada[0][None, :])
    shift, scale, gate = ada[:, :D_MODEL], ada[:, D_MODEL:2 * D_MODEL], ada[:, 2 * D_MODEL:]
    mod = lambda m, lo, hi: m[lo:hi][:, None, :]

    wi = w_in[0]
    o_r = _W_HEAD_COLS
    o_u = o_r + 2 * GATE_RANK
    w_r = jnp.pad(wi[:, o_r:o_u], ((0, 0), (0, LANES - 2 * GATE_RANK))).astype(BF16)
    w_ab = jnp.zeros((LANES, 2 * KDIM), F32)
    w_ab = w_ab.at[:GATE_RANK, :KDIM].set(w_alpha_fwd[0]).at[GATE_RANK:2 * GATE_RANK, KDIM:].set(w_alpha_bwd[0])
    wts = dict(
        norm_g=norm_g[0][None, :], w_head=wi[:, :o_r].astype(BF16), w_tail=wi[:, o_u:].astype(BF16),
        w_r=w_r, w_ab=w_ab.astype(BF16),
        b_ab=jnp.concatenate([b_alpha_fwd[0], b_alpha_bwd[0]])[None, :], cs_tab=_channel_dft_table(),
        gla_norm_g=gla_norm_g[0][None, :], w_four=w_four[0].astype(BF16),
        w_proj_a=w_proj_a[0].astype(BF16), w_proj_b=w_proj_b[0].astype(BF16), w_out=w_out[0].astype(BF16),
        final_norm_g=final_norm_g[None, :])

    y_prompt, sf, sb = _path(x_prompt, mod(scale, bs, bs + 1), mod(shift, bs, bs + 1), mod(gate, bs, bs + 1),
                             None, None, N_HEADS, wts)
    y_sample, _, _ = _path(x_sample, mod(scale, 0, bs), mod(shift, 0, bs), mod(gate, 0, bs),
                           (state_gla_fwd[:, 0], state_gla_bwd[:, 0]), _rope_tables(x_sample.shape[1]), 2, wts)
    return (y_prompt, y_sample, sf[:, None].astype(x_prompt.dtype), sb[:, None].astype(x_prompt.dtype))
```

```python
import functools

import numpy as np
import jax
import jax.numpy as jnp
from jax import lax
from jax.experimental import pallas as pl
from jax.experimental.pallas import tpu as pltpu

F32 = jnp.float32
BF16 = jnp.bfloat16

D_MODEL = 1024
N_HEADS = 4
HEAD_DK = 128
HEAD_DV = 256
KDIM = N_HEADS * HEAD_DK
VDIM = N_HEADS * HEAD_DV
GATE_RANK = 16
GATE_NORM = 16.0
CHUNK = 64
N_FGROUPS = 4
FGROUP_CH = 128
FOURIER_DIM = N_FGROUPS * FGROUP_CH
GRID_W = 64
ROPE_BASE = 10000.0
EPS = 1e-6

LANES = 128
MXU_COLS = 256
TOKEN_TILE = 256
CHUNKS_PER_TILE = TOKEN_TILE // CHUNK
IN_TILES_PER_STEP = 2
OUT_TILES_PER_STEP = 4
GLA_SHORT_SEQS_PER_STEP = 4
FOURIER_ROWS_PER_STEP = 512
FOURIER_SHORT_SEQS_PER_STEP = 8
VMEM_LIMIT_BYTES = 56 * 1024 * 1024

_OFF_QK = 0
_OFF_V = _OFF_QK + 2 * KDIM
_OFF_GA = _OFF_V + VDIM
_W_HEAD_COLS = _OFF_GA + VDIM
_OFF_U = 0
_OFF_GB = _OFF_U + FOURIER_DIM
_OFF_MA = _OFF_GB + FOURIER_DIM
_OFF_MB = _OFF_MA + D_MODEL
_W_TAIL_COLS = _OFF_MB + D_MODEL


def _mm(a, b):
    return jnp.dot(a, b, preferred_element_type=F32)


def _mm_ta(a, b):
    return lax.dot_general(a, b, (((0,), (0,)), ((), ())), preferred_element_type=F32)


def _mm_tb(a, b):
    return lax.dot_general(a, b, (((1,), (1,)), ((), ())), preferred_element_type=F32)


def _split_bf16(x):
    hi = x.astype(BF16)
    lo = (x - hi.astype(F32)).astype(BF16)
    return hi, lo


def _const_spec(shape):
    nd = len(shape)
    return pl.BlockSpec(shape, lambda *_: (0,) * nd)


def _ada_kernel(c_ref, w_ref, b_ref, o_ref):
    c = c_ref[...]
    s = c * jax.nn.sigmoid(c)
    o_ref[...] = _mm(s.astype(BF16), w_ref[...].astype(BF16)) + b_ref[...]


def _ada(cond, w, b):
    rows = cond.shape[0]
    return pl.pallas_call(
        _ada_kernel,
        grid=(3,),
        in_specs=[pl.BlockSpec((rows, D_MODEL), lambda n: (0, 0)),
                  pl.BlockSpec((D_MODEL, D_MODEL), lambda n: (0, n)),
                  pl.BlockSpec((1, D_MODEL), lambda n: (0, n))],
        out_specs=pl.BlockSpec((rows, D_MODEL), lambda n: (0, n)),
        out_shape=jax.ShapeDtypeStruct((rows, 3 * D_MODEL), F32),
        compiler_params=pltpu.CompilerParams(
            dimension_semantics=("arbitrary",), vmem_limit_bytes=VMEM_LIMIT_BYTES),
        name="ada",
    )(cond, w, b)


def _in_proj_kernel(use_rope, *refs):
    n_shared = 9
    n_in = 1 + n_shared + (2 if use_rope else 0)
    dt_slots = (n_in + 6, n_in + 7)
    for i in range(IN_TILES_PER_STEP):
        rows = pl.ds(i * TOKEN_TILE, TOKEN_TILE)
        tile_refs = []
        for idx, ref in enumerate(refs):
            if 1 <= idx <= n_shared:
                tile_refs.append(ref)
            elif idx in dt_slots:
                tile_refs.append(ref.at[i])
            else:
                tile_refs.append(ref.at[rows])
        _in_proj_tile(use_rope, *tile_refs)


def _project(hb, w_ref, off, width, act, out_ref):
    for n in range(0, width, MXU_COLS):
        z = _mm(hb, w_ref[:, off + n:off + n + MXU_COLS])
        out_ref[:, n:n + MXU_COLS] = act(z).astype(BF16)


def _in_proj_tile(use_rope, *refs):
    if use_rope:
        (x_ref, sc_ref, sh_ref, ng_ref, wa_ref, wb_ref, wr_ref, wab_ref, bab_ref, cs_ref, cos_ref, sin_ref,
         qdf_ref, kif_ref, kef_ref, qdb_ref, kib_ref, keb_ref, dtf_ref, dtb_ref,
         v_ref, ga_ref, a_ref, b_ref, gb_ref, sa_ref, sb_ref) = refs
    else:
        (x_ref, sc_ref, sh_ref, ng_ref, wa_ref, wb_ref, wr_ref, wab_ref, bab_ref, cs_ref,
         qdf_ref, kif_ref, kef_ref, qdb_ref, kib_ref, keb_ref, dtf_ref, dtb_ref,
         v_ref, ga_ref, a_ref, b_ref, gb_ref, sa_ref, sb_ref) = refs

    x = x_ref[...]
    xn = x * lax.rsqrt(jnp.mean(x * x, axis=-1, keepdims=True) + EPS)
    h = (xn * ng_ref[...]) * (1.0 + sc_ref[...]) + sh_ref[...]
    hb = h.astype(BF16)

    r = _mm(hb, wr_ref[...])
    qk = _mm(hb, wa_ref[:, _OFF_QK:_OFF_QK + 2 * KDIM])
    xg = _mm(r.astype(BF16), wab_ref[...]) + bab_ref[...]
    _project(hb, wa_ref, _OFF_V, VDIM, lambda z: z, v_ref)
    g_all = (jnp.minimum(xg, 0.0) - jnp.log1p(jnp.exp(-jnp.abs(xg)))) * (1.0 / GATE_NORM)

    q = qk[:, :KDIM] * (HEAD_DK ** -0.5)
    k = qk[:, KDIM:]
    if use_rope:
        cos = cos_ref[...]
        sin = sin_ref[...]
        lane = lax.broadcasted_iota(jnp.int32, (TOKEN_TILE, HEAD_DK), 1)
        first_half = (lane // (HEAD_DK // 4)) % 2 == 0

        def rope(t):
            outs = []
            for hh in range(N_HEADS):
                th = t[:, hh * HEAD_DK:(hh + 1) * HEAD_DK]
                partner = jnp.where(first_half,
                                    pltpu.roll(th, HEAD_DK - HEAD_DK // 4, axis=1),
                                    pltpu.roll(th, HEAD_DK // 4, axis=1))
                outs.append(th * cos + partner * sin)
            return jnp.concatenate(outs, axis=1)

        q = rope(q)
        k = rope(k)

    ub = _mm(hb, wb_ref[:, _OFF_U:_OFF_U + FOURIER_DIM]).astype(BF16)

    row = lax.broadcasted_iota(jnp.int32, (CHUNK, 2 * CHUNK), 0)
    col = lax.broadcasted_iota(jnp.int32, (CHUNK, 2 * CHUNK), 1) % CHUNK
    bcs = []
    for direction in range(2):
        g_hi, g_lo = _split_bf16(g_all[:, direction * KDIM:(direction + 1) * KDIM])
        tri = jnp.where((col <= row) if direction == 0 else (col >= row), 1.0, 0.0).astype(BF16)
        bc_chunks = []
        for c in range(CHUNKS_PER_TILE):
            cr = slice(c * CHUNK, (c + 1) * CHUNK)
            bc_chunks.append(_mm(tri, jnp.concatenate([g_hi[cr], g_lo[cr]], axis=0)))
        bcs.append(jnp.concatenate(bc_chunks, axis=0))

    _project(hb, wa_ref, _OFF_GA, VDIM, jax.nn.silu, ga_ref)

    n_parts = 4 * CHUNKS_PER_TILE
    sel_r = lax.broadcasted_iota(jnp.int32, (n_parts, LANES), 0)
    sel_c = lax.broadcasted_iota(jnp.int32, (n_parts, LANES), 1)
    part_sel = jnp.where((sel_r % CHUNKS_PER_TILE == sel_c) & (sel_r < 3 * CHUNKS_PER_TILE), 1.0, 0.0).astype(BF16)
    for direction, (qd_ref, ki_ref, ke_ref, dt_ref) in enumerate(
            ((qdf_ref, kif_ref, kef_ref, dtf_ref), (qdb_ref, kib_ref, keb_ref, dtb_ref))):
        bc = bcs[direction]
        edge = CHUNK - 1 if direction == 0 else 0
        bl_rows = bc.reshape(CHUNKS_PER_TILE, CHUNK, KDIM)[:, edge, :]
        bl = jnp.broadcast_to(bl_rows[:, None, :], (CHUNKS_PER_TILE, CHUNK, KDIM)).reshape(TOKEN_TILE, KDIM)
        qd_ref[...] = (q * jnp.exp(bc)).astype(BF16)
        ki_ref[...] = (k * jnp.exp(-bc)).astype(BF16)
        ke_ref[...] = (k * jnp.exp(bl - bc)).astype(BF16)
        p0 = bl_rows.astype(BF16).astype(F32)
        p1 = (bl_rows - p0).astype(BF16).astype(F32)
        p2 = (bl_rows - p0 - p1).astype(BF16).astype(F32)
        parts = jnp.concatenate([p0, p1, p2, jnp.zeros_like(p0)], axis=0).astype(BF16)
        dt_ref[...] = jnp.exp(_mm_ta(parts, part_sel))

    for g in range(N_FGROUPS):
        ab = _mm(ub[:, g * FGROUP_CH:(g + 1) * FGROUP_CH], cs_ref[...])
        a_ref[:, g * FGROUP_CH:(g + 1) * FGROUP_CH] = ab[:, :FGROUP_CH].astype(BF16)
        b_ref[:, g * FGROUP_CH:(g + 1) * FGROUP_CH] = ab[:, FGROUP_CH:].astype(BF16)

    _project(hb, wb_ref, _OFF_GB, FOURIER_DIM, jax.nn.silu, gb_ref)
    _project(hb, wb_ref, _OFF_MA, D_MODEL, jax.nn.sigmoid, sa_ref)
    _project(hb, wb_ref, _OFF_MB, D_MODEL, jax.nn.sigmoid, sb_ref)


def _in_proj(x, scale, shift, norm_g, w_head, w_tail, w_r, w_ab, b_ab, cs_tab, rope):
    B, T, _ = x.shape
    use_rope = rope is not None
    per_batch_mod = scale.shape[0] > 1
    step_rows = IN_TILES_PER_STEP * TOKEN_TILE
    if T < step_rows:
        assert not per_batch_mod and not use_rope and (B * T) % step_rows == 0
        outs = _in_proj(x.reshape(B * T // step_rows, step_rows, D_MODEL), scale, shift, norm_g,
                        w_head, w_tail, w_r, w_ab, b_ab, cs_tab, rope)
        return [o.reshape((B, T // TOKEN_TILE) + o.shape[2:]) if o.ndim == 4 else o.reshape(B, T, o.shape[-1])
                for o in outs]
    nt = T // TOKEN_TILE

    def tok_spec(cols):
        return pl.BlockSpec((None, step_rows, cols), lambda b, t: (b, t, 0))

    mod_spec = pl.BlockSpec((None, 1, D_MODEL), (lambda b, t: (b, 0, 0)) if per_batch_mod else (lambda b, t: (0, 0, 0)))
    in_specs = [
        tok_spec(D_MODEL), mod_spec, mod_spec, _const_spec((1, D_MODEL)),
        _const_spec((D_MODEL, _W_HEAD_COLS)), _const_spec((D_MODEL, _W_TAIL_COLS)), _const_spec((D_MODEL, LANES)),
        _const_spec((LANES, 2 * KDIM)), _const_spec((1, 2 * KDIM)),
        _const_spec((FGROUP_CH, 2 * FGROUP_CH)),
    ]
    args = [x, scale, shift, norm_g, w_head, w_tail, w_r, w_ab, b_ab, cs_tab]
    if use_rope:
        in_specs += [pl.BlockSpec((step_rows, HEAD_DK), lambda b, t: (t, 0))] * 2
        args += list(rope)

    dt_spec = pl.BlockSpec((None, IN_TILES_PER_STEP, KDIM, LANES), lambda b, t: (b, t, 0, 0))
    tok_bf = lambda cols: jax.ShapeDtypeStruct((B, T, cols), BF16)
    dt_shape = jax.ShapeDtypeStruct((B, nt, KDIM, LANES), F32)
    out_specs = [tok_spec(KDIM)] * 6 + [dt_spec] * 2 + [
        tok_spec(VDIM), tok_spec(VDIM), tok_spec(FOURIER_DIM), tok_spec(FOURIER_DIM),
        tok_spec(FOURIER_DIM), tok_spec(D_MODEL), tok_spec(D_MODEL)]
    out_shape = [tok_bf(KDIM)] * 6 + [dt_shape] * 2 + [
        tok_bf(VDIM), tok_bf(VDIM), tok_bf(FOURIER_DIM), tok_bf(FOURIER_DIM),
        tok_bf(FOURIER_DIM), tok_bf(D_MODEL), tok_bf(D_MODEL)]
    return pl.pallas_call(
        functools.partial(_in_proj_kernel, use_rope),
        grid=(B, T // step_rows),
        in_specs=in_specs,
        out_specs=out_specs,
        out_shape=out_shape,
        compiler_params=pltpu.CompilerParams(
            dimension_semantics=("arbitrary", "arbitrary"), vmem_limit_bytes=VMEM_LIMIT_BYTES),
        name="in_proj_rope" if use_rope else "in_proj",
    )(*args)


def _gla_kernel(heads, n_tiles, has_init, seqs, *refs):
    gain_pos = 10 + (2 if has_init else 0)
    for bb in range(seqs):
        _gla_seq(heads, n_tiles, has_init, *[r if i == gain_pos else r.at[bb] for i, r in enumerate(refs)])


def _gla_seq(heads, n_tiles, has_init, *refs):
    (qdf_ref, kif_ref, kef_ref, qdb_ref, kib_ref, keb_ref, dtf_ref, dtb_ref, v_ref, ga_ref) = refs[:10]
    refs = refs[10:]
    if has_init:
        s0f_ref, s0b_ref = refs[:2]
        refs = refs[2:]
    gn_ref, og_ref, sf_ref, sb_ref, state, d_state, of_acc, ob_acc = refs
    if has_init:
        state[0] = s0f_ref[...]
        state[1] = s0b_ref[...]
    else:
        state[...] = jnp.zeros_like(state)
    dir_refs = ((qdf_ref, kif_ref, kef_ref, dtf_ref, of_acc), (qdb_ref, kib_ref, keb_ref, dtb_ref, ob_acc))

    def tile_body(finalize, j, carry):
        row = lax.broadcasted_iota(jnp.int32, (TOKEN_TILE, TOKEN_TILE), 0)
        col = lax.broadcasted_iota(jnp.int32, (TOKEN_TILE, TOKEN_TILE), 1)
        same_chunk = (row // CHUNK) == (col // CHUNK)
        causal = (same_chunk & (col <= row), same_chunk & (col >= row))
        tiles = (j, n_tiles - 1 - j)
        bases = tuple(pl.multiple_of(t * TOKEN_TILE, TOKEN_TILE) for t in tiles)

        for direction in range(2):
            qd_ref, ki_ref, ke_ref, _, o_acc = dir_refs[direction]
            trows = pl.ds(bases[direction], TOKEN_TILE)
            for hh in range(heads):
                kc = slice(hh * HEAD_DK, (hh + 1) * HEAD_DK)
                vc = slice(hh * HEAD_DV, (hh + 1) * HEAD_DV)
                scores = jnp.where(causal[direction], _mm_tb(qd_ref[trows, kc], ki_ref[trows, kc]), 0.0)
                o_acc[trows, vc] = _mm(scores.astype(BF16), v_ref[trows, vc])
                for c in range(CHUNKS_PER_TILE):
                    rows = pl.ds(bases[direction] + c * CHUNK, CHUNK)
                    d_state[direction, hh, c] = _mm_ta(ke_ref[rows, kc], v_ref[rows, vc])

        steps = [(step, direction) for step in range(CHUNKS_PER_TILE) for direction in range(2)]
        if finalize[0] != finalize[1]:
            steps.sort(key=lambda sd: finalize[sd[1]])
        for step, direction in steps:
            qd_ref, _, _, dt_ref, o_acc = dir_refs[direction]
            other_acc = dir_refs[1 - direction][4]
            c = step if direction == 0 else CHUNKS_PER_TILE - 1 - step
            rows = pl.ds(bases[direction] + c * CHUNK, CHUNK)
            for hh in range(heads):
                kc = slice(hh * HEAD_DK, (hh + 1) * HEAD_DK)
                vc = slice(hh * HEAD_DV, (hh + 1) * HEAD_DV)
                s = state[direction, hh]
                o = o_acc[rows, vc] + _mm(qd_ref[rows, kc], s.astype(BF16))
                if finalize[direction]:
                    o = o + other_acc[rows, vc]
                    on = o * lax.rsqrt(jnp.mean(o * o, axis=-1, keepdims=True) + EPS) * gn_ref[...]
                    og_ref[rows, vc] = (on * ga_ref[rows, vc].astype(F32)).astype(BF16)
                else:
                    o_acc[rows, vc] = o
                dec = dt_ref[tiles[direction], kc, c:c + 1]
                state[direction, hh] = dec * s + d_state[direction, hh, c]
        return carry

    first_half = n_tiles // 2
    unroll = 4 if first_half % 4 == 0 else 1
    lax.fori_loop(0, first_half, functools.partial(tile_body, (False, False)), 0, unroll=unroll)
    if n_tiles % 2:
        tile_body((False, True), first_half, 0)
    lax.fori_loop(n_tiles - first_half, n_tiles, functools.partial(tile_body, (True, True)), 0, unroll=unroll)
    sf_ref[...] = state[0]
    sb_ref[...] = state[1]


def _gla(qdf, kif, kef, qdb, kib, keb, dtf, dtb, v, ga, init_states, gla_norm_g, heads):
    B, T, _ = v.shape
    nt = T // TOKEN_TILE
    hg = N_HEADS // heads
    has_init = init_states is not None
    seqs = GLA_SHORT_SEQS_PER_STEP if nt == 1 else 1
    assert B % seqs == 0
    k_spec = pl.BlockSpec((seqs, T, heads * HEAD_DK), lambda b, h: (b, 0, h))
    v_spec = pl.BlockSpec((seqs, T, heads * HEAD_DV), lambda b, h: (b, 0, h))
    dt_spec = pl.BlockSpec((seqs, nt, heads * HEAD_DK, LANES), lambda b, h: (b, 0, h, 0))
    s_spec = pl.BlockSpec((seqs, heads, HEAD_DK, HEAD_DV), lambda b, h: (b, h, 0, 0))
    s_shape = jax.ShapeDtypeStruct((B, N_HEADS, HEAD_DK, HEAD_DV), F32)
    return pl.pallas_call(
        functools.partial(_gla_kernel, heads, nt, has_init, seqs),
        grid=(B // seqs, hg),
        in_specs=([k_spec] * 6 + [dt_spec] * 2 + [v_spec, v_spec] + [s_spec] * (2 if has_init else 0)
                  + [_const_spec((1, HEAD_DV))]),
        out_specs=[v_spec, s_spec, s_spec],
        out_shape=[jax.ShapeDtypeStruct((B, T, VDIM), BF16), s_shape, s_shape],
        scratch_shapes=[pltpu.VMEM((seqs, 2, heads, HEAD_DK, HEAD_DV), F32),
                        pltpu.VMEM((seqs, 2, heads, CHUNKS_PER_TILE, HEAD_DK, HEAD_DV), F32),
                        pltpu.VMEM((seqs, T, heads * HEAD_DV), F32),
                        pltpu.VMEM((seqs, T, heads * HEAD_DV), F32)],
        compiler_params=pltpu.CompilerParams(
            dimension_semantics=("arbitrary", "arbitrary"), vmem_limit_bytes=VMEM_LIMIT_BYTES),
        name="gla",
    )(qdf, kif, kef, qdb, kib, keb, dtf, dtb, v, ga, *(init_states or ()), gla_norm_g)


def _fourier_kernel(seqs, ct_ref, st_ref, a_ref, b_ref, gb_ref, wf_ref, o_ref):
    for bb in range(seqs):
        f = (_mm(ct_ref[...], a_ref[bb]) + _mm(st_ref[...], b_ref[bb])).astype(BF16)
        for g in range(N_FGROUPS):
            cols = slice(g * FGROUP_CH, (g + 1) * FGROUP_CH)
            o_ref[bb, :, cols] = (_mm(f[:, cols], wf_ref[g]) * gb_ref[bb, :, cols].astype(F32)).astype(BF16)


def _fourier(ct, stn, a, bm, gb, wf):
    B, T, _ = a.shape
    tf = min(T, FOURIER_ROWS_PER_STEP)
    seqs = FOURIER_SHORT_SEQS_PER_STEP if T < FOURIER_ROWS_PER_STEP else 1
    assert B % seqs == 0
    tab_spec = pl.BlockSpec((tf, T), lambda b, t: (t, 0))
    seq_spec = pl.BlockSpec((seqs, T, FOURIER_DIM), lambda b, t: (b, 0, 0))
    tile_spec = pl.BlockSpec((seqs, tf, FOURIER_DIM), lambda b, t: (b, t, 0))
    return pl.pallas_call(
        functools.partial(_fourier_kernel, seqs),
        grid=(B // seqs, T // tf),
        in_specs=[tab_spec, tab_spec, seq_spec, seq_spec, tile_spec,
                  _const_spec((N_FGROUPS, FGROUP_CH, FGROUP_CH))],
        out_specs=tile_spec,
        out_shape=jax.ShapeDtypeStruct((B, T, FOURIER_DIM), BF16),
        compiler_params=pltpu.CompilerParams(
            dimension_semantics=("arbitrary", "arbitrary"), vmem_limit_bytes=VMEM_LIMIT_BYTES),
        name="fourier",
    )(ct, stn, a, bm, gb, wf)


def _fourier_split_kernel(ce_ref, se_ref, co_ref, so_ref, a_ref, b_ref, gb_ref, wf_ref, o_ref,
                          wide, a_eo, b_eo):
    half = a_eo.shape[1]
    for src, dst in ((a_ref, a_eo), (b_ref, b_eo)):
        for g in range(N_FGROUPS):
            cols = slice(g * FGROUP_CH, (g + 1) * FGROUP_CH)
            wide[g] = src[:, cols].astype(F32)
            for parity in range(2):
                dst[parity, :, cols] = wide[g, pl.ds(parity, half, stride=2), :].astype(BF16)

    for kb in range(half // FOURIER_ROWS_PER_STEP):
        rows = slice(kb * FOURIER_ROWS_PER_STEP, (kb + 1) * FOURIER_ROWS_PER_STEP)
        e = _mm(ce_ref[rows, :], a_eo[0]) + _mm(se_ref[rows, :], b_eo[0])
        o = _mm(co_ref[rows, :], a_eo[1]) + _mm(so_ref[rows, :], b_eo[1])
        for upper, f in enumerate(((e + o).astype(BF16), (e - o).astype(BF16))):
            for g in range(N_FGROUPS):
                cols = slice(g * FGROUP_CH, (g + 1) * FGROUP_CH)
                o_ref[upper, rows, cols] = (
                    _mm(f[:, cols], wf_ref[g]) * gb_ref[upper, rows, cols].astype(F32)).astype(BF16)


def _fourier_split(a, bm, gb, wf):
    B, T, _ = a.shape
    half = T // 2
    ce, sen = _time_dft_tables(T, half, 2 * np.arange(half))
    co, son = _time_dft_tables(T, half, 2 * np.arange(half) + 1)
    seq_spec = pl.BlockSpec((None, T, FOURIER_DIM), lambda b: (b, 0, 0))
    halves_spec = pl.BlockSpec((None, 2, half, FOURIER_DIM), lambda b: (b, 0, 0, 0))
    out = pl.pallas_call(
        _fourier_split_kernel,
        grid=(B,),
        in_specs=[_const_spec((half, half))] * 4 + [seq_spec, seq_spec, halves_spec,
                                                     _const_spec((N_FGROUPS, FGROUP_CH, FGROUP_CH))],
        out_specs=halves_spec,
        out_shape=jax.ShapeDtypeStruct((B, 2, half, FOURIER_DIM), BF16),
        scratch_shapes=[pltpu.VMEM((N_FGROUPS, T, FGROUP_CH), F32),
                        pltpu.VMEM((2, half, FOURIER_DIM), BF16),
                        pltpu.VMEM((2, half, FOURIER_DIM), BF16)],
        compiler_params=pltpu.CompilerParams(
            dimension_semantics=("arbitrary",), vmem_limit_bytes=VMEM_LIMIT_BYTES),
        name="fourier_split",
    )(ce, sen, co, son, a, bm, gb.reshape(B, 2, half, FOURIER_DIM), wf)
    return out.reshape(B, T, FOURIER_DIM)


def _out_proj_kernel(og_ref, fg_ref, sa_ref, sb_ref, x_ref, gate_ref, fng_ref, wpa_ref, wpb_ref, wo_ref, y_ref):
    for i in range(OUT_TILES_PER_STEP):
        rows = slice(i * TOKEN_TILE, (i + 1) * TOKEN_TILE)
        ya = _mm(og_ref[rows, :], wpa_ref[...])
        yb = _mm(fg_ref[rows, :], wpb_ref[...])
        merged = sa_ref[rows, :].astype(F32) * ya + sb_ref[rows, :].astype(F32) * yb
        xo = x_ref[rows, :] + gate_ref[...] * _mm(merged.astype(BF16), wo_ref[...])
        y_ref[rows, :] = xo * lax.rsqrt(jnp.mean(xo * xo, axis=-1, keepdims=True) + EPS) * fng_ref[...]


def _out_proj(og, fg, sa, sb, x, gate, final_norm_g, wpa, wpb, wo):
    B, T, _ = x.shape
    per_batch_mod = gate.shape[0] > 1
    step_rows = OUT_TILES_PER_STEP * TOKEN_TILE
    if T < step_rows:
        assert not per_batch_mod and (B * T) % step_rows == 0
        fold = lambda a: a.reshape(B * T // step_rows, step_rows, a.shape[-1])
        y = _out_proj(fold(og), fold(fg), fold(sa), fold(sb), fold(x), gate, final_norm_g, wpa, wpb, wo)
        return y.reshape(B, T, D_MODEL)

    def tok_spec(cols):
        return pl.BlockSpec((None, step_rows, cols), lambda b, t: (b, t, 0))

    mod_spec = pl.BlockSpec((None, 1, D_MODEL), (lambda b, t: (b, 0, 0)) if per_batch_mod else (lambda b, t: (0, 0, 0)))
    return pl.pallas_call(
        _out_proj_kernel,
        grid=(B, T // step_rows),
        in_specs=[tok_spec(VDIM), tok_spec(FOURIER_DIM), tok_spec(D_MODEL), tok_spec(D_MODEL), tok_spec(D_MODEL),
                  mod_spec, _const_spec((1, D_MODEL)),
                  _const_spec((VDIM, D_MODEL)), _const_spec((FOURIER_DIM, D_MODEL)), _const_spec((D_MODEL, D_MODEL))],
        out_specs=tok_spec(D_MODEL),
        out_shape=jax.ShapeDtypeStruct((B, T, D_MODEL), F32),
        compiler_params=pltpu.CompilerParams(
            dimension_semantics=("arbitrary", "arbitrary"), vmem_limit_bytes=VMEM_LIMIT_BYTES),
        name="out_proj",
    )(og, fg, sa, sb, x, gate, final_norm_g, wpa, wpb, wo)


def _channel_dft_table():
    n = np.arange(FGROUP_CH)
    ang = 2.0 * np.pi * ((n[:, None] * n[None, :]) % FGROUP_CH) / FGROUP_CH
    tab = np.concatenate([np.cos(ang), np.sin(ang)], axis=1) / np.sqrt(FGROUP_CH)
    return jnp.asarray(tab, F32).astype(BF16)


def _time_dft_tables(T, n_rows, positions):
    lo = 32
    hi = n_rows // lo
    n = np.asarray(positions)
    ang_hi = 2.0 * np.pi * (((np.arange(hi)[:, None] * lo) * n[None, :]) % T) / T
    ang_lo = 2.0 * np.pi * ((np.arange(lo)[:, None] * n[None, :]) % T) / T
    scale = 1.0 / np.sqrt(T)
    ch = jnp.asarray(np.cos(ang_hi) * scale, F32)[:, None, :]
    sh = jnp.asarray(np.sin(ang_hi) * scale, F32)[:, None, :]
    cl = jnp.asarray(np.cos(ang_lo), F32)[None, :, :]
    sl = jnp.asarray(np.sin(ang_lo), F32)[None, :, :]
    ct = (ch * cl - sh * sl).astype(BF16).reshape(n_rows, n.size)
    stn = (-(sh * cl + ch * sl)).astype(BF16).reshape(n_rows, n.size)
    return ct, stn


def _rope_tables(T):
    rows = T // GRID_W
    r = jnp.repeat(jnp.arange(rows), GRID_W).astype(F32)
    c = jnp.tile(jnp.arange(GRID_W), rows).astype(F32)
    n_freq = HEAD_DK // 4
    freqs = ROPE_BASE ** (-jnp.arange(n_freq, dtype=F32) / n_freq)
    ang_r = r[:, None] * freqs
    ang_c = c[:, None] * freqs
    cos = jnp.concatenate([jnp.cos(ang_r), jnp.cos(ang_r), jnp.cos(ang_c), jnp.cos(ang_c)], axis=-1)
    sin = jnp.concatenate([-jnp.sin(ang_r), jnp.sin(ang_r), -jnp.sin(ang_c), jnp.sin(ang_c)], axis=-1)
    return cos, sin


def _path(x, scale, shift, gate, init_states, rope, heads, wts):
    T = x.shape[1]
    (qdf, kif, kef, qdb, kib, keb, dtf, dtb, v, ga, a, bm, gb, sa, sb) = _in_proj(
        x, scale, shift, wts["norm_g"], wts["w_head"], wts["w_tail"], wts["w_r"], wts["w_ab"], wts["b_ab"],
        wts["cs_tab"], rope)
    og, sf, sbw = _gla(qdf, kif, kef, qdb, kib, keb, dtf, dtb, v, ga, init_states, wts["gla_norm_g"], heads)
    if T >= 4 * FOURIER_ROWS_PER_STEP:
        fg = _fourier_split(a, bm, gb, wts["w_four"])
    else:
        ct, stn = _time_dft_tables(T, T, np.arange(T))
        fg = _fourier(ct, stn, a, bm, gb, wts["w_four"])
    y = _out_proj(og, fg, sa, sb, x, gate, wts["final_norm_g"], wts["w_proj_a"], wts["w_proj_b"], wts["w_out"])
    return y, sf, sbw


def kernel(x_prompt, x_sample, state_gla_fwd, state_gla_bwd, c, c_ctx, w_ada, b_ada, norm_g, w_in,
           w_alpha_fwd, b_alpha_fwd, w_alpha_bwd, b_alpha_bwd, gla_norm_g, w_four, w_proj_a, w_proj_b,
           w_out, final_norm_g):
    depth = w_in.shape[0]
    assert depth == 1, "single trunk layer"
    bs = x_sample.shape[0]

    n_cond = bs + 1
    cond_rows = -(-n_cond // 8) * 8
    cond = jnp.concatenate([c, c_ctx[None, :], jnp.zeros((cond_rows - n_cond, D_MODEL), F32)], axis=0)
    ada = _ada(cond, w_ada[0], b_ada[0][None, :])
    shift, scale, gate = ada[:, :D_MODEL], ada[:, D_MODEL:2 * D_MODEL], ada[:, 2 * D_MODEL:]
    mod = lambda m, lo, hi: m[lo:hi][:, None, :]

    wi = w_in[0]
    o_r = _W_HEAD_COLS
    o_u = o_r + 2 * GATE_RANK
    w_r = jnp.pad(wi[:, o_r:o_u], ((0, 0), (0, LANES - 2 * GATE_RANK))).astype(BF16)
    w_ab = jnp.zeros((LANES, 2 * KDIM), F32)
    w_ab = w_ab.at[:GATE_RANK, :KDIM].set(w_alpha_fwd[0]).at[GATE_RANK:2 * GATE_RANK, KDIM:].set(w_alpha_bwd[0])
    wts = dict(
        norm_g=norm_g[0][None, :], w_head=wi[:, :o_r].astype(BF16), w_tail=wi[:, o_u:].astype(BF16),
        w_r=w_r, w_ab=w_ab.astype(BF16),
        b_ab=jnp.concatenate([b_alpha_fwd[0], b_alpha_bwd[0]])[None, :], cs_tab=_channel_dft_table(),
        gla_norm_g=gla_norm_g[0][None, :], w_four=w_four[0].astype(BF16),
        w_proj_a=w_proj_a[0].astype(BF16), w_proj_b=w_proj_b[0].astype(BF16), w_out=w_out[0].astype(BF16),
        final_norm_g=final_norm_g[None, :])

    y_prompt, sf, sb = _path(x_prompt, mod(scale, bs, bs + 1), mod(shift, bs, bs + 1), mod(gate, bs, bs + 1),
                             None, None, N_HEADS, wts)
    y_sample, _, _ = _path(x_sample, mod(scale, 0, bs), mod(shift, 0, bs), mod(gate, 0, bs),
                           (state_gla_fwd[:, 0], state_gla_bwd[:, 0]), _rope_tables(x_sample.shape[1]), 2, wts)
    return (y_prompt, y_sample, sf[:, None].astype(x_prompt.dtype), sb[:, None].astype(x_prompt.dtype))
```

```python
import functools

import numpy as np
import jax
import jax.numpy as jnp
from jax import lax
from jax.experimental import pallas as pl
from jax.experimental.pallas import tpu as pltpu

F32 = jnp.float32
BF16 = jnp.bfloat16

D_MODEL = 1024
N_HEADS = 4
HEAD_DK = 128
HEAD_DV = 256
KDIM = N_HEADS * HEAD_DK
VDIM = N_HEADS * HEAD_DV
GATE_RANK = 16
GATE_NORM = 16.0
CHUNK = 64
N_FGROUPS = 4
FGROUP_CH = 128
FOURIER_DIM = N_FGROUPS * FGROUP_CH
GRID_W = 64
ROPE_BASE = 10000.0
EPS = 1e-6

LANES = 128
MXU_COLS = 256
TOKEN_TILE = 256
CHUNKS_PER_TILE = TOKEN_TILE // CHUNK
IN_TILES_PER_STEP = 2
OUT_TILES_PER_STEP = 4
GLA_SHORT_SEQS_PER_STEP = 4
FOURIER_ROWS_PER_STEP = 512
FOURIER_SHORT_SEQS_PER_STEP = 8
VMEM_LIMIT_BYTES = 56 * 1024 * 1024

_OFF_QK = 0
_OFF_V = _OFF_QK + 2 * KDIM
_OFF_GA = _OFF_V + VDIM
_W_HEAD_COLS = _OFF_GA + VDIM
_OFF_U = 0
_OFF_GB = _OFF_U + FOURIER_DIM
_OFF_MA = _OFF_GB + FOURIER_DIM
_OFF_MB = _OFF_MA + D_MODEL
_W_TAIL_COLS = _OFF_MB + D_MODEL


def _mm(a, b):
    return jnp.dot(a, b, preferred_element_type=F32)


def _mm_ta(a, b):
    return lax.dot_general(a, b, (((0,), (0,)), ((), ())), preferred_element_type=F32)


def _mm_tb(a, b):
    return lax.dot_general(a, b, (((1,), (1,)), ((), ())), preferred_element_type=F32)


def _split_bf16(x):
    hi = x.astype(BF16)
    lo = (x - hi.astype(F32)).astype(BF16)
    return hi, lo


def _const_spec(shape):
    nd = len(shape)
    return pl.BlockSpec(shape, lambda *_: (0,) * nd)


def _ada_kernel(c_ref, w_ref, b_ref, o_ref):
    c = c_ref[...]
    s = c * jax.nn.sigmoid(c)
    o_ref[...] = _mm(s.astype(BF16), w_ref[...].astype(BF16)) + b_ref[...]


def _ada(cond, w, b):
    rows = cond.shape[0]
    return pl.pallas_call(
        _ada_kernel,
        grid=(3,),
        in_specs=[pl.BlockSpec((rows, D_MODEL), lambda n: (0, 0)),
                  pl.BlockSpec((D_MODEL, D_MODEL), lambda n: (0, n)),
                  pl.BlockSpec((1, D_MODEL), lambda n: (0, n))],
        out_specs=pl.BlockSpec((rows, D_MODEL), lambda n: (0, n)),
        out_shape=jax.ShapeDtypeStruct((rows, 3 * D_MODEL), F32),
        compiler_params=pltpu.CompilerParams(
            dimension_semantics=("arbitrary",), vmem_limit_bytes=VMEM_LIMIT_BYTES),
        name="ada",
    )(cond, w, b)


def _in_proj_kernel(use_rope, *refs):
    n_shared = 8
    n_in = 1 + n_shared + (2 if use_rope else 0)
    dt_slots = (n_in + 6, n_in + 7)
    for i in range(IN_TILES_PER_STEP):
        rows = pl.ds(i * TOKEN_TILE, TOKEN_TILE)
        tile_refs = []
        for idx, ref in enumerate(refs):
            if 1 <= idx <= n_shared:
                tile_refs.append(ref)
            elif idx in dt_slots:
                tile_refs.append(ref.at[i])
            else:
                tile_refs.append(ref.at[rows])
        _in_proj_tile(use_rope, *tile_refs)


def _project(hb, w_ref, off, width, act, out_ref):
    for n in range(0, width, MXU_COLS):
        z = _mm(hb, w_ref[:, off + n:off + n + MXU_COLS])
        out_ref[:, n:n + MXU_COLS] = act(z).astype(BF16)


def _in_proj_tile(use_rope, *refs):
    if use_rope:
        (x_ref, sc_ref, sh_ref, ng_ref, wa_ref, wb_ref, wr_ref, wab_ref, bab_ref, cos_ref, sin_ref,
         qdf_ref, kif_ref, kef_ref, qdb_ref, kib_ref, keb_ref, dtf_ref, dtb_ref,
         v_ref, ga_ref, u_ref, gb_ref, sa_ref, sb_ref) = refs
    else:
        (x_ref, sc_ref, sh_ref, ng_ref, wa_ref, wb_ref, wr_ref, wab_ref, bab_ref,
         qdf_ref, kif_ref, kef_ref, qdb_ref, kib_ref, keb_ref, dtf_ref, dtb_ref,
         v_ref, ga_ref, u_ref, gb_ref, sa_ref, sb_ref) = refs

    x = x_ref[...]
    xn = x * lax.rsqrt(jnp.mean(x * x, axis=-1, keepdims=True) + EPS)
    h = (xn * ng_ref[...]) * (1.0 + sc_ref[...]) + sh_ref[...]
    hb = h.astype(BF16)

    r = _mm(hb, wr_ref[...])
    qk = _mm(hb, wa_ref[:, _OFF_QK:_OFF_QK + 2 * KDIM])
    xg = _mm(r.astype(BF16), wab_ref[...]) + bab_ref[...]
    _project(hb, wa_ref, _OFF_V, VDIM, lambda z: z, v_ref)
    g_all = (jnp.minimum(xg, 0.0) - jnp.log1p(jnp.exp(-jnp.abs(xg)))) * (1.0 / GATE_NORM)

    q = qk[:, :KDIM] * (HEAD_DK ** -0.5)
    k = qk[:, KDIM:]
    if use_rope:
        cos = cos_ref[...]
        sin = sin_ref[...]
        lane = lax.broadcasted_iota(jnp.int32, (TOKEN_TILE, HEAD_DK), 1)
        first_half = (lane // (HEAD_DK // 4)) % 2 == 0

        def rope(t):
            outs = []
            for hh in range(N_HEADS):
                th = t[:, hh * HEAD_DK:(hh + 1) * HEAD_DK]
                partner = jnp.where(first_half,
                                    pltpu.roll(th, HEAD_DK - HEAD_DK // 4, axis=1),
                                    pltpu.roll(th, HEAD_DK // 4, axis=1))
                outs.append(th * cos + partner * sin)
            return jnp.concatenate(outs, axis=1)

        q = rope(q)
        k = rope(k)

    _project(hb, wb_ref, _OFF_U, FOURIER_DIM, lambda z: z, u_ref)

    row = lax.broadcasted_iota(jnp.int32, (CHUNK, 2 * CHUNK), 0)
    col = lax.broadcasted_iota(jnp.int32, (CHUNK, 2 * CHUNK), 1) % CHUNK
    bcs = []
    for direction in range(2):
        g_hi, g_lo = _split_bf16(g_all[:, direction * KDIM:(direction + 1) * KDIM])
        tri = jnp.where((col <= row) if direction == 0 else (col >= row), 1.0, 0.0).astype(BF16)
        bc_chunks = []
        for c in range(CHUNKS_PER_TILE):
            cr = slice(c * CHUNK, (c + 1) * CHUNK)
            bc_chunks.append(_mm(tri, jnp.concatenate([g_hi[cr], g_lo[cr]], axis=0)))
        bcs.append(jnp.concatenate(bc_chunks, axis=0))

    _project(hb, wa_ref, _OFF_GA, VDIM, jax.nn.silu, ga_ref)

    n_parts = 4 * CHUNKS_PER_TILE
    sel_r = lax.broadcasted_iota(jnp.int32, (n_parts, LANES), 0)
    sel_c = lax.broadcasted_iota(jnp.int32, (n_parts, LANES), 1)
    part_sel = jnp.where((sel_r % CHUNKS_PER_TILE == sel_c) & (sel_r < 3 * CHUNKS_PER_TILE), 1.0, 0.0).astype(BF16)
    for direction, (qd_ref, ki_ref, ke_ref, dt_ref) in enumerate(
            ((qdf_ref, kif_ref, kef_ref, dtf_ref), (qdb_ref, kib_ref, keb_ref, dtb_ref))):
        bc = bcs[direction]
        edge = CHUNK - 1 if direction == 0 else 0
        bl_rows = bc.reshape(CHUNKS_PER_TILE, CHUNK, KDIM)[:, edge, :]
        bl = jnp.broadcast_to(bl_rows[:, None, :], (CHUNKS_PER_TILE, CHUNK, KDIM)).reshape(TOKEN_TILE, KDIM)
        qd_ref[...] = (q * jnp.exp(bc)).astype(BF16)
        ki_ref[...] = (k * jnp.exp(-bc)).astype(BF16)
        ke_ref[...] = (k * jnp.exp(bl - bc)).astype(BF16)
        p0 = bl_rows.astype(BF16).astype(F32)
        p1 = (bl_rows - p0).astype(BF16).astype(F32)
        p2 = (bl_rows - p0 - p1).astype(BF16).astype(F32)
        parts = jnp.concatenate([p0, p1, p2, jnp.zeros_like(p0)], axis=0).astype(BF16)
        dt_ref[...] = jnp.exp(_mm_ta(parts, part_sel))

    _project(hb, wb_ref, _OFF_GB, FOURIER_DIM, jax.nn.silu, gb_ref)
    _project(hb, wb_ref, _OFF_MA, D_MODEL, jax.nn.sigmoid, sa_ref)
    _project(hb, wb_ref, _OFF_MB, D_MODEL, jax.nn.sigmoid, sb_ref)


def _in_proj(x, scale, shift, norm_g, w_head, w_tail, w_r, w_ab, b_ab, rope):
    B, T, _ = x.shape
    use_rope = rope is not None
    per_batch_mod = scale.shape[0] > 1
    step_rows = IN_TILES_PER_STEP * TOKEN_TILE
    if T < step_rows:
        assert not per_batch_mod and not use_rope and (B * T) % step_rows == 0
        outs = _in_proj(x.reshape(B * T // step_rows, step_rows, D_MODEL), scale, shift, norm_g,
                        w_head, w_tail, w_r, w_ab, b_ab, rope)
        return [o.reshape((B, T // TOKEN_TILE) + o.shape[2:]) if o.ndim == 4 else o.reshape(B, T, o.shape[-1])
                for o in outs]
    nt = T // TOKEN_TILE

    def tok_spec(cols):
        return pl.BlockSpec((None, step_rows, cols), lambda b, t: (b, t, 0))

    mod_spec = pl.BlockSpec((None, 1, D_MODEL), (lambda b, t: (b, 0, 0)) if per_batch_mod else (lambda b, t: (0, 0, 0)))
    in_specs = [
        tok_spec(D_MODEL), mod_spec, mod_spec, _const_spec((1, D_MODEL)),
        _const_spec((D_MODEL, _W_HEAD_COLS)), _const_spec((D_MODEL, _W_TAIL_COLS)), _const_spec((D_MODEL, LANES)),
        _const_spec((LANES, 2 * KDIM)), _const_spec((1, 2 * KDIM)),
    ]
    args = [x, scale, shift, norm_g, w_head, w_tail, w_r, w_ab, b_ab]
    if use_rope:
        in_specs += [pl.BlockSpec((step_rows, HEAD_DK), lambda b, t: (t, 0))] * 2
        args += list(rope)

    dt_spec = pl.BlockSpec((None, IN_TILES_PER_STEP, KDIM, LANES), lambda b, t: (b, t, 0, 0))
    tok_bf = lambda cols: jax.ShapeDtypeStruct((B, T, cols), BF16)
    dt_shape = jax.ShapeDtypeStruct((B, nt, KDIM, LANES), F32)
    out_specs = [tok_spec(KDIM)] * 6 + [dt_spec] * 2 + [
        tok_spec(VDIM), tok_spec(VDIM), tok_spec(FOURIER_DIM),
        tok_spec(FOURIER_DIM), tok_spec(D_MODEL), tok_spec(D_MODEL)]
    out_shape = [tok_bf(KDIM)] * 6 + [dt_shape] * 2 + [
        tok_bf(VDIM), tok_bf(VDIM), tok_bf(FOURIER_DIM),
        tok_bf(FOURIER_DIM), tok_bf(D_MODEL), tok_bf(D_MODEL)]
    return pl.pallas_call(
        functools.partial(_in_proj_kernel, use_rope),
        grid=(B, T // step_rows),
        in_specs=in_specs,
        out_specs=out_specs,
        out_shape=out_shape,
        compiler_params=pltpu.CompilerParams(
            dimension_semantics=("arbitrary", "arbitrary"), vmem_limit_bytes=VMEM_LIMIT_BYTES),
        name="in_proj_rope" if use_rope else "in_proj",
    )(*args)


def _gla_kernel(heads, n_tiles, has_init, seqs, *refs):
    gain_pos = 10 + (2 if has_init else 0)
    for bb in range(seqs):
        _gla_seq(heads, n_tiles, has_init, *[r if i == gain_pos else r.at[bb] for i, r in enumerate(refs)])


def _gla_seq(heads, n_tiles, has_init, *refs):
    (qdf_ref, kif_ref, kef_ref, qdb_ref, kib_ref, keb_ref, dtf_ref, dtb_ref, v_ref, ga_ref) = refs[:10]
    refs = refs[10:]
    if has_init:
        s0f_ref, s0b_ref = refs[:2]
        refs = refs[2:]
    gn_ref, og_ref, sf_ref, sb_ref, state, d_state, of_acc, ob_acc = refs
    if has_init:
        state[0] = s0f_ref[...]
        state[1] = s0b_ref[...]
    else:
        state[...] = jnp.zeros_like(state)
    dir_refs = ((qdf_ref, kif_ref, kef_ref, dtf_ref, of_acc), (qdb_ref, kib_ref, keb_ref, dtb_ref, ob_acc))

    def tile_body(finalize, j, carry):
        row = lax.broadcasted_iota(jnp.int32, (TOKEN_TILE, TOKEN_TILE), 0)
        col = lax.broadcasted_iota(jnp.int32, (TOKEN_TILE, TOKEN_TILE), 1)
        same_chunk = (row // CHUNK) == (col // CHUNK)
        causal = (same_chunk & (col <= row), same_chunk & (col >= row))
        tiles = (j, n_tiles - 1 - j)
        bases = tuple(pl.multiple_of(t * TOKEN_TILE, TOKEN_TILE) for t in tiles)

        for direction in range(2):
            qd_ref, ki_ref, ke_ref, _, o_acc = dir_refs[direction]
            trows = pl.ds(bases[direction], TOKEN_TILE)
            for hh in range(heads):
                kc = slice(hh * HEAD_DK, (hh + 1) * HEAD_DK)
                vc = slice(hh * HEAD_DV, (hh + 1) * HEAD_DV)
                scores = jnp.where(causal[direction], _mm_tb(qd_ref[trows, kc], ki_ref[trows, kc]), 0.0)
                o_acc[trows, vc] = _mm(scores.astype(BF16), v_ref[trows, vc])
                for c in range(CHUNKS_PER_TILE):
                    rows = pl.ds(bases[direction] + c * CHUNK, CHUNK)
                    d_state[direction, hh, c] = _mm_ta(ke_ref[rows, kc], v_ref[rows, vc])

        steps = [(step, direction) for step in range(CHUNKS_PER_TILE) for direction in range(2)]
        if finalize[0] != finalize[1]:
            steps.sort(key=lambda sd: finalize[sd[1]])
        for step, direction in steps:
            qd_ref, _, _, dt_ref, o_acc = dir_refs[direction]
            other_acc = dir_refs[1 - direction][4]
            c = step if direction == 0 else CHUNKS_PER_TILE - 1 - step
            rows = pl.ds(bases[direction] + c * CHUNK, CHUNK)
            for hh in range(heads):
                kc = slice(hh * HEAD_DK, (hh + 1) * HEAD_DK)
                vc = slice(hh * HEAD_DV, (hh + 1) * HEAD_DV)
                s = state[direction, hh]
                o = o_acc[rows, vc] + _mm(qd_ref[rows, kc], s.astype(BF16))
                if finalize[direction]:
                    o = o + other_acc[rows, vc]
                    on = o * lax.rsqrt(jnp.mean(o * o, axis=-1, keepdims=True) + EPS) * gn_ref[...]
                    og_ref[rows, vc] = (on * ga_ref[rows, vc].astype(F32)).astype(BF16)
                else:
                    o_acc[rows, vc] = o
                dec = dt_ref[tiles[direction], kc, c:c + 1]
                state[direction, hh] = dec * s + d_state[direction, hh, c]
        return carry

    first_half = n_tiles // 2
    unroll = 4 if first_half % 4 == 0 else 1
    lax.fori_loop(0, first_half, functools.partial(tile_body, (False, False)), 0, unroll=unroll)
    if n_tiles % 2:
        tile_body((False, True), first_half, 0)
    lax.fori_loop(n_tiles - first_half, n_tiles, functools.partial(tile_body, (True, True)), 0, unroll=unroll)
    sf_ref[...] = state[0]
    sb_ref[...] = state[1]


def _gla(qdf, kif, kef, qdb, kib, keb, dtf, dtb, v, ga, init_states, gla_norm_g, heads):
    B, T, _ = v.shape
    nt = T // TOKEN_TILE
    hg = N_HEADS // heads
    has_init = init_states is not None
    seqs = GLA_SHORT_SEQS_PER_STEP if nt == 1 else 1
    assert B % seqs == 0
    k_spec = pl.BlockSpec((seqs, T, heads * HEAD_DK), lambda b, h: (b, 0, h))
    v_spec = pl.BlockSpec((seqs, T, heads * HEAD_DV), lambda b, h: (b, 0, h))
    dt_spec = pl.BlockSpec((seqs, nt, heads * HEAD_DK, LANES), lambda b, h: (b, 0, h, 0))
    s_spec = pl.BlockSpec((seqs, heads, HEAD_DK, HEAD_DV), lambda b, h: (b, h, 0, 0))
    s_shape = jax.ShapeDtypeStruct((B, N_HEADS, HEAD_DK, HEAD_DV), F32)
    return pl.pallas_call(
        functools.partial(_gla_kernel, heads, nt, has_init, seqs),
        grid=(B // seqs, hg),
        in_specs=([k_spec] * 6 + [dt_spec] * 2 + [v_spec, v_spec] + [s_spec] * (2 if has_init else 0)
                  + [_const_spec((1, HEAD_DV))]),
        out_specs=[v_spec, s_spec, s_spec],
        out_shape=[jax.ShapeDtypeStruct((B, T, VDIM), BF16), s_shape, s_shape],
        scratch_shapes=[pltpu.VMEM((seqs, 2, heads, HEAD_DK, HEAD_DV), F32),
                        pltpu.VMEM((seqs, 2, heads, CHUNKS_PER_TILE, HEAD_DK, HEAD_DV), F32),
                        pltpu.VMEM((seqs, T, heads * HEAD_DV), F32),
                        pltpu.VMEM((seqs, T, heads * HEAD_DV), F32)],
        compiler_params=pltpu.CompilerParams(
            dimension_semantics=("arbitrary", "arbitrary"), vmem_limit_bytes=VMEM_LIMIT_BYTES),
        name="gla",
    )(qdf, kif, kef, qdb, kib, keb, dtf, dtb, v, ga, *(init_states or ()), gla_norm_g)


def _mix_weights_kernel(tab_ref, wf_ref, o_ref):
    for g in range(N_FGROUPS):
        o_ref[g] = jnp.dot(tab_ref[...], wf_ref[g], preferred_element_type=F32,
                           precision=lax.Precision.HIGHEST).astype(BF16)


def _mix_weights(wf):
    n = np.arange(FGROUP_CH)
    ang = 2.0 * np.pi * ((n[:, None] * n[None, :]) % FGROUP_CH) / FGROUP_CH
    tab = jnp.asarray(np.concatenate([np.cos(ang), np.sin(ang)], axis=0) / np.sqrt(FGROUP_CH), F32)
    return pl.pallas_call(
        _mix_weights_kernel,
        out_shape=jax.ShapeDtypeStruct((N_FGROUPS, 2 * FGROUP_CH, FGROUP_CH), BF16),
        name="mix_weights",
    )(tab, wf)


def _mix_and_gate(fp, fq, mix_ref, gb, store):
    fp = fp.astype(BF16)
    fq = fq.astype(BF16)
    for g in range(N_FGROUPS):
        cols = slice(g * FGROUP_CH, (g + 1) * FGROUP_CH)
        z = _mm(jnp.concatenate([fp[:, cols], fq[:, cols]], axis=1), mix_ref[g])
        store(cols, (z * gb(cols).astype(F32)).astype(BF16))


def _fourier_kernel(seqs, ct_ref, st_ref, u_ref, gb_ref, mix_ref, o_ref):
    for bb in range(seqs):
        u = u_ref[bb]

        def store(cols, val, bb=bb):
            o_ref[bb, :, cols] = val

        _mix_and_gate(_mm(ct_ref[...], u), _mm(st_ref[...], u), mix_ref,
                      lambda cols, bb=bb: gb_ref[bb, :, cols], store)


def _fourier(ct, stn, u, gb, mix):
    B, T, _ = u.shape
    tf = min(T, FOURIER_ROWS_PER_STEP)
    seqs = FOURIER_SHORT_SEQS_PER_STEP if T < FOURIER_ROWS_PER_STEP else 1
    assert B % seqs == 0
    tab_spec = pl.BlockSpec((tf, T), lambda b, t: (t, 0))
    seq_spec = pl.BlockSpec((seqs, T, FOURIER_DIM), lambda b, t: (b, 0, 0))
    tile_spec = pl.BlockSpec((seqs, tf, FOURIER_DIM), lambda b, t: (b, t, 0))
    return pl.pallas_call(
        functools.partial(_fourier_kernel, seqs),
        grid=(B // seqs, T // tf),
        in_specs=[tab_spec, tab_spec, seq_spec, tile_spec,
                  _const_spec((N_FGROUPS, 2 * FGROUP_CH, FGROUP_CH))],
        out_specs=tile_spec,
        out_shape=jax.ShapeDtypeStruct((B, T, FOURIER_DIM), BF16),
        compiler_params=pltpu.CompilerParams(
            dimension_semantics=("arbitrary", "arbitrary"), vmem_limit_bytes=VMEM_LIMIT_BYTES),
        name="fourier",
    )(ct, stn, u, gb, mix)


def _fourier_split_kernel(ce_ref, se_ref, co_ref, so_ref, u_ref, gb_ref, mix_ref, o_ref, wide, u_eo):
    half = u_eo.shape[1]
    for g in range(N_FGROUPS):
        cols = slice(g * FGROUP_CH, (g + 1) * FGROUP_CH)
        wide[g] = u_ref[:, cols].astype(F32)
        for parity in range(2):
            u_eo[parity, :, cols] = wide[g, pl.ds(parity, half, stride=2), :].astype(BF16)

    for kb in range(half // FOURIER_ROWS_PER_STEP):
        rows = slice(kb * FOURIER_ROWS_PER_STEP, (kb + 1) * FOURIER_ROWS_PER_STEP)
        ep, eq = _mm(ce_ref[rows, :], u_eo[0]), _mm(se_ref[rows, :], u_eo[0])
        op, oq = _mm(co_ref[rows, :], u_eo[1]), _mm(so_ref[rows, :], u_eo[1])
        for upper, (fp, fq) in enumerate(((ep + op, eq + oq), (ep - op, eq - oq))):
            def store(cols, val, upper=upper):
                o_ref[upper, rows, cols] = val

            _mix_and_gate(fp, fq, mix_ref, lambda cols, upper=upper: gb_ref[upper, rows, cols], store)


def _fourier_split(u, gb, mix):
    B, T, _ = u.shape
    half = T // 2
    ce, sen = _time_dft_tables(T, half, 2 * np.arange(half))
    co, son = _time_dft_tables(T, half, 2 * np.arange(half) + 1)
    seq_spec = pl.BlockSpec((None, T, FOURIER_DIM), lambda b: (b, 0, 0))
    halves_spec = pl.BlockSpec((None, 2, half, FOURIER_DIM), lambda b: (b, 0, 0, 0))
    out = pl.pallas_call(
        _fourier_split_kernel,
        grid=(B,),
        in_specs=[_const_spec((half, half))] * 4 + [seq_spec, halves_spec,
                                                     _const_spec((N_FGROUPS, 2 * FGROUP_CH, FGROUP_CH))],
        out_specs=halves_spec,
        out_shape=jax.ShapeDtypeStruct((B, 2, half, FOURIER_DIM), BF16),
        scratch_shapes=[pltpu.VMEM((N_FGROUPS, T, FGROUP_CH), F32),
                        pltpu.VMEM((2, half, FOURIER_DIM), BF16)],
        compiler_params=pltpu.CompilerParams(
            dimension_semantics=("arbitrary",), vmem_limit_bytes=VMEM_LIMIT_BYTES),
        name="fourier_split",
    )(ce, sen, co, son, u, gb.reshape(B, 2, half, FOURIER_DIM), mix)
    return out.reshape(B, T, FOURIER_DIM)


def _out_proj_kernel(og_ref, fg_ref, sa_ref, sb_ref, x_ref, gate_ref, fng_ref, wpa_ref, wpb_ref, wo_ref, y_ref):
    for i in range(OUT_TILES_PER_STEP):
        rows = slice(i * TOKEN_TILE, (i + 1) * TOKEN_TILE)
        ya = _mm(og_ref[rows, :], wpa_ref[...])
        yb = _mm(fg_ref[rows, :], wpb_ref[...])
        merged = sa_ref[rows, :].astype(F32) * ya + sb_ref[rows, :].astype(F32) * yb
        xo = x_ref[rows, :] + gate_ref[...] * _mm(merged.astype(BF16), wo_ref[...])
        y_ref[rows, :] = xo * lax.rsqrt(jnp.mean(xo * xo, axis=-1, keepdims=True) + EPS) * fng_ref[...]


def _out_proj(og, fg, sa, sb, x, gate, final_norm_g, wpa, wpb, wo):
    B, T, _ = x.shape
    per_batch_mod = gate.shape[0] > 1
    step_rows = OUT_TILES_PER_STEP * TOKEN_TILE
    if T < step_rows:
        assert not per_batch_mod and (B * T) % step_rows == 0
        fold = lambda a: a.reshape(B * T // step_rows, step_rows, a.shape[-1])
        y = _out_proj(fold(og), fold(fg), fold(sa), fold(sb), fold(x), gate, final_norm_g, wpa, wpb, wo)
        return y.reshape(B, T, D_MODEL)

    def tok_spec(cols):
        return pl.BlockSpec((None, step_rows, cols), lambda b, t: (b, t, 0))

    mod_spec = pl.BlockSpec((None, 1, D_MODEL), (lambda b, t: (b, 0, 0)) if per_batch_mod else (lambda b, t: (0, 0, 0)))
    return pl.pallas_call(
        _out_proj_kernel,
        grid=(B, T // step_rows),
        in_specs=[tok_spec(VDIM), tok_spec(FOURIER_DIM), tok_spec(D_MODEL), tok_spec(D_MODEL), tok_spec(D_MODEL),
                  mod_spec, _const_spec((1, D_MODEL)),
                  _const_spec((VDIM, D_MODEL)), _const_spec((FOURIER_DIM, D_MODEL)), _const_spec((D_MODEL, D_MODEL))],
        out_specs=tok_spec(D_MODEL),
        out_shape=jax.ShapeDtypeStruct((B, T, D_MODEL), F32),
        compiler_params=pltpu.CompilerParams(
            dimension_semantics=("arbitrary", "arbitrary"), vmem_limit_bytes=VMEM_LIMIT_BYTES),
        name="out_proj",
    )(og, fg, sa, sb, x, gate, final_norm_g, wpa, wpb, wo)


def _time_dft_tables(T, n_rows, positions):
    lo = 32
    hi = n_rows // lo
    n = np.asarray(positions)
    ang_hi = 2.0 * np.pi * (((np.arange(hi)[:, None] * lo) * n[None, :]) % T) / T
    ang_lo = 2.0 * np.pi * ((np.arange(lo)[:, None] * n[None, :]) % T) / T
    scale = 1.0 / np.sqrt(T)
    ch = jnp.asarray(np.cos(ang_hi) * scale, F32)[:, None, :]
    sh = jnp.asarray(np.sin(ang_hi) * scale, F32)[:, None, :]
    cl = jnp.asarray(np.cos(ang_lo), F32)[None, :, :]
    sl = jnp.asarray(np.sin(ang_lo), F32)[None, :, :]
    ct = (ch * cl - sh * sl).astype(BF16).reshape(n_rows, n.size)
    stn = (-(sh * cl + ch * sl)).astype(BF16).reshape(n_rows, n.size)
    return ct, stn


def _rope_tables(T):
    rows = T // GRID_W
    r = np.repeat(np.arange(rows), GRID_W).astype(np.float64)
    c = np.tile(np.arange(GRID_W), rows).astype(np.float64)
    n_freq = HEAD_DK // 4
    freqs = ROPE_BASE ** (-np.arange(n_freq, dtype=np.float64) / n_freq)
    ang_r = r[:, None] * freqs
    ang_c = c[:, None] * freqs
    cos = np.concatenate([np.cos(ang_r), np.cos(ang_r), np.cos(ang_c), np.cos(ang_c)], axis=-1)
    sin = np.concatenate([-np.sin(ang_r), np.sin(ang_r), -np.sin(ang_c), np.sin(ang_c)], axis=-1)
    return jnp.asarray(cos, F32), jnp.asarray(sin, F32)


def _path(x, scale, shift, gate, init_states, rope, heads, wts):
    T = x.shape[1]
    (qdf, kif, kef, qdb, kib, keb, dtf, dtb, v, ga, u, gb, sa, sb) = _in_proj(
        x, scale, shift, wts["norm_g"], wts["w_head"], wts["w_tail"], wts["w_r"], wts["w_ab"], wts["b_ab"], rope)
    og, sf, sbw = _gla(qdf, kif, kef, qdb, kib, keb, dtf, dtb, v, ga, init_states, wts["gla_norm_g"], heads)
    if T >= 4 * FOURIER_ROWS_PER_STEP:
        fg = _fourier_split(u, gb, wts["four_mix"])
    else:
        ct, stn = _time_dft_tables(T, T, np.arange(T))
        fg = _fourier(ct, stn, u, gb, wts["four_mix"])
    y = _out_proj(og, fg, sa, sb, x, gate, wts["final_norm_g"], wts["w_proj_a"], wts["w_proj_b"], wts["w_out"])
    return y, sf, sbw


def kernel(x_prompt, x_sample, state_gla_fwd, state_gla_bwd, c, c_ctx, w_ada, b_ada, norm_g, w_in,
           w_alpha_fwd, b_alpha_fwd, w_alpha_bwd, b_alpha_bwd, gla_norm_g, w_four, w_proj_a, w_proj_b,
           w_out, final_norm_g):
    depth = w_in.shape[0]
    assert depth == 1, "single trunk layer"
    bs = x_sample.shape[0]

    n_cond = bs + 1
    cond_rows = -(-n_cond // 8) * 8
    cond = jnp.concatenate([c, c_ctx[None, :], jnp.zeros((cond_rows - n_cond, D_MODEL), F32)], axis=0)
    ada = _ada(cond, w_ada[0], b_ada[0][None, :])
    shift, scale, gate = ada[:, :D_MODEL], ada[:, D_MODEL:2 * D_MODEL], ada[:, 2 * D_MODEL:]
    mod = lambda m, lo, hi: m[lo:hi][:, None, :]

    wi = w_in[0]
    o_r = _W_HEAD_COLS
    o_u = o_r + 2 * GATE_RANK
    w_r = jnp.pad(wi[:, o_r:o_u], ((0, 0), (0, LANES - 2 * GATE_RANK))).astype(BF16)
    w_ab = jnp.zeros((LANES, 2 * KDIM), F32)
    w_ab = w_ab.at[:GATE_RANK, :KDIM].set(w_alpha_fwd[0]).at[GATE_RANK:2 * GATE_RANK, KDIM:].set(w_alpha_bwd[0])
    wts = dict(
        norm_g=norm_g[0][None, :], w_head=wi[:, :o_r].astype(BF16), w_tail=wi[:, o_u:].astype(BF16),
        w_r=w_r, w_ab=w_ab.astype(BF16),
        b_ab=jnp.concatenate([b_alpha_fwd[0], b_alpha_bwd[0]])[None, :],
        gla_norm_g=gla_norm_g[0][None, :], four_mix=_mix_weights(w_four[0]),
        w_proj_a=w_proj_a[0].astype(BF16), w_proj_b=w_proj_b[0].astype(BF16), w_out=w_out[0].astype(BF16),
        final_norm_g=final_norm_g[None, :])

    y_prompt, sf, sb = _path(x_prompt, mod(scale, bs, bs + 1), mod(shift, bs, bs + 1), mod(gate, bs, bs + 1),
                             None, None, N_HEADS, wts)
    y_sample, _, _ = _path(x_sample, mod(scale, 0, bs), mod(shift, 0, bs), mod(gate, 0, bs),
                           (state_gla_fwd[:, 0], state_gla_bwd[:, 0]), _rope_tables(x_sample.shape[1]), 2, wts)
    return (y_prompt, y_sample, sf[:, None].astype(x_prompt.dtype), sb[:, None].astype(x_prompt.dtype))
```

```python
import functools

import numpy as np
import jax
import jax.numpy as jnp
from jax import lax
from jax.experimental import pallas as pl
from jax.experimental.pallas import tpu as pltpu

F32 = jnp.float32
BF16 = jnp.bfloat16

D_MODEL = 1024
N_HEADS = 4
HEAD_DK = 128
HEAD_DV = 256
KDIM = N_HEADS * HEAD_DK
VDIM = N_HEADS * HEAD_DV
GATE_RANK = 16
GATE_NORM = 16.0
CHUNK = 64
N_FGROUPS = 4
FGROUP_CH = 128
FOURIER_DIM = N_FGROUPS * FGROUP_CH
GRID_W = 64
ROPE_BASE = 10000.0
EPS = 1e-6

LANES = 128
MXU_COLS = 256
TOKEN_TILE = 256
CHUNKS_PER_TILE = TOKEN_TILE // CHUNK
IN_TILES_PER_STEP = 2
OUT_TILES_PER_STEP = 4
GLA_SHORT_SEQS_PER_STEP = 4
FOURIER_ROWS_PER_STEP = 512
FOURIER_SHORT_SEQS_PER_STEP = 8
VMEM_LIMIT_BYTES = 56 * 1024 * 1024

_OFF_QK = 0
_OFF_V = _OFF_QK + 2 * KDIM
_OFF_GA = _OFF_V + VDIM
_W_HEAD_COLS = _OFF_GA + VDIM
_OFF_U = 0
_OFF_GB = _OFF_U + FOURIER_DIM
_OFF_MA = _OFF_GB + FOURIER_DIM
_OFF_MB = _OFF_MA + D_MODEL
_W_TAIL_COLS = _OFF_MB + D_MODEL


def _mm(a, b):
    return jnp.dot(a, b, preferred_element_type=F32)


def _mm_ta(a, b):
    return lax.dot_general(a, b, (((0,), (0,)), ((), ())), preferred_element_type=F32)


def _mm_tb(a, b):
    return lax.dot_general(a, b, (((1,), (1,)), ((), ())), preferred_element_type=F32)


def _split_bf16(x):
    hi = x.astype(BF16)
    lo = (x - hi.astype(F32)).astype(BF16)
    return hi, lo


def _const_spec(shape):
    nd = len(shape)
    return pl.BlockSpec(shape, lambda *_: (0,) * nd)


def _ada_kernel(c_ref, w_ref, b_ref, o_ref):
    c = c_ref[...]
    s = c * jax.nn.sigmoid(c)
    o_ref[...] = _mm(s.astype(BF16), w_ref[...].astype(BF16)) + b_ref[...]


def _ada(cond, w, b):
    rows = cond.shape[0]
    return pl.pallas_call(
        _ada_kernel,
        grid=(3,),
        in_specs=[pl.BlockSpec((rows, D_MODEL), lambda n: (0, 0)),
                  pl.BlockSpec((D_MODEL, D_MODEL), lambda n: (0, n)),
                  pl.BlockSpec((1, D_MODEL), lambda n: (0, n))],
        out_specs=pl.BlockSpec((rows, D_MODEL), lambda n: (0, n)),
        out_shape=jax.ShapeDtypeStruct((rows, 3 * D_MODEL), F32),
        compiler_params=pltpu.CompilerParams(
            dimension_semantics=("arbitrary",), vmem_limit_bytes=VMEM_LIMIT_BYTES),
        name="ada",
    )(cond, w, b)


def _in_proj_kernel(use_rope, *refs):
    n_shared = 8
    n_in = 1 + n_shared + (2 if use_rope else 0)
    dt_slots = (n_in + 6, n_in + 7)
    for i in range(IN_TILES_PER_STEP):
        rows = pl.ds(i * TOKEN_TILE, TOKEN_TILE)
        tile_refs = []
        for idx, ref in enumerate(refs):
            if 1 <= idx <= n_shared:
                tile_refs.append(ref)
            elif idx in dt_slots:
                tile_refs.append(ref.at[i])
            else:
                tile_refs.append(ref.at[rows])
        _in_proj_tile(use_rope, *tile_refs)


def _project(hb, w_ref, off, width, act, out_ref):
    for n in range(0, width, MXU_COLS):
        z = _mm(hb, w_ref[:, off + n:off + n + MXU_COLS])
        out_ref[:, n:n + MXU_COLS] = act(z).astype(BF16)


def _in_proj_tile(use_rope, *refs):
    if use_rope:
        (x_ref, sc_ref, sh_ref, ng_ref, wa_ref, wb_ref, wr_ref, wab_ref, bab_ref, cos_ref, sin_ref,
         qdf_ref, kif_ref, kef_ref, qdb_ref, kib_ref, keb_ref, dtf_ref, dtb_ref,
         v_ref, ga_ref, u_ref, gb_ref, sa_ref, sb_ref) = refs
    else:
        (x_ref, sc_ref, sh_ref, ng_ref, wa_ref, wb_ref, wr_ref, wab_ref, bab_ref,
         qdf_ref, kif_ref, kef_ref, qdb_ref, kib_ref, keb_ref, dtf_ref, dtb_ref,
         v_ref, ga_ref, u_ref, gb_ref, sa_ref, sb_ref) = refs

    x = x_ref[...]
    xn = x * lax.rsqrt(jnp.mean(x * x, axis=-1, keepdims=True) + EPS)
    h = (xn * ng_ref[...]) * (1.0 + sc_ref[...]) + sh_ref[...]
    hb = h.astype(BF16)

    r_t = _mm_tb(wr_ref[...], hb)
    qk = _mm(hb, wa_ref[:, _OFF_QK:_OFF_QK + 2 * KDIM])
    xg = _mm_ta(r_t.astype(BF16), wab_ref[...]) + bab_ref[...]
    _project(hb, wa_ref, _OFF_V, VDIM, lambda z: z, v_ref)
    g_all = (jnp.minimum(xg, 0.0) - jnp.log1p(jnp.exp(-jnp.abs(xg)))) * (1.0 / GATE_NORM)

    q = qk[:, :KDIM] * (HEAD_DK ** -0.5)
    k = qk[:, KDIM:]
    if use_rope:
        cos = cos_ref[...]
        sin = sin_ref[...]
        lane = lax.broadcasted_iota(jnp.int32, (TOKEN_TILE, HEAD_DK), 1)
        first_half = (lane // (HEAD_DK // 4)) % 2 == 0

        def rope(t):
            outs = []
            for hh in range(N_HEADS):
                th = t[:, hh * HEAD_DK:(hh + 1) * HEAD_DK]
                partner = jnp.where(first_half,
                                    pltpu.roll(th, HEAD_DK - HEAD_DK // 4, axis=1),
                                    pltpu.roll(th, HEAD_DK // 4, axis=1))
                outs.append(th * cos + partner * sin)
            return jnp.concatenate(outs, axis=1)

        q = rope(q)
        k = rope(k)

    _project(hb, wb_ref, _OFF_U, FOURIER_DIM, lambda z: z, u_ref)

    row = lax.broadcasted_iota(jnp.int32, (CHUNK, 2 * CHUNK), 0)
    col = lax.broadcasted_iota(jnp.int32, (CHUNK, 2 * CHUNK), 1) % CHUNK
    bcs = []
    for direction in range(2):
        g_hi, g_lo = _split_bf16(g_all[:, direction * KDIM:(direction + 1) * KDIM])
        tri = jnp.where((col <= row) if direction == 0 else (col >= row), 1.0, 0.0).astype(BF16)
        bc_chunks = []
        for c in range(CHUNKS_PER_TILE):
            cr = slice(c * CHUNK, (c + 1) * CHUNK)
            bc_chunks.append(_mm(tri, jnp.concatenate([g_hi[cr], g_lo[cr]], axis=0)))
        bcs.append(jnp.concatenate(bc_chunks, axis=0))

    _project(hb, wa_ref, _OFF_GA, VDIM, jax.nn.silu, ga_ref)

    n_parts = 4 * CHUNKS_PER_TILE
    sel_r = lax.broadcasted_iota(jnp.int32, (n_parts, LANES), 0)
    sel_c = lax.broadcasted_iota(jnp.int32, (n_parts, LANES), 1)
    part_sel = jnp.where((sel_r % CHUNKS_PER_TILE == sel_c) & (sel_r < 3 * CHUNKS_PER_TILE), 1.0, 0.0).astype(BF16)
    for direction, (qd_ref, ki_ref, ke_ref, dt_ref) in enumerate(
            ((qdf_ref, kif_ref, kef_ref, dtf_ref), (qdb_ref, kib_ref, keb_ref, dtb_ref))):
        bc = bcs[direction]
        edge = CHUNK - 1 if direction == 0 else 0
        bl_rows = bc.reshape(CHUNKS_PER_TILE, CHUNK, KDIM)[:, edge, :]
        bl = jnp.broadcast_to(bl_rows[:, None, :], (CHUNKS_PER_TILE, CHUNK, KDIM)).reshape(TOKEN_TILE, KDIM)
        qd_ref[...] = (q * jnp.exp(bc)).astype(BF16)
        ki_ref[...] = (k * jnp.exp(-bc)).astype(BF16)
        ke_ref[...] = (k * jnp.exp(bl - bc)).astype(BF16)
        p0 = bl_rows.astype(BF16).astype(F32)
        p1 = (bl_rows - p0).astype(BF16).astype(F32)
        p2 = (bl_rows - p0 - p1).astype(BF16).astype(F32)
        parts = jnp.concatenate([p0, p1, p2, jnp.zeros_like(p0)], axis=0).astype(BF16)
        dt_ref[...] = jnp.exp(_mm_ta(parts, part_sel))

    _project(hb, wb_ref, _OFF_GB, FOURIER_DIM, jax.nn.silu, gb_ref)
    _project(hb, wb_ref, _OFF_MA, D_MODEL, jax.nn.sigmoid, sa_ref)
    _project(hb, wb_ref, _OFF_MB, D_MODEL, jax.nn.sigmoid, sb_ref)


def _in_proj(x, scale, shift, norm_g, w_head, w_tail, w_r, w_ab, b_ab, rope):
    B, T, _ = x.shape
    use_rope = rope is not None
    per_batch_mod = scale.shape[0] > 1
    step_rows = IN_TILES_PER_STEP * TOKEN_TILE
    if T < step_rows:
        assert not per_batch_mod and not use_rope and (B * T) % step_rows == 0
        outs = _in_proj(x.reshape(B * T // step_rows, step_rows, D_MODEL), scale, shift, norm_g,
                        w_head, w_tail, w_r, w_ab, b_ab, rope)
        return [o.reshape((B, T // TOKEN_TILE) + o.shape[2:]) if o.ndim == 4 else o.reshape(B, T, o.shape[-1])
                for o in outs]
    nt = T // TOKEN_TILE

    def tok_spec(cols):
        return pl.BlockSpec((None, step_rows, cols), lambda b, t: (b, t, 0))

    mod_spec = pl.BlockSpec((None, 1, D_MODEL), (lambda b, t: (b, 0, 0)) if per_batch_mod else (lambda b, t: (0, 0, 0)))
    in_specs = [
        tok_spec(D_MODEL), mod_spec, mod_spec, _const_spec((1, D_MODEL)),
        _const_spec((D_MODEL, _W_HEAD_COLS)), _const_spec((D_MODEL, _W_TAIL_COLS)),
        _const_spec((2 * GATE_RANK, D_MODEL)),
        _const_spec((2 * GATE_RANK, 2 * KDIM)), _const_spec((1, 2 * KDIM)),
    ]
    args = [x, scale, shift, norm_g, w_head, w_tail, w_r, w_ab, b_ab]
    if use_rope:
        in_specs += [pl.BlockSpec((step_rows, HEAD_DK), lambda b, t: (t, 0))] * 2
        args += list(rope)

    dt_spec = pl.BlockSpec((None, IN_TILES_PER_STEP, KDIM, LANES), lambda b, t: (b, t, 0, 0))
    tok_bf = lambda cols: jax.ShapeDtypeStruct((B, T, cols), BF16)
    dt_shape = jax.ShapeDtypeStruct((B, nt, KDIM, LANES), F32)
    out_specs = [tok_spec(KDIM)] * 6 + [dt_spec] * 2 + [
        tok_spec(VDIM), tok_spec(VDIM), tok_spec(FOURIER_DIM),
        tok_spec(FOURIER_DIM), tok_spec(D_MODEL), tok_spec(D_MODEL)]
    out_shape = [tok_bf(KDIM)] * 6 + [dt_shape] * 2 + [
        tok_bf(VDIM), tok_bf(VDIM), tok_bf(FOURIER_DIM),
        tok_bf(FOURIER_DIM), tok_bf(D_MODEL), tok_bf(D_MODEL)]
    return pl.pallas_call(
        functools.partial(_in_proj_kernel, use_rope),
        grid=(B, T // step_rows),
        in_specs=in_specs,
        out_specs=out_specs,
        out_shape=out_shape,
        compiler_params=pltpu.CompilerParams(
            dimension_semantics=("arbitrary", "arbitrary"), vmem_limit_bytes=VMEM_LIMIT_BYTES),
        name="in_proj_rope" if use_rope else "in_proj",
    )(*args)


def _gla_kernel(heads, n_tiles, has_init, seqs, *refs):
    gain_pos = 10 + (2 if has_init else 0)
    for bb in range(seqs):
        _gla_seq(heads, n_tiles, has_init, *[r if i == gain_pos else r.at[bb] for i, r in enumerate(refs)])


def _gla_seq(heads, n_tiles, has_init, *refs):
    (qdf_ref, kif_ref, kef_ref, qdb_ref, kib_ref, keb_ref, dtf_ref, dtb_ref, v_ref, ga_ref) = refs[:10]
    refs = refs[10:]
    if has_init:
        s0f_ref, s0b_ref = refs[:2]
        refs = refs[2:]
    gn_ref, og_ref, sf_ref, sb_ref, state, d_state, of_acc, ob_acc = refs
    if has_init:
        state[0] = s0f_ref[...]
        state[1] = s0b_ref[...]
    else:
        state[...] = jnp.zeros_like(state)
    dir_refs = ((qdf_ref, kif_ref, kef_ref, dtf_ref, of_acc), (qdb_ref, kib_ref, keb_ref, dtb_ref, ob_acc))

    def tile_body(finalize, j, carry):
        row = lax.broadcasted_iota(jnp.int32, (TOKEN_TILE, TOKEN_TILE), 0)
        col = lax.broadcasted_iota(jnp.int32, (TOKEN_TILE, TOKEN_TILE), 1)
        same_chunk = (row // CHUNK) == (col // CHUNK)
        causal = (same_chunk & (col <= row), same_chunk & (col >= row))
        tiles = (j, n_tiles - 1 - j)
        bases = tuple(pl.multiple_of(t * TOKEN_TILE, TOKEN_TILE) for t in tiles)

        scores = {}
        for direction in range(2):
            qd_ref, ki_ref, ke_ref, _, o_acc = dir_refs[direction]
            trows = pl.ds(bases[direction], TOKEN_TILE)
            for hh in range(heads):
                kc = slice(hh * HEAD_DK, (hh + 1) * HEAD_DK)
                vc = slice(hh * HEAD_DV, (hh + 1) * HEAD_DV)
                scores[direction, hh] = jnp.where(
                    causal[direction], _mm_tb(qd_ref[trows, kc], ki_ref[trows, kc]), 0.0).astype(BF16)
                for c in range(CHUNKS_PER_TILE):
                    rows = pl.ds(bases[direction] + c * CHUNK, CHUNK)
                    d_state[direction, hh, c] = _mm_ta(ke_ref[rows, kc], v_ref[rows, vc])

        steps = [(step, direction) for step in range(CHUNKS_PER_TILE) for direction in range(2)]
        if finalize[0] != finalize[1]:
            steps.sort(key=lambda sd: finalize[sd[1]])
        for step, direction in steps:
            qd_ref, _, _, dt_ref, o_acc = dir_refs[direction]
            other_acc = dir_refs[1 - direction][4]
            c = step if direction == 0 else CHUNKS_PER_TILE - 1 - step
            rows = pl.ds(bases[direction] + c * CHUNK, CHUNK)
            for hh in range(heads):
                kc = slice(hh * HEAD_DK, (hh + 1) * HEAD_DK)
                vc = slice(hh * HEAD_DV, (hh + 1) * HEAD_DV)
                s = state[direction, hh]
                blk = (c * CHUNK) // LANES
                sc = scores[direction, hh][c * CHUNK:(c + 1) * CHUNK, blk * LANES:(blk + 1) * LANES]
                v_blk = v_ref[pl.ds(bases[direction] + blk * LANES, LANES), vc]
                o = _mm(jnp.concatenate([qd_ref[rows, kc], sc], axis=1),
                        jnp.concatenate([s.astype(BF16), v_blk], axis=0))
                if finalize[direction]:
                    o = o + other_acc[rows, vc]
                    on = o * lax.rsqrt(jnp.mean(o * o, axis=-1, keepdims=True) + EPS) * gn_ref[...]
                    og_ref[rows, vc] = (on * ga_ref[rows, vc].astype(F32)).astype(BF16)
                else:
                    o_acc[rows, vc] = o
                dec = dt_ref[tiles[direction], kc, c:c + 1]
                state[direction, hh] = dec * s + d_state[direction, hh, c]
        return carry

    first_half = n_tiles // 2
    unroll = 4 if first_half % 4 == 0 else 1
    lax.fori_loop(0, first_half, functools.partial(tile_body, (False, False)), 0, unroll=unroll)
    if n_tiles % 2:
        tile_body((False, True), first_half, 0)
    lax.fori_loop(n_tiles - first_half, n_tiles, functools.partial(tile_body, (True, True)), 0, unroll=unroll)
    sf_ref[...] = state[0]
    sb_ref[...] = state[1]


def _gla(qdf, kif, kef, qdb, kib, keb, dtf, dtb, v, ga, init_states, gla_norm_g, heads):
    B, T, _ = v.shape
    nt = T // TOKEN_TILE
    hg = N_HEADS // heads
    has_init = init_states is not None
    seqs = GLA_SHORT_SEQS_PER_STEP if nt == 1 else 1
    assert B % seqs == 0
    k_spec = pl.BlockSpec((seqs, T, heads * HEAD_DK), lambda b, h: (b, 0, h))
    v_spec = pl.BlockSpec((seqs, T, heads * HEAD_DV), lambda b, h: (b, 0, h))
    dt_spec = pl.BlockSpec((seqs, nt, heads * HEAD_DK, LANES), lambda b, h: (b, 0, h, 0))
    s_spec = pl.BlockSpec((seqs, heads, HEAD_DK, HEAD_DV), lambda b, h: (b, h, 0, 0))
    s_shape = jax.ShapeDtypeStruct((B, N_HEADS, HEAD_DK, HEAD_DV), F32)
    return pl.pallas_call(
        functools.partial(_gla_kernel, heads, nt, has_init, seqs),
        grid=(B // seqs, hg),
        in_specs=([k_spec] * 6 + [dt_spec] * 2 + [v_spec, v_spec] + [s_spec] * (2 if has_init else 0)
                  + [_const_spec((1, HEAD_DV))]),
        out_specs=[v_spec, s_spec, s_spec],
        out_shape=[jax.ShapeDtypeStruct((B, T, VDIM), BF16), s_shape, s_shape],
        scratch_shapes=[pltpu.VMEM((seqs, 2, heads, HEAD_DK, HEAD_DV), F32),
                        pltpu.VMEM((seqs, 2, heads, CHUNKS_PER_TILE, HEAD_DK, HEAD_DV), F32),
                        pltpu.VMEM((seqs, T, heads * HEAD_DV), F32),
                        pltpu.VMEM((seqs, T, heads * HEAD_DV), F32)],
        compiler_params=pltpu.CompilerParams(
            dimension_semantics=("arbitrary", "arbitrary"), vmem_limit_bytes=VMEM_LIMIT_BYTES),
        name="gla",
    )(qdf, kif, kef, qdb, kib, keb, dtf, dtb, v, ga, *(init_states or ()), gla_norm_g)


def _mix_weights_kernel(tab_ref, wf_ref, o_ref):
    for g in range(N_FGROUPS):
        o_ref[g] = jnp.dot(tab_ref[...], wf_ref[g], preferred_element_type=F32,
                           precision=lax.Precision.HIGHEST).astype(BF16)


def _mix_weights(wf):
    n = np.arange(FGROUP_CH)
    ang = 2.0 * np.pi * ((n[:, None] * n[None, :]) % FGROUP_CH) / FGROUP_CH
    tab = jnp.asarray(np.concatenate([np.cos(ang), np.sin(ang)], axis=0) / np.sqrt(FGROUP_CH), F32)
    return pl.pallas_call(
        _mix_weights_kernel,
        out_shape=jax.ShapeDtypeStruct((N_FGROUPS, 2 * FGROUP_CH, FGROUP_CH), BF16),
        name="mix_weights",
    )(tab, wf)


def _mix_and_gate(fp, fq, mix_ref, gb, store):
    fp = fp.astype(BF16)
    fq = fq.astype(BF16)
    for g in range(N_FGROUPS):
        cols = slice(g * FGROUP_CH, (g + 1) * FGROUP_CH)
        z = _mm(jnp.concatenate([fp[:, cols], fq[:, cols]], axis=1), mix_ref[g])
        store(cols, (z * gb(cols).astype(F32)).astype(BF16))


def _fourier_kernel(seqs, ct_ref, st_ref, u_ref, gb_ref, mix_ref, o_ref):
    for bb in range(seqs):
        u = u_ref[bb]

        def store(cols, val, bb=bb):
            o_ref[bb, :, cols] = val

        _mix_and_gate(_mm(ct_ref[...], u), _mm(st_ref[...], u), mix_ref,
                      lambda cols, bb=bb: gb_ref[bb, :, cols], store)


def _fourier(ct, stn, u, gb, mix):
    B, T, _ = u.shape
    tf = min(T, FOURIER_ROWS_PER_STEP)
    seqs = FOURIER_SHORT_SEQS_PER_STEP if T < FOURIER_ROWS_PER_STEP else 1
    assert B % seqs == 0
    tab_spec = pl.BlockSpec((tf, T), lambda b, t: (t, 0))
    seq_spec = pl.BlockSpec((seqs, T, FOURIER_DIM), lambda b, t: (b, 0, 0))
    tile_spec = pl.BlockSpec((seqs, tf, FOURIER_DIM), lambda b, t: (b, t, 0))
    return pl.pallas_call(
        functools.partial(_fourier_kernel, seqs),
        grid=(B // seqs, T // tf),
        in_specs=[tab_spec, tab_spec, seq_spec, tile_spec,
                  _const_spec((N_FGROUPS, 2 * FGROUP_CH, FGROUP_CH))],
        out_specs=tile_spec,
        out_shape=jax.ShapeDtypeStruct((B, T, FOURIER_DIM), BF16),
        compiler_params=pltpu.CompilerParams(
            dimension_semantics=("arbitrary", "arbitrary"), vmem_limit_bytes=VMEM_LIMIT_BYTES),
        name="fourier",
    )(ct, stn, u, gb, mix)


def _fourier_split_kernel(ce_ref, se_ref, co_ref, so_ref, u_ref, gb_ref, mix_ref, o_ref, wide, u_eo):
    half = u_eo.shape[1]
    for g in range(N_FGROUPS):
        cols = slice(g * FGROUP_CH, (g + 1) * FGROUP_CH)
        wide[g] = u_ref[:, cols].astype(F32)
        for parity in range(2):
            u_eo[parity, :, cols] = wide[g, pl.ds(parity, half, stride=2), :].astype(BF16)

    for kb in range(half // FOURIER_ROWS_PER_STEP):
        rows = slice(kb * FOURIER_ROWS_PER_STEP, (kb + 1) * FOURIER_ROWS_PER_STEP)
        ep, eq = _mm(ce_ref[rows, :], u_eo[0]), _mm(se_ref[rows, :], u_eo[0])
        op, oq = _mm(co_ref[rows, :], u_eo[1]), _mm(so_ref[rows, :], u_eo[1])
        for upper, (fp, fq) in enumerate(((ep + op, eq + oq), (ep - op, eq - oq))):
            def store(cols, val, upper=upper):
                o_ref[upper, rows, cols] = val

            _mix_and_gate(fp, fq, mix_ref, lambda cols, upper=upper: gb_ref[upper, rows, cols], store)


def _fourier_split(u, gb, mix):
    B, T, _ = u.shape
    half = T // 2
    ce, sen = _time_dft_tables(T, half, 2 * np.arange(half))
    co, son = _time_dft_tables(T, half, 2 * np.arange(half) + 1)
    seq_spec = pl.BlockSpec((None, T, FOURIER_DIM), lambda b: (b, 0, 0))
    halves_spec = pl.BlockSpec((None, 2, half, FOURIER_DIM), lambda b: (b, 0, 0, 0))
    out = pl.pallas_call(
        _fourier_split_kernel,
        grid=(B,),
        in_specs=[_const_spec((half, half))] * 4 + [seq_spec, halves_spec,
                                                     _const_spec((N_FGROUPS, 2 * FGROUP_CH, FGROUP_CH))],
        out_specs=halves_spec,
        out_shape=jax.ShapeDtypeStruct((B, 2, half, FOURIER_DIM), BF16),
        scratch_shapes=[pltpu.VMEM((N_FGROUPS, T, FGROUP_CH), F32),
                        pltpu.VMEM((2, half, FOURIER_DIM), BF16)],
        compiler_params=pltpu.CompilerParams(
            dimension_semantics=("arbitrary",), vmem_limit_bytes=VMEM_LIMIT_BYTES),
        name="fourier_split",
    )(ce, sen, co, son, u, gb.reshape(B, 2, half, FOURIER_DIM), mix)
    return out.reshape(B, T, FOURIER_DIM)


def _out_proj_kernel(og_ref, fg_ref, sa_ref, sb_ref, x_ref, gate_ref, fng_ref, wpa_ref, wpb_ref, wo_ref, y_ref):
    for i in range(OUT_TILES_PER_STEP):
        rows = slice(i * TOKEN_TILE, (i + 1) * TOKEN_TILE)
        ya = _mm(og_ref[rows, :], wpa_ref[...])
        yb = _mm(fg_ref[rows, :], wpb_ref[...])
        merged = sa_ref[rows, :].astype(F32) * ya + sb_ref[rows, :].astype(F32) * yb
        xo = x_ref[rows, :] + gate_ref[...] * _mm(merged.astype(BF16), wo_ref[...])
        y_ref[rows, :] = xo * lax.rsqrt(jnp.mean(xo * xo, axis=-1, keepdims=True) + EPS) * fng_ref[...]


def _out_proj(og, fg, sa, sb, x, gate, final_norm_g, wpa, wpb, wo):
    B, T, _ = x.shape
    per_batch_mod = gate.shape[0] > 1
    step_rows = OUT_TILES_PER_STEP * TOKEN_TILE
    if T < step_rows:
        assert not per_batch_mod and (B * T) % step_rows == 0
        fold = lambda a: a.reshape(B * T // step_rows, step_rows, a.shape[-1])
        y = _out_proj(fold(og), fold(fg), fold(sa), fold(sb), fold(x), gate, final_norm_g, wpa, wpb, wo)
        return y.reshape(B, T, D_MODEL)

    def tok_spec(cols):
        return pl.BlockSpec((None, step_rows, cols), lambda b, t: (b, t, 0))

    mod_spec = pl.BlockSpec((None, 1, D_MODEL), (lambda b, t: (b, 0, 0)) if per_batch_mod else (lambda b, t: (0, 0, 0)))
    return pl.pallas_call(
        _out_proj_kernel,
        grid=(B, T // step_rows),
        in_specs=[tok_spec(VDIM), tok_spec(FOURIER_DIM), tok_spec(D_MODEL), tok_spec(D_MODEL), tok_spec(D_MODEL),
                  mod_spec, _const_spec((1, D_MODEL)),
                  _const_spec((VDIM, D_MODEL)), _const_spec((FOURIER_DIM, D_MODEL)), _const_spec((D_MODEL, D_MODEL))],
        out_specs=tok_spec(D_MODEL),
        out_shape=jax.ShapeDtypeStruct((B, T, D_MODEL), F32),
        compiler_params=pltpu.CompilerParams(
            dimension_semantics=("arbitrary", "arbitrary"), vmem_limit_bytes=VMEM_LIMIT_BYTES),
        name="out_proj",
    )(og, fg, sa, sb, x, gate, final_norm_g, wpa, wpb, wo)


def _time_dft_tables(T, n_rows, positions):
    lo = 32
    hi = n_rows // lo
    n = np.asarray(positions)
    ang_hi = 2.0 * np.pi * (((np.arange(hi)[:, None] * lo) * n[None, :]) % T) / T
    ang_lo = 2.0 * np.pi * ((np.arange(lo)[:, None] * n[None, :]) % T) / T
    scale = 1.0 / np.sqrt(T)
    ch = jnp.asarray(np.cos(ang_hi) * scale, F32)[:, None, :]
    sh = jnp.asarray(np.sin(ang_hi) * scale, F32)[:, None, :]
    cl = jnp.asarray(np.cos(ang_lo), F32)[None, :, :]
    sl = jnp.asarray(np.sin(ang_lo), F32)[None, :, :]
    ct = (ch * cl - sh * sl).astype(BF16).reshape(n_rows, n.size)
    stn = (-(sh * cl + ch * sl)).astype(BF16).reshape(n_rows, n.size)
    return ct, stn


def _rope_tables(T):
    rows = T // GRID_W
    r = np.repeat(np.arange(rows), GRID_W).astype(np.float64)
    c = np.tile(np.arange(GRID_W), rows).astype(np.float64)
    n_freq = HEAD_DK // 4
    freqs = ROPE_BASE ** (-np.arange(n_freq, dtype=np.float64) / n_freq)
    ang_r = r[:, None] * freqs
    ang_c = c[:, None] * freqs
    cos = np.concatenate([np.cos(ang_r), np.cos(ang_r), np.cos(ang_c), np.cos(ang_c)], axis=-1)
    sin = np.concatenate([-np.sin(ang_r), np.sin(ang_r), -np.sin(ang_c), np.sin(ang_c)], axis=-1)
    return jnp.asarray(cos, F32), jnp.asarray(sin, F32)


def _path(x, scale, shift, gate, init_states, rope, heads, wts):
    T = x.shape[1]
    (qdf, kif, kef, qdb, kib, keb, dtf, dtb, v, ga, u, gb, sa, sb) = _in_proj(
        x, scale, shift, wts["norm_g"], wts["w_head"], wts["w_tail"], wts["w_r"], wts["w_ab"], wts["b_ab"], rope)
    og, sf, sbw = _gla(qdf, kif, kef, qdb, kib, keb, dtf, dtb, v, ga, init_states, wts["gla_norm_g"], heads)
    if T >= 4 * FOURIER_ROWS_PER_STEP:
        fg = _fourier_split(u, gb, wts["four_mix"])
    else:
        ct, stn = _time_dft_tables(T, T, np.arange(T))
        fg = _fourier(ct, stn, u, gb, wts["four_mix"])
    y = _out_proj(og, fg, sa, sb, x, gate, wts["final_norm_g"], wts["w_proj_a"], wts["w_proj_b"], wts["w_out"])
    return y, sf, sbw


def kernel(x_prompt, x_sample, state_gla_fwd, state_gla_bwd, c, c_ctx, w_ada, b_ada, norm_g, w_in,
           w_alpha_fwd, b_alpha_fwd, w_alpha_bwd, b_alpha_bwd, gla_norm_g, w_four, w_proj_a, w_proj_b,
           w_out, final_norm_g):
    depth = w_in.shape[0]
    assert depth == 1, "single trunk layer"
    bs = x_sample.shape[0]

    n_cond = bs + 1
    cond_rows = -(-n_cond // 8) * 8
    cond = jnp.concatenate([c, c_ctx[None, :], jnp.zeros((cond_rows - n_cond, D_MODEL), F32)], axis=0)
    ada = _ada(cond, w_ada[0], b_ada[0][None, :])
    shift, scale, gate = ada[:, :D_MODEL], ada[:, D_MODEL:2 * D_MODEL], ada[:, 2 * D_MODEL:]
    mod = lambda m, lo, hi: m[lo:hi][:, None, :]

    wi = w_in[0]
    o_r = _W_HEAD_COLS
    o_u = o_r + 2 * GATE_RANK
    w_r = wi[:, o_r:o_u].T.astype(BF16)
    w_ab = jnp.zeros((2 * GATE_RANK, 2 * KDIM), F32)
    w_ab = w_ab.at[:GATE_RANK, :KDIM].set(w_alpha_fwd[0]).at[GATE_RANK:, KDIM:].set(w_alpha_bwd[0])
    wts = dict(
        norm_g=norm_g[0][None, :], w_head=wi[:, :o_r].astype(BF16), w_tail=wi[:, o_u:].astype(BF16),
        w_r=w_r, w_ab=w_ab.astype(BF16),
        b_ab=jnp.concatenate([b_alpha_fwd[0], b_alpha_bwd[0]])[None, :],
        gla_norm_g=gla_norm_g[0][None, :], four_mix=_mix_weights(w_four[0]),
        w_proj_a=w_proj_a[0].astype(BF16), w_proj_b=w_proj_b[0].astype(BF16), w_out=w_out[0].astype(BF16),
        final_norm_g=final_norm_g[None, :])

    y_prompt, sf, sb = _path(x_prompt, mod(scale, bs, bs + 1), mod(shift, bs, bs + 1), mod(gate, bs, bs + 1),
                             None, None, N_HEADS, wts)
    y_sample, _, _ = _path(x_sample, mod(scale, 0, bs), mod(shift, 0, bs), mod(gate, 0, bs),
                           (state_gla_fwd[:, 0], state_gla_bwd[:, 0]), _rope_tables(x_sample.shape[1]), 2, wts)
    return (y_prompt, y_sample, sf[:, None].astype(x_prompt.dtype), sb[:, None].astype(x_prompt.dtype))
```

```python
import functools

import numpy as np
import jax
import jax.numpy as jnp
from jax import lax
from jax.experimental import pallas as pl
from jax.experimental.pallas import tpu as pltpu

F32 = jnp.float32
BF16 = jnp.bfloat16

D_MODEL = 1024
N_HEADS = 4
HEAD_DK = 128
HEAD_DV = 256
KDIM = N_HEADS * HEAD_DK
VDIM = N_HEADS * HEAD_DV
GATE_RANK = 16
GATE_NORM = 16.0
CHUNK = 64
N_FGROUPS = 4
FGROUP_CH = 128
FOURIER_DIM = N_FGROUPS * FGROUP_CH
GRID_W = 64
ROPE_BASE = 10000.0
EPS = 1e-6

LANES = 128
MXU_COLS = 256
TOKEN_TILE = 256
CHUNKS_PER_TILE = TOKEN_TILE // CHUNK
IN_TILES_PER_STEP = 2
OUT_TILES_PER_STEP = 4
GLA_SHORT_SEQS_PER_STEP = 4
FOURIER_ROWS_PER_STEP = 512
FOURIER_SHORT_SEQS_PER_STEP = 8
VMEM_LIMIT_BYTES = 56 * 1024 * 1024

_OFF_QK = 0
_OFF_V = _OFF_QK + 2 * KDIM
_OFF_GA = _OFF_V + VDIM
_W_HEAD_COLS = _OFF_GA + VDIM
_OFF_U = 0
_OFF_GB = _OFF_U + FOURIER_DIM
_OFF_MA = _OFF_GB + FOURIER_DIM
_OFF_MB = _OFF_MA + D_MODEL
_W_TAIL_COLS = _OFF_MB + D_MODEL


def _mm(a, b):
    return jnp.dot(a, b, preferred_element_type=F32)


def _mm_ta(a, b):
    return lax.dot_general(a, b, (((0,), (0,)), ((), ())), preferred_element_type=F32)


def _mm_tb(a, b):
    return lax.dot_general(a, b, (((1,), (1,)), ((), ())), preferred_element_type=F32)


def _split_bf16(x):
    hi = x.astype(BF16)
    lo = (x - hi.astype(F32)).astype(BF16)
    return hi, lo


def _const_spec(shape):
    nd = len(shape)
    return pl.BlockSpec(shape, lambda *_: (0,) * nd)


def _ada_kernel(c_ref, w_ref, b_ref, o_ref):
    c = c_ref[...]
    s = c * jax.nn.sigmoid(c)
    o_ref[...] = _mm(s.astype(BF16), w_ref[...].astype(BF16)) + b_ref[...]


def _ada(cond, w, b):
    rows = cond.shape[0]
    return pl.pallas_call(
        _ada_kernel,
        grid=(3,),
        in_specs=[pl.BlockSpec((rows, D_MODEL), lambda n: (0, 0)),
                  pl.BlockSpec((D_MODEL, D_MODEL), lambda n: (0, n)),
                  pl.BlockSpec((1, D_MODEL), lambda n: (0, n))],
        out_specs=pl.BlockSpec((rows, D_MODEL), lambda n: (0, n)),
        out_shape=jax.ShapeDtypeStruct((rows, 3 * D_MODEL), F32),
        compiler_params=pltpu.CompilerParams(
            dimension_semantics=("arbitrary",), vmem_limit_bytes=VMEM_LIMIT_BYTES),
        name="ada",
    )(cond, w, b)


def _in_proj_kernel(use_rope, *refs):
    n_shared = 8
    n_in = 1 + n_shared + (2 if use_rope else 0)
    dt_slots = (n_in + 6, n_in + 7)
    tiles = []
    for i in range(IN_TILES_PER_STEP):
        rows = pl.ds(i * TOKEN_TILE, TOKEN_TILE)
        tile_refs = []
        for idx, ref in enumerate(refs):
            if 1 <= idx <= n_shared:
                tile_refs.append(ref)
            elif idx in dt_slots:
                tile_refs.append(ref.at[i])
            else:
                tile_refs.append(ref.at[rows])
        tiles.append(_in_proj_tile(use_rope, *tile_refs))
    for _ in range(_IN_PROJ_STAGES):
        for tile in tiles:
            next(tile)


def _project(hb, w_ref, off, width, act, out_ref):
    for n in range(0, width, MXU_COLS):
        z = _mm(hb, w_ref[:, off + n:off + n + MXU_COLS])
        out_ref[:, n:n + MXU_COLS] = act(z).astype(BF16)


_IN_PROJ_STAGES = 3


def _in_proj_tile(use_rope, *refs):
    if use_rope:
        (x_ref, sc_ref, sh_ref, ng_ref, wa_ref, wb_ref, wr_ref, wab_ref, bab_ref, cos_ref, sin_ref,
         qdf_ref, kif_ref, kef_ref, qdb_ref, kib_ref, keb_ref, dtf_ref, dtb_ref,
         v_ref, ga_ref, u_ref, gb_ref, sa_ref, sb_ref) = refs
    else:
        (x_ref, sc_ref, sh_ref, ng_ref, wa_ref, wb_ref, wr_ref, wab_ref, bab_ref,
         qdf_ref, kif_ref, kef_ref, qdb_ref, kib_ref, keb_ref, dtf_ref, dtb_ref,
         v_ref, ga_ref, u_ref, gb_ref, sa_ref, sb_ref) = refs

    x = x_ref[...]
    xn = x * lax.rsqrt(jnp.mean(x * x, axis=-1, keepdims=True) + EPS)
    h = xn * (ng_ref[...] * (1.0 + sc_ref[...])) + sh_ref[...]
    hb = h.astype(BF16)

    r_t = lax.dot_general(wr_ref[...], hb, (((0,), (1,)), ((), ())), preferred_element_type=F32)
    qk = _mm(hb, wa_ref[:, _OFF_QK:_OFF_QK + 2 * KDIM])
    xg = _mm_ta(r_t.astype(BF16), wab_ref[...]) + bab_ref[...]
    yield
    _project(hb, wa_ref, _OFF_V, VDIM, lambda z: z, v_ref)
    g_all = (jnp.minimum(xg, 0.0) - jnp.log(1.0 + jnp.exp(-jnp.abs(xg)))) * (1.0 / GATE_NORM)

    row = lax.broadcasted_iota(jnp.int32, (CHUNK, 2 * CHUNK), 0)
    col = lax.broadcasted_iota(jnp.int32, (CHUNK, 2 * CHUNK), 1) % CHUNK
    bcs = []
    for direction in range(2):
        g_hi, g_lo = _split_bf16(g_all[:, direction * KDIM:(direction + 1) * KDIM])
        tri = jnp.where((col <= row) if direction == 0 else (col >= row), 1.0, 0.0).astype(BF16)
        bc_chunks = []
        for c in range(CHUNKS_PER_TILE):
            cr = slice(c * CHUNK, (c + 1) * CHUNK)
            bc_chunks.append(_mm(tri, jnp.concatenate([g_hi[cr], g_lo[cr]], axis=0)))
        bcs.append(jnp.concatenate(bc_chunks, axis=0))
    yield

    q = qk[:, :KDIM] * (HEAD_DK ** -0.5)
    k = qk[:, KDIM:]
    if use_rope:
        cos = cos_ref[...]
        sin = sin_ref[...]
        lane = lax.broadcasted_iota(jnp.int32, (TOKEN_TILE, HEAD_DK), 1)
        first_half = (lane // (HEAD_DK // 4)) % 2 == 0

        def rope(t):
            outs = []
            for hh in range(N_HEADS):
                th = t[:, hh * HEAD_DK:(hh + 1) * HEAD_DK]
                partner = jnp.where(first_half,
                                    pltpu.roll(th, HEAD_DK - HEAD_DK // 4, axis=1),
                                    pltpu.roll(th, HEAD_DK // 4, axis=1))
                outs.append(th * cos + partner * sin)
            return jnp.concatenate(outs, axis=1)

        q = rope(q)
        k = rope(k)

    for direction, (qd_ref, ki_ref, ke_ref, dt_ref) in enumerate(
            ((qdf_ref, kif_ref, kef_ref, dtf_ref), (qdb_ref, kib_ref, keb_ref, dtb_ref))):
        bc = bcs[direction]
        edge = CHUNK - 1 if direction == 0 else 0
        bl_rows = bc.reshape(CHUNKS_PER_TILE, CHUNK, KDIM)[:, edge, :]
        bl = jnp.broadcast_to(bl_rows[:, None, :], (CHUNKS_PER_TILE, CHUNK, KDIM)).reshape(TOKEN_TILE, KDIM)
        qd_ref[...] = (q * jnp.exp(bc)).astype(BF16)
        ki_ref[...] = (k * jnp.exp(-bc)).astype(BF16)
        ke_ref[...] = (k * jnp.exp(bl - bc)).astype(BF16)
        padded = jnp.concatenate([bl_rows, jnp.zeros((LANES - CHUNKS_PER_TILE, KDIM), F32)], axis=0)
        dt_ref[...] = jnp.exp(padded.T)

    _project(hb, wa_ref, _OFF_GA, VDIM, jax.nn.silu, ga_ref)
    _project(hb, wb_ref, _OFF_U, FOURIER_DIM, lambda z: z, u_ref)
    _project(hb, wb_ref, _OFF_GB, FOURIER_DIM, jax.nn.silu, gb_ref)
    _project(hb, wb_ref, _OFF_MA, D_MODEL, jax.nn.sigmoid, sa_ref)
    _project(hb, wb_ref, _OFF_MB, D_MODEL, jax.nn.sigmoid, sb_ref)
    yield


def _in_proj(x, scale, shift, norm_g, w_head, w_tail, w_r, w_ab, b_ab, rope):
    B, T, _ = x.shape
    use_rope = rope is not None
    per_batch_mod = scale.shape[0] > 1
    step_rows = IN_TILES_PER_STEP * TOKEN_TILE
    if T < step_rows:
        assert not per_batch_mod and not use_rope and (B * T) % step_rows == 0
        outs = _in_proj(x.reshape(B * T // step_rows, step_rows, D_MODEL), scale, shift, norm_g,
                        w_head, w_tail, w_r, w_ab, b_ab, rope)
        return [o.reshape((B, T // TOKEN_TILE) + o.shape[2:]) if o.ndim == 4 else o.reshape(B, T, o.shape[-1])
                for o in outs]
    nt = T // TOKEN_TILE

    def tok_spec(cols):
        return pl.BlockSpec((None, step_rows, cols), lambda b, t: (b, t, 0))

    mod_spec = pl.BlockSpec((None, 1, D_MODEL), (lambda b, t: (b, 0, 0)) if per_batch_mod else (lambda b, t: (0, 0, 0)))
    in_specs = [
        tok_spec(D_MODEL), mod_spec, mod_spec, _const_spec((1, D_MODEL)),
        _const_spec((D_MODEL, _W_HEAD_COLS)), _const_spec((D_MODEL, _W_TAIL_COLS)),
        _const_spec((D_MODEL, 2 * GATE_RANK)),
        _const_spec((2 * GATE_RANK, 2 * KDIM)), _const_spec((1, 2 * KDIM)),
    ]
    args = [x, scale, shift, norm_g, w_head, w_tail, w_r, w_ab, b_ab]
    if use_rope:
        in_specs += [pl.BlockSpec((step_rows, HEAD_DK), lambda b, t: (t, 0))] * 2
        args += list(rope)

    dt_spec = pl.BlockSpec((None, IN_TILES_PER_STEP, KDIM, LANES), lambda b, t: (b, t, 0, 0))
    tok_bf = lambda cols: jax.ShapeDtypeStruct((B, T, cols), BF16)
    dt_shape = jax.ShapeDtypeStruct((B, nt, KDIM, LANES), F32)
    out_specs = [tok_spec(KDIM)] * 6 + [dt_spec] * 2 + [
        tok_spec(VDIM), tok_spec(VDIM), tok_spec(FOURIER_DIM),
        tok_spec(FOURIER_DIM), tok_spec(D_MODEL), tok_spec(D_MODEL)]
    out_shape = [tok_bf(KDIM)] * 6 + [dt_shape] * 2 + [
        tok_bf(VDIM), tok_bf(VDIM), tok_bf(FOURIER_DIM),
        tok_bf(FOURIER_DIM), tok_bf(D_MODEL), tok_bf(D_MODEL)]
    return pl.pallas_call(
        functools.partial(_in_proj_kernel, use_rope),
        grid=(B, T // step_rows),
        in_specs=in_specs,
        out_specs=out_specs,
        out_shape=out_shape,
        compiler_params=pltpu.CompilerParams(
            dimension_semantics=("arbitrary", "arbitrary"), vmem_limit_bytes=VMEM_LIMIT_BYTES),
        name="in_proj_rope" if use_rope else "in_proj",
    )(*args)


def _gla_kernel(heads, n_tiles, has_init, seqs, *refs):
    gain_pos = 10 + (2 if has_init else 0)
    for bb in range(seqs):
        _gla_seq(heads, n_tiles, has_init, *[r if i == gain_pos else r.at[bb] for i, r in enumerate(refs)])


def _gla_seq(heads, n_tiles, has_init, *refs):
    (qdf_ref, kif_ref, kef_ref, qdb_ref, kib_ref, keb_ref, dtf_ref, dtb_ref, v_ref, ga_ref) = refs[:10]
    refs = refs[10:]
    if has_init:
        s0f_ref, s0b_ref = refs[:2]
        refs = refs[2:]
    gn_ref, og_ref, sf_ref, sb_ref, state, d_state, of_acc, ob_acc = refs
    if has_init:
        state[0] = s0f_ref[...]
        state[1] = s0b_ref[...]
    else:
        state[...] = jnp.zeros_like(state)
    dir_refs = ((qdf_ref, kif_ref, kef_ref, dtf_ref, of_acc), (qdb_ref, kib_ref, keb_ref, dtb_ref, ob_acc))

    def tile_body(finalize, j, carry):
        row = lax.broadcasted_iota(jnp.int32, (TOKEN_TILE, TOKEN_TILE), 0)
        col = lax.broadcasted_iota(jnp.int32, (TOKEN_TILE, TOKEN_TILE), 1)
        same_chunk = (row // CHUNK) == (col // CHUNK)
        causal = (same_chunk & (col <= row), same_chunk & (col >= row))
        tiles = (j, n_tiles - 1 - j)
        bases = tuple(pl.multiple_of(t * TOKEN_TILE, TOKEN_TILE) for t in tiles)

        scores = {}
        for direction in range(2):
            qd_ref, ki_ref, ke_ref, _, o_acc = dir_refs[direction]
            trows = pl.ds(bases[direction], TOKEN_TILE)
            for hh in range(heads):
                kc = slice(hh * HEAD_DK, (hh + 1) * HEAD_DK)
                vc = slice(hh * HEAD_DV, (hh + 1) * HEAD_DV)
                scores[direction, hh] = jnp.where(
                    causal[direction], _mm_tb(qd_ref[trows, kc], ki_ref[trows, kc]), 0.0).astype(BF16)
                for c in range(CHUNKS_PER_TILE):
                    rows = pl.ds(bases[direction] + c * CHUNK, CHUNK)
                    d_state[direction, hh, c] = _mm_ta(ke_ref[rows, kc], v_ref[rows, vc])

        steps = [(step, direction) for step in range(CHUNKS_PER_TILE) for direction in range(2)]
        if finalize[0] != finalize[1]:
            steps.sort(key=lambda sd: finalize[sd[1]])
        for step, direction in steps:
            qd_ref, _, _, dt_ref, o_acc = dir_refs[direction]
            other_acc = dir_refs[1 - direction][4]
            c = step if direction == 0 else CHUNKS_PER_TILE - 1 - step
            rows = pl.ds(bases[direction] + c * CHUNK, CHUNK)
            for hh in range(heads):
                kc = slice(hh * HEAD_DK, (hh + 1) * HEAD_DK)
                vc = slice(hh * HEAD_DV, (hh + 1) * HEAD_DV)
                s = state[direction, hh]
                blk = (c * CHUNK) // LANES
                sc = scores[direction, hh][c * CHUNK:(c + 1) * CHUNK, blk * LANES:(blk + 1) * LANES]
                v_blk = v_ref[pl.ds(bases[direction] + blk * LANES, LANES), vc]
                o = _mm(jnp.concatenate([qd_ref[rows, kc], sc], axis=1),
                        jnp.concatenate([s.astype(BF16), v_blk], axis=0))
                if finalize[direction]:
                    o = o + other_acc[rows, vc]
                    on = o * lax.rsqrt(jnp.mean(o * o, axis=-1, keepdims=True) + EPS) * gn_ref[...]
                    og_ref[rows, vc] = (on * ga_ref[rows, vc].astype(F32)).astype(BF16)
                else:
                    o_acc[rows, vc] = o
                dec = dt_ref[tiles[direction], kc, c:c + 1]
                state[direction, hh] = dec * s + d_state[direction, hh, c]
        return carry

    first_half = n_tiles // 2
    unroll = 4 if first_half % 4 == 0 else 1
    lax.fori_loop(0, first_half, functools.partial(tile_body, (False, False)), 0, unroll=unroll)
    if n_tiles % 2:
        tile_body((False, True), first_half, 0)
    lax.fori_loop(n_tiles - first_half, n_tiles, functools.partial(tile_body, (True, True)), 0, unroll=unroll)
    sf_ref[...] = state[0]
    sb_ref[...] = state[1]


def _gla(qdf, kif, kef, qdb, kib, keb, dtf, dtb, v, ga, init_states, gla_norm_g, heads):
    B, T, _ = v.shape
    nt = T // TOKEN_TILE
    hg = N_HEADS // heads
    has_init = init_states is not None
    seqs = GLA_SHORT_SEQS_PER_STEP if nt == 1 else 1
    assert B % seqs == 0
    k_spec = pl.BlockSpec((seqs, T, heads * HEAD_DK), lambda b, h: (b, 0, h))
    v_spec = pl.BlockSpec((seqs, T, heads * HEAD_DV), lambda b, h: (b, 0, h))
    dt_spec = pl.BlockSpec((seqs, nt, heads * HEAD_DK, LANES), lambda b, h: (b, 0, h, 0))
    s_spec = pl.BlockSpec((seqs, heads, HEAD_DK, HEAD_DV), lambda b, h: (b, h, 0, 0))
    s_shape = jax.ShapeDtypeStruct((B, N_HEADS, HEAD_DK, HEAD_DV), F32)
    return pl.pallas_call(
        functools.partial(_gla_kernel, heads, nt, has_init, seqs),
        grid=(B // seqs, hg),
        in_specs=([k_spec] * 6 + [dt_spec] * 2 + [v_spec, v_spec] + [s_spec] * (2 if has_init else 0)
                  + [_const_spec((1, HEAD_DV))]),
        out_specs=[v_spec, s_spec, s_spec],
        out_shape=[jax.ShapeDtypeStruct((B, T, VDIM), BF16), s_shape, s_shape],
        scratch_shapes=[pltpu.VMEM((seqs, 2, heads, HEAD_DK, HEAD_DV), F32),
                        pltpu.VMEM((seqs, 2, heads, CHUNKS_PER_TILE, HEAD_DK, HEAD_DV), F32),
                        pltpu.VMEM((seqs, T, heads * HEAD_DV), F32),
                        pltpu.VMEM((seqs, T, heads * HEAD_DV), F32)],
        compiler_params=pltpu.CompilerParams(
            dimension_semantics=("arbitrary", "arbitrary"), vmem_limit_bytes=VMEM_LIMIT_BYTES),
        name="gla",
    )(qdf, kif, kef, qdb, kib, keb, dtf, dtb, v, ga, *(init_states or ()), gla_norm_g)


def _mix_weights_kernel(tab_ref, wf_ref, o_ref):
    for g in range(N_FGROUPS):
        o_ref[g] = jnp.dot(tab_ref[...], wf_ref[g], preferred_element_type=F32,
                           precision=lax.Precision.HIGHEST).astype(BF16)


def _mix_weights(wf):
    n = np.arange(FGROUP_CH)
    ang = 2.0 * np.pi * ((n[:, None] * n[None, :]) % FGROUP_CH) / FGROUP_CH
    tab = jnp.asarray(np.concatenate([np.cos(ang), np.sin(ang)], axis=0) / np.sqrt(FGROUP_CH), F32)
    return pl.pallas_call(
        _mix_weights_kernel,
        out_shape=jax.ShapeDtypeStruct((N_FGROUPS, 2 * FGROUP_CH, FGROUP_CH), BF16),
        name="mix_weights",
    )(tab, wf)


def _mix_and_gate(fp, fq, mix_ref, gb, store):
    fp = fp.astype(BF16)
    fq = fq.astype(BF16)
    for g in range(N_FGROUPS):
        cols = slice(g * FGROUP_CH, (g + 1) * FGROUP_CH)
        z = _mm(jnp.concatenate([fp[:, cols], fq[:, cols]], axis=1), mix_ref[g])
        store(cols, (z * gb(cols).astype(F32)).astype(BF16))


def _fourier_kernel(seqs, ct_ref, st_ref, u_ref, gb_ref, mix_ref, o_ref):
    for bb in range(seqs):
        u = u_ref[bb]

        def store(cols, val, bb=bb):
            o_ref[bb, :, cols] = val

        _mix_and_gate(_mm(ct_ref[...], u), _mm(st_ref[...], u), mix_ref,
                      lambda cols, bb=bb: gb_ref[bb, :, cols], store)


def _fourier(ct, stn, u, gb, mix):
    B, T, _ = u.shape
    tf = min(T, FOURIER_ROWS_PER_STEP)
    seqs = FOURIER_SHORT_SEQS_PER_STEP if T < FOURIER_ROWS_PER_STEP else 1
    assert B % seqs == 0
    tab_spec = pl.BlockSpec((tf, T), lambda b, t: (t, 0))
    seq_spec = pl.BlockSpec((seqs, T, FOURIER_DIM), lambda b, t: (b, 0, 0))
    tile_spec = pl.BlockSpec((seqs, tf, FOURIER_DIM), lambda b, t: (b, t, 0))
    return pl.pallas_call(
        functools.partial(_fourier_kernel, seqs),
        grid=(B // seqs, T // tf),
        in_specs=[tab_spec, tab_spec, seq_spec, tile_spec,
                  _const_spec((N_FGROUPS, 2 * FGROUP_CH, FGROUP_CH))],
        out_specs=tile_spec,
        out_shape=jax.ShapeDtypeStruct((B, T, FOURIER_DIM), BF16),
        compiler_params=pltpu.CompilerParams(
            dimension_semantics=("arbitrary", "arbitrary"), vmem_limit_bytes=VMEM_LIMIT_BYTES),
        name="fourier",
    )(ct, stn, u, gb, mix)


def _fourier_split_kernel(ce_ref, se_ref, co_ref, so_ref, u_ref, gb_ref, mix_ref, o_ref, wide, u_eo):
    half = u_eo.shape[1]
    for g in range(N_FGROUPS):
        cols = slice(g * FGROUP_CH, (g + 1) * FGROUP_CH)
        wide[g] = u_ref[:, cols].astype(F32)
        for parity in range(2):
            u_eo[parity, :, cols] = wide[g, pl.ds(parity, half, stride=2), :].astype(BF16)

    for kb in range(half // FOURIER_ROWS_PER_STEP):
        rows = slice(kb * FOURIER_ROWS_PER_STEP, (kb + 1) * FOURIER_ROWS_PER_STEP)
        ep, eq = _mm(ce_ref[rows, :], u_eo[0]), _mm(se_ref[rows, :], u_eo[0])
        op, oq = _mm(co_ref[rows, :], u_eo[1]), _mm(so_ref[rows, :], u_eo[1])
        for upper, (fp, fq) in enumerate(((ep + op, eq + oq), (ep - op, eq - oq))):
            def store(cols, val, upper=upper):
                o_ref[upper, rows, cols] = val

            _mix_and_gate(fp, fq, mix_ref, lambda cols, upper=upper: gb_ref[upper, rows, cols], store)


def _fourier_split(u, gb, mix):
    B, T, _ = u.shape
    half = T // 2
    ce, sen = _time_dft_tables(T, half, 2 * np.arange(half))
    co, son = _time_dft_tables(T, half, 2 * np.arange(half) + 1)
    seq_spec = pl.BlockSpec((None, T, FOURIER_DIM), lambda b: (b, 0, 0))
    halves_spec = pl.BlockSpec((None, 2, half, FOURIER_DIM), lambda b: (b, 0, 0, 0))
    out = pl.pallas_call(
        _fourier_split_kernel,
        grid=(B,),
        in_specs=[_const_spec((half, half))] * 4 + [seq_spec, halves_spec,
                                                     _const_spec((N_FGROUPS, 2 * FGROUP_CH, FGROUP_CH))],
        out_specs=halves_spec,
        out_shape=jax.ShapeDtypeStruct((B, 2, half, FOURIER_DIM), BF16),
        scratch_shapes=[pltpu.VMEM((N_FGROUPS, T, FGROUP_CH), F32),
                        pltpu.VMEM((2, half, FOURIER_DIM), BF16)],
        compiler_params=pltpu.CompilerParams(
            dimension_semantics=("arbitrary",), vmem_limit_bytes=VMEM_LIMIT_BYTES),
        name="fourier_split",
    )(ce, sen, co, son, u, gb.reshape(B, 2, half, FOURIER_DIM), mix)
    return out.reshape(B, T, FOURIER_DIM)


def _out_proj_kernel(og_ref, fg_ref, sa_ref, sb_ref, x_ref, gate_ref, fng_ref, wpa_ref, wpb_ref, wo_ref, y_ref):
    for i in range(OUT_TILES_PER_STEP):
        rows = slice(i * TOKEN_TILE, (i + 1) * TOKEN_TILE)
        ya = _mm(og_ref[rows, :], wpa_ref[...])
        yb = _mm(fg_ref[rows, :], wpb_ref[...])
        merged = sa_ref[rows, :].astype(F32) * ya + sb_ref[rows, :].astype(F32) * yb
        xo = x_ref[rows, :] + gate_ref[...] * _mm(merged.astype(BF16), wo_ref[...])
        y_ref[rows, :] = xo * lax.rsqrt(jnp.mean(xo * xo, axis=-1, keepdims=True) + EPS) * fng_ref[...]


def _out_proj(og, fg, sa, sb, x, gate, final_norm_g, wpa, wpb, wo):
    B, T, _ = x.shape
    per_batch_mod = gate.shape[0] > 1
    step_rows = OUT_TILES_PER_STEP * TOKEN_TILE
    if T < step_rows:
        assert not per_batch_mod and (B * T) % step_rows == 0
        fold = lambda a: a.reshape(B * T // step_rows, step_rows, a.shape[-1])
        y = _out_proj(fold(og), fold(fg), fold(sa), fold(sb), fold(x), gate, final_norm_g, wpa, wpb, wo)
        return y.reshape(B, T, D_MODEL)

    def tok_spec(cols):
        return pl.BlockSpec((None, step_rows, cols), lambda b, t: (b, t, 0))

    mod_spec = pl.BlockSpec((None, 1, D_MODEL), (lambda b, t: (b, 0, 0)) if per_batch_mod else (lambda b, t: (0, 0, 0)))
    return pl.pallas_call(
        _out_proj_kernel,
        grid=(B, T // step_rows),
        in_specs=[tok_spec(VDIM), tok_spec(FOURIER_DIM), tok_spec(D_MODEL), tok_spec(D_MODEL), tok_spec(D_MODEL),
                  mod_spec, _const_spec((1, D_MODEL)),
                  _const_spec((VDIM, D_MODEL)), _const_spec((FOURIER_DIM, D_MODEL)), _const_spec((D_MODEL, D_MODEL))],
        out_specs=tok_spec(D_MODEL),
        out_shape=jax.ShapeDtypeStruct((B, T, D_MODEL), F32),
        compiler_params=pltpu.CompilerParams(
            dimension_semantics=("arbitrary", "arbitrary"), vmem_limit_bytes=VMEM_LIMIT_BYTES),
        name="out_proj",
    )(og, fg, sa, sb, x, gate, final_norm_g, wpa, wpb, wo)


def _time_dft_tables(T, n_rows, positions):
    lo = 32
    hi = n_rows // lo
    n = np.asarray(positions)
    ang_hi = 2.0 * np.pi * (((np.arange(hi)[:, None] * lo) * n[None, :]) % T) / T
    ang_lo = 2.0 * np.pi * ((np.arange(lo)[:, None] * n[None, :]) % T) / T
    scale = 1.0 / np.sqrt(T)
    ch = jnp.asarray(np.cos(ang_hi) * scale, F32)[:, None, :]
    sh = jnp.asarray(np.sin(ang_hi) * scale, F32)[:, None, :]
    cl = jnp.asarray(np.cos(ang_lo), F32)[None, :, :]
    sl = jnp.asarray(np.sin(ang_lo), F32)[None, :, :]
    ct = (ch * cl - sh * sl).astype(BF16).reshape(n_rows, n.size)
    stn = (-(sh * cl + ch * sl)).astype(BF16).reshape(n_rows, n.size)
    return ct, stn


def _rope_tables(T):
    rows = T // GRID_W
    r = np.repeat(np.arange(rows), GRID_W).astype(np.float64)
    c = np.tile(np.arange(GRID_W), rows).astype(np.float64)
    n_freq = HEAD_DK // 4
    freqs = ROPE_BASE ** (-np.arange(n_freq, dtype=np.float64) / n_freq)
    ang_r = r[:, None] * freqs
    ang_c = c[:, None] * freqs
    cos = np.concatenate([np.cos(ang_r), np.cos(ang_r), np.cos(ang_c), np.cos(ang_c)], axis=-1)
    sin = np.concatenate([-np.sin(ang_r), np.sin(ang_r), -np.sin(ang_c), np.sin(ang_c)], axis=-1)
    return jnp.asarray(cos, F32), jnp.asarray(sin, F32)


def _path(x, scale, shift, gate, init_states, rope, heads, wts):
    T = x.shape[1]
    (qdf, kif, kef, qdb, kib, keb, dtf, dtb, v, ga, u, gb, sa, sb) = _in_proj(
        x, scale, shift, wts["norm_g"], wts["w_head"], wts["w_tail"], wts["w_r"], wts["w_ab"], wts["b_ab"], rope)
    og, sf, sbw = _gla(qdf, kif, kef, qdb, kib, keb, dtf, dtb, v, ga, init_states, wts["gla_norm_g"], heads)
    if T >= 4 * FOURIER_ROWS_PER_STEP:
        fg = _fourier_split(u, gb, wts["four_mix"])
    else:
        ct, stn = _time_dft_tables(T, T, np.arange(T))
        fg = _fourier(ct, stn, u, gb, wts["four_mix"])
    y = _out_proj(og, fg, sa, sb, x, gate, wts["final_norm_g"], wts["w_proj_a"], wts["w_proj_b"], wts["w_out"])
    return y, sf, sbw


def kernel(x_prompt, x_sample, state_gla_fwd, state_gla_bwd, c, c_ctx, w_ada, b_ada, norm_g, w_in,
           w_alpha_fwd, b_alpha_fwd, w_alpha_bwd, b_alpha_bwd, gla_norm_g, w_four, w_proj_a, w_proj_b,
           w_out, final_norm_g):
    depth = w_in.shape[0]
    assert depth == 1, "single trunk layer"
    bs = x_sample.shape[0]

    n_cond = bs + 1
    cond_rows = -(-n_cond // 8) * 8
    cond = jnp.concatenate([c, c_ctx[None, :], jnp.zeros((cond_rows - n_cond, D_MODEL), F32)], axis=0)
    ada = _ada(cond, w_ada[0], b_ada[0][None, :])
    shift, scale, gate = ada[:, :D_MODEL], ada[:, D_MODEL:2 * D_MODEL], ada[:, 2 * D_MODEL:]
    mod = lambda m, lo, hi: m[lo:hi][:, None, :]

    wi = w_in[0]
    o_r = _W_HEAD_COLS
    o_u = o_r + 2 * GATE_RANK
    w_r = wi[:, o_r:o_u].astype(BF16)
    w_ab = jnp.zeros((2 * GATE_RANK, 2 * KDIM), F32)
    w_ab = w_ab.at[:GATE_RANK, :KDIM].set(w_alpha_fwd[0]).at[GATE_RANK:, KDIM:].set(w_alpha_bwd[0])
    wts = dict(
        norm_g=norm_g[0][None, :], w_head=wi.astype(BF16), w_tail=wi[:, o_u:].astype(BF16),
        w_r=w_r, w_ab=w_ab.astype(BF16),
        b_ab=jnp.concatenate([b_alpha_fwd[0], b_alpha_bwd[0]])[None, :],
        gla_norm_g=gla_norm_g[0][None, :], four_mix=_mix_weights(w_four[0]),
        w_proj_a=w_proj_a[0].astype(BF16), w_proj_b=w_proj_b[0].astype(BF16), w_out=w_out[0].astype(BF16),
        final_norm_g=final_norm_g[None, :])

    y_prompt, sf, sb = _path(x_prompt, mod(scale, bs, bs + 1), mod(shift, bs, bs + 1), mod(gate, bs, bs + 1),
                             None, None, N_HEADS, wts)
    y_sample, _, _ = _path(x_sample, mod(scale, 0, bs), mod(shift, 0, bs), mod(gate, 0, bs),
                           (state_gla_fwd[:, 0], state_gla_bwd[:, 0]), _rope_tables(x_sample.shape[1]), 2, wts)
    return (y_prompt, y_sample, sf[:, None].astype(x_prompt.dtype), sb[:, None].astype(x_prompt.dtype))
```

```python
import functools

import numpy as np
import jax
import jax.numpy as jnp
from jax import lax
from jax.experimental import pallas as pl
from jax.experimental.pallas import tpu as pltpu

F32 = jnp.float32
BF16 = jnp.bfloat16

D_MODEL = 1024
N_HEADS = 4
HEAD_DK = 128
HEAD_DV = 256
KDIM = N_HEADS * HEAD_DK
VDIM = N_HEADS * HEAD_DV
GATE_RANK = 16
GATE_NORM = 16.0
CHUNK = 64
N_FGROUPS = 4
FGROUP_CH = 128
FOURIER_DIM = N_FGROUPS * FGROUP_CH
GRID_W = 64
ROPE_BASE = 10000.0
EPS = 1e-6

LANES = 128
MXU_COLS = 256
TOKEN_TILE = 256
CHUNKS_PER_TILE = TOKEN_TILE // CHUNK
IN_TILES_PER_STEP = 2
OUT_TILES_PER_STEP = 4
GLA_SHORT_SEQS_PER_STEP = 4
FOURIER_ROWS_PER_STEP = 512
FOURIER_SHORT_SEQS_PER_STEP = 8
VMEM_LIMIT_BYTES = 56 * 1024 * 1024

_OFF_QK = 0
_OFF_V = _OFF_QK + 2 * KDIM
_OFF_GA = _OFF_V + VDIM
_W_HEAD_COLS = _OFF_GA + VDIM
_OFF_U = 0
_OFF_GB = _OFF_U + FOURIER_DIM
_OFF_MA = _OFF_GB + FOURIER_DIM
_OFF_MB = _OFF_MA + D_MODEL
_W_TAIL_COLS = _OFF_MB + D_MODEL


def _mm(a, b):
    return jnp.dot(a, b, preferred_element_type=F32)


def _mm_ta(a, b):
    return lax.dot_general(a, b, (((0,), (0,)), ((), ())), preferred_element_type=F32)


def _mm_tb(a, b):
    return lax.dot_general(a, b, (((1,), (1,)), ((), ())), preferred_element_type=F32)


def _split_bf16(x):
    hi = x.astype(BF16)
    lo = (x - hi.astype(F32)).astype(BF16)
    return hi, lo


def _const_spec(shape):
    nd = len(shape)
    return pl.BlockSpec(shape, lambda *_: (0,) * nd)


def _ada_kernel(c_ref, w_ref, b_ref, o_ref):
    c = c_ref[...]
    s = c * jax.nn.sigmoid(c)
    o_ref[...] = _mm(s.astype(BF16), w_ref[...].astype(BF16)) + b_ref[...]


def _ada(cond, w, b):
    rows = cond.shape[0]
    return pl.pallas_call(
        _ada_kernel,
        grid=(3,),
        in_specs=[pl.BlockSpec((rows, D_MODEL), lambda n: (0, 0)),
                  pl.BlockSpec((D_MODEL, D_MODEL), lambda n: (0, n)),
                  pl.BlockSpec((1, D_MODEL), lambda n: (0, n))],
        out_specs=pl.BlockSpec((rows, D_MODEL), lambda n: (0, n)),
        out_shape=jax.ShapeDtypeStruct((rows, 3 * D_MODEL), F32),
        compiler_params=pltpu.CompilerParams(
            dimension_semantics=("arbitrary",), vmem_limit_bytes=VMEM_LIMIT_BYTES),
        name="ada",
    )(cond, w, b)


def _split_w_in_kernel(w_ref, head_ref, r_ref, tail_ref):
    o_r = _W_HEAD_COLS
    o_u = o_r + 2 * GATE_RANK
    head_ref[...] = w_ref[:, :o_r].astype(BF16)
    r_ref[...] = w_ref[:, o_r:o_u].astype(BF16)
    tail_ref[...] = w_ref[:, o_u:o_u + _W_TAIL_COLS].astype(BF16)


def _split_w_in(w):
    rows = 128
    n_cols = w.shape[1]
    return pl.pallas_call(
        _split_w_in_kernel,
        grid=(D_MODEL // rows,),
        in_specs=[pl.BlockSpec((rows, n_cols), lambda i: (i, 0))],
        out_specs=[pl.BlockSpec((rows, _W_HEAD_COLS), lambda i: (i, 0)),
                   pl.BlockSpec((rows, 2 * GATE_RANK), lambda i: (i, 0)),
                   pl.BlockSpec((rows, _W_TAIL_COLS), lambda i: (i, 0))],
        out_shape=[jax.ShapeDtypeStruct((D_MODEL, _W_HEAD_COLS), BF16),
                   jax.ShapeDtypeStruct((D_MODEL, 2 * GATE_RANK), BF16),
                   jax.ShapeDtypeStruct((D_MODEL, _W_TAIL_COLS), BF16)],
        compiler_params=pltpu.CompilerParams(
            dimension_semantics=("arbitrary",), vmem_limit_bytes=VMEM_LIMIT_BYTES),
        name="split_w_in",
    )(w)


def _in_proj_kernel(use_rope, *refs):
    n_shared = 8
    n_in = 1 + n_shared + (2 if use_rope else 0)
    dt_slots = (n_in + 6, n_in + 7)
    tiles = []
    for i in range(IN_TILES_PER_STEP):
        rows = pl.ds(i * TOKEN_TILE, TOKEN_TILE)
        tile_refs = []
        for idx, ref in enumerate(refs):
            if 1 <= idx <= n_shared:
                tile_refs.append(ref)
            elif idx in dt_slots:
                tile_refs.append(ref.at[i])
            else:
                tile_refs.append(ref.at[rows])
        tiles.append(_in_proj_tile(use_rope, *tile_refs))
    for _ in range(_IN_PROJ_STAGES):
        for tile in tiles:
            next(tile)


def _project(hb, w_ref, off, width, act, out_ref):
    for n in range(0, width, MXU_COLS):
        z = _mm(hb, w_ref[:, off + n:off + n + MXU_COLS])
        out_ref[:, n:n + MXU_COLS] = act(z).astype(BF16)


_IN_PROJ_STAGES = 3


def _in_proj_tile(use_rope, *refs):
    if use_rope:
        (x_ref, sc_ref, sh_ref, ng_ref, wa_ref, wb_ref, wr_ref, wab_ref, bab_ref, cos_ref, sin_ref,
         qdf_ref, kif_ref, kef_ref, qdb_ref, kib_ref, keb_ref, dtf_ref, dtb_ref,
         v_ref, ga_ref, u_ref, gb_ref, sa_ref, sb_ref) = refs
    else:
        (x_ref, sc_ref, sh_ref, ng_ref, wa_ref, wb_ref, wr_ref, wab_ref, bab_ref,
         qdf_ref, kif_ref, kef_ref, qdb_ref, kib_ref, keb_ref, dtf_ref, dtb_ref,
         v_ref, ga_ref, u_ref, gb_ref, sa_ref, sb_ref) = refs

    x = x_ref[...]
    xn = x * lax.rsqrt(jnp.mean(x * x, axis=-1, keepdims=True) + EPS)
    h = xn * (ng_ref[...] * (1.0 + sc_ref[...])) + sh_ref[...]
    hb = h.astype(BF16)

    r_t = lax.dot_general(wr_ref[...], hb, (((0,), (1,)), ((), ())), preferred_element_type=F32)
    qk = _mm(hb, wa_ref[:, _OFF_QK:_OFF_QK + 2 * KDIM])
    xg = _mm_ta(r_t.astype(BF16), wab_ref[...]) + bab_ref[...]
    yield
    _project(hb, wa_ref, _OFF_V, VDIM, lambda z: z, v_ref)
    g_all = (jnp.minimum(xg, 0.0) - jnp.log(1.0 + jnp.exp(-jnp.abs(xg)))) * (1.0 / GATE_NORM)

    row = lax.broadcasted_iota(jnp.int32, (CHUNK, 2 * CHUNK), 0)
    col = lax.broadcasted_iota(jnp.int32, (CHUNK, 2 * CHUNK), 1) % CHUNK
    bcs = []
    for direction in range(2):
        g_hi, g_lo = _split_bf16(g_all[:, direction * KDIM:(direction + 1) * KDIM])
        tri = jnp.where((col <= row) if direction == 0 else (col >= row), 1.0, 0.0).astype(BF16)
        bc_chunks = []
        for c in range(CHUNKS_PER_TILE):
            cr = slice(c * CHUNK, (c + 1) * CHUNK)
            bc_chunks.append(_mm(tri, jnp.concatenate([g_hi[cr], g_lo[cr]], axis=0)))
        bcs.append(jnp.concatenate(bc_chunks, axis=0))
    yield

    q = qk[:, :KDIM] * (HEAD_DK ** -0.5)
    k = qk[:, KDIM:]
    if use_rope:
        cos = cos_ref[...]
        sin = sin_ref[...]
        lane = lax.broadcasted_iota(jnp.int32, (TOKEN_TILE, HEAD_DK), 1)
        first_half = (lane // (HEAD_DK // 4)) % 2 == 0

        def rope(t):
            outs = []
            for hh in range(N_HEADS):
                th = t[:, hh * HEAD_DK:(hh + 1) * HEAD_DK]
                partner = jnp.where(first_half,
                                    pltpu.roll(th, HEAD_DK - HEAD_DK // 4, axis=1),
                                    pltpu.roll(th, HEAD_DK // 4, axis=1))
                outs.append(th * cos + partner * sin)
            return jnp.concatenate(outs, axis=1)

        q = rope(q)
        k = rope(k)

    for direction, (qd_ref, ki_ref, ke_ref, dt_ref) in enumerate(
            ((qdf_ref, kif_ref, kef_ref, dtf_ref), (qdb_ref, kib_ref, keb_ref, dtb_ref))):
        bc = bcs[direction]
        edge = CHUNK - 1 if direction == 0 else 0
        bl_rows = bc.reshape(CHUNKS_PER_TILE, CHUNK, KDIM)[:, edge, :]
        bl = jnp.broadcast_to(bl_rows[:, None, :], (CHUNKS_PER_TILE, CHUNK, KDIM)).reshape(TOKEN_TILE, KDIM)
        qd_ref[...] = (q * jnp.exp(bc)).astype(BF16)
        ki_ref[...] = (k * jnp.exp(-bc)).astype(BF16)
        ke_ref[...] = (k * jnp.exp(bl - bc)).astype(BF16)
        padded = jnp.concatenate([bl_rows, jnp.zeros((LANES - CHUNKS_PER_TILE, KDIM), F32)], axis=0)
        dt_ref[...] = jnp.exp(padded.T)

    _project(hb, wa_ref, _OFF_GA, VDIM, jax.nn.silu, ga_ref)
    _project(hb, wb_ref, _OFF_U, FOURIER_DIM, lambda z: z, u_ref)
    _project(hb, wb_ref, _OFF_GB, FOURIER_DIM, jax.nn.silu, gb_ref)
    _project(hb, wb_ref, _OFF_MA, D_MODEL, jax.nn.sigmoid, sa_ref)
    _project(hb, wb_ref, _OFF_MB, D_MODEL, jax.nn.sigmoid, sb_ref)
    yield


def _in_proj(x, scale, shift, norm_g, w_head, w_tail, w_r, w_ab, b_ab, rope):
    B, T, _ = x.shape
    use_rope = rope is not None
    per_batch_mod = scale.shape[0] > 1
    step_rows = IN_TILES_PER_STEP * TOKEN_TILE
    if T < step_rows:
        assert not per_batch_mod and not use_rope and (B * T) % step_rows == 0
        outs = _in_proj(x.reshape(B * T // step_rows, step_rows, D_MODEL), scale, shift, norm_g,
                        w_head, w_tail, w_r, w_ab, b_ab, rope)
        return [o.reshape((B, T // TOKEN_TILE) + o.shape[2:]) if o.ndim == 4 else o.reshape(B, T, o.shape[-1])
                for o in outs]
    nt = T // TOKEN_TILE

    def tok_spec(cols):
        return pl.BlockSpec((None, step_rows, cols), lambda b, t: (b, t, 0))

    mod_spec = pl.BlockSpec((None, 1, D_MODEL), (lambda b, t: (b, 0, 0)) if per_batch_mod else (lambda b, t: (0, 0, 0)))
    in_specs = [
        tok_spec(D_MODEL), mod_spec, mod_spec, _const_spec((1, D_MODEL)),
        _const_spec((D_MODEL, _W_HEAD_COLS)), _const_spec((D_MODEL, _W_TAIL_COLS)),
        _const_spec((D_MODEL, 2 * GATE_RANK)),
        _const_spec((2 * GATE_RANK, 2 * KDIM)), _const_spec((1, 2 * KDIM)),
    ]
    args = [x, scale, shift, norm_g, w_head, w_tail, w_r, w_ab, b_ab]
    if use_rope:
        in_specs += [pl.BlockSpec((step_rows, HEAD_DK), lambda b, t: (t, 0))] * 2
        args += list(rope)

    dt_spec = pl.BlockSpec((None, IN_TILES_PER_STEP, KDIM, LANES), lambda b, t: (b, t, 0, 0))
    tok_bf = lambda cols: jax.ShapeDtypeStruct((B, T, cols), BF16)
    dt_shape = jax.ShapeDtypeStruct((B, nt, KDIM, LANES), F32)
    out_specs = [tok_spec(KDIM)] * 6 + [dt_spec] * 2 + [
        tok_spec(VDIM), tok_spec(VDIM), tok_spec(FOURIER_DIM),
        tok_spec(FOURIER_DIM), tok_spec(D_MODEL), tok_spec(D_MODEL)]
    out_shape = [tok_bf(KDIM)] * 6 + [dt_shape] * 2 + [
        tok_bf(VDIM), tok_bf(VDIM), tok_bf(FOURIER_DIM),
        tok_bf(FOURIER_DIM), tok_bf(D_MODEL), tok_bf(D_MODEL)]
    return pl.pallas_call(
        functools.partial(_in_proj_kernel, use_rope),
        grid=(B, T // step_rows),
        in_specs=in_specs,
        out_specs=out_specs,
        out_shape=out_shape,
        compiler_params=pltpu.CompilerParams(
            dimension_semantics=("arbitrary", "arbitrary"), vmem_limit_bytes=VMEM_LIMIT_BYTES),
        name="in_proj_rope" if use_rope else "in_proj",
    )(*args)


def _gla_kernel(heads, n_tiles, has_init, seqs, *refs):
    gain_pos = 10 + (2 if has_init else 0)
    for bb in range(seqs):
        _gla_seq(heads, n_tiles, has_init, *[r if i == gain_pos else r.at[bb] for i, r in enumerate(refs)])


def _gla_seq(heads, n_tiles, has_init, *refs):
    (qdf_ref, kif_ref, kef_ref, qdb_ref, kib_ref, keb_ref, dtf_ref, dtb_ref, v_ref, ga_ref) = refs[:10]
    refs = refs[10:]
    if has_init:
        s0f_ref, s0b_ref = refs[:2]
        refs = refs[2:]
    gn_ref, og_ref, sf_ref, sb_ref, state, d_state, of_acc, ob_acc = refs
    if has_init:
        state[0] = s0f_ref[...]
        state[1] = s0b_ref[...]
    else:
        state[...] = jnp.zeros_like(state)
    dir_refs = ((qdf_ref, kif_ref, kef_ref, dtf_ref, of_acc), (qdb_ref, kib_ref, keb_ref, dtb_ref, ob_acc))

    def tile_body(finalize, j, carry):
        row = lax.broadcasted_iota(jnp.int32, (TOKEN_TILE, TOKEN_TILE), 0)
        col = lax.broadcasted_iota(jnp.int32, (TOKEN_TILE, TOKEN_TILE), 1)
        same_chunk = (row // CHUNK) == (col // CHUNK)
        causal = (same_chunk & (col <= row), same_chunk & (col >= row))
        tiles = (j, n_tiles - 1 - j)
        bases = tuple(pl.multiple_of(t * TOKEN_TILE, TOKEN_TILE) for t in tiles)

        scores = {}
        for direction in range(2):
            qd_ref, ki_ref, ke_ref, _, o_acc = dir_refs[direction]
            trows = pl.ds(bases[direction], TOKEN_TILE)
            for hh in range(heads):
                kc = slice(hh * HEAD_DK, (hh + 1) * HEAD_DK)
                vc = slice(hh * HEAD_DV, (hh + 1) * HEAD_DV)
                scores[direction, hh] = jnp.where(
                    causal[direction], _mm_tb(qd_ref[trows, kc], ki_ref[trows, kc]), 0.0).astype(BF16)
                for c in range(CHUNKS_PER_TILE):
                    rows = pl.ds(bases[direction] + c * CHUNK, CHUNK)
                    d_state[direction, hh, c] = _mm_ta(ke_ref[rows, kc], v_ref[rows, vc])

        steps = [(step, direction) for step in range(CHUNKS_PER_TILE) for direction in range(2)]
        if finalize[0] != finalize[1]:
            steps.sort(key=lambda sd: finalize[sd[1]])
        for step, direction in steps:
            qd_ref, _, _, dt_ref, o_acc = dir_refs[direction]
            other_acc = dir_refs[1 - direction][4]
            c = step if direction == 0 else CHUNKS_PER_TILE - 1 - step
            rows = pl.ds(bases[direction] + c * CHUNK, CHUNK)
            for hh in range(heads):
                kc = slice(hh * HEAD_DK, (hh + 1) * HEAD_DK)
                vc = slice(hh * HEAD_DV, (hh + 1) * HEAD_DV)
                s = state[direction, hh]
                blk = (c * CHUNK) // LANES
                sc = scores[direction, hh][c * CHUNK:(c + 1) * CHUNK, blk * LANES:(blk + 1) * LANES]
                v_blk = v_ref[pl.ds(bases[direction] + blk * LANES, LANES), vc]
                o = _mm(jnp.concatenate([qd_ref[rows, kc], sc], axis=1),
                        jnp.concatenate([s.astype(BF16), v_blk], axis=0))
                if finalize[direction]:
                    o = o + other_acc[rows, vc]
                    on = o * lax.rsqrt(jnp.mean(o * o, axis=-1, keepdims=True) + EPS) * gn_ref[...]
                    og_ref[rows, vc] = (on * ga_ref[rows, vc].astype(F32)).astype(BF16)
                else:
                    o_acc[rows, vc] = o
                dec = dt_ref[tiles[direction], kc, c:c + 1]
                state[direction, hh] = dec * s + d_state[direction, hh, c]
        return carry

    first_half = n_tiles // 2
    unroll = 4 if first_half % 4 == 0 else 1
    lax.fori_loop(0, first_half, functools.partial(tile_body, (False, False)), 0, unroll=unroll)
    if n_tiles % 2:
        tile_body((False, True), first_half, 0)
    lax.fori_loop(n_tiles - first_half, n_tiles, functools.partial(tile_body, (True, True)), 0, unroll=unroll)
    sf_ref[...] = state[0]
    sb_ref[...] = state[1]


def _gla(qdf, kif, kef, qdb, kib, keb, dtf, dtb, v, ga, init_states, gla_norm_g, heads):
    B, T, _ = v.shape
    nt = T // TOKEN_TILE
    hg = N_HEADS // heads
    has_init = init_states is not None
    seqs = GLA_SHORT_SEQS_PER_STEP if nt == 1 else 1
    assert B % seqs == 0
    k_spec = pl.BlockSpec((seqs, T, heads * HEAD_DK), lambda b, h: (b, 0, h))
    v_spec = pl.BlockSpec((seqs, T, heads * HEAD_DV), lambda b, h: (b, 0, h))
    dt_spec = pl.BlockSpec((seqs, nt, heads * HEAD_DK, LANES), lambda b, h: (b, 0, h, 0))
    s_spec = pl.BlockSpec((seqs, heads, HEAD_DK, HEAD_DV), lambda b, h: (b, h, 0, 0))
    s_shape = jax.ShapeDtypeStruct((B, N_HEADS, HEAD_DK, HEAD_DV), F32)
    return pl.pallas_call(
        functools.partial(_gla_kernel, heads, nt, has_init, seqs),
        grid=(B // seqs, hg),
        in_specs=([k_spec] * 6 + [dt_spec] * 2 + [v_spec, v_spec] + [s_spec] * (2 if has_init else 0)
                  + [_const_spec((1, HEAD_DV))]),
        out_specs=[v_spec, s_spec, s_spec],
        out_shape=[jax.ShapeDtypeStruct((B, T, VDIM), BF16), s_shape, s_shape],
        scratch_shapes=[pltpu.VMEM((seqs, 2, heads, HEAD_DK, HEAD_DV), F32),
                        pltpu.VMEM((seqs, 2, heads, CHUNKS_PER_TILE, HEAD_DK, HEAD_DV), F32),
                        pltpu.VMEM((seqs, T, heads * HEAD_DV), F32),
                        pltpu.VMEM((seqs, T, heads * HEAD_DV), F32)],
        compiler_params=pltpu.CompilerParams(
            dimension_semantics=("arbitrary", "arbitrary"), vmem_limit_bytes=VMEM_LIMIT_BYTES),
        name="gla",
    )(qdf, kif, kef, qdb, kib, keb, dtf, dtb, v, ga, *(init_states or ()), gla_norm_g)


def _mix_weights_kernel(tab_ref, wf_ref, o_ref):
    for g in range(N_FGROUPS):
        o_ref[g] = jnp.dot(tab_ref[...], wf_ref[g], preferred_element_type=F32,
                           precision=lax.Precision.HIGHEST).astype(BF16)


def _mix_weights(wf):
    n = np.arange(FGROUP_CH)
    ang = 2.0 * np.pi * ((n[:, None] * n[None, :]) % FGROUP_CH) / FGROUP_CH
    tab = jnp.asarray(np.concatenate([np.cos(ang), np.sin(ang)], axis=0) / np.sqrt(FGROUP_CH), F32)
    return pl.pallas_call(
        _mix_weights_kernel,
        out_shape=jax.ShapeDtypeStruct((N_FGROUPS, 2 * FGROUP_CH, FGROUP_CH), BF16),
        name="mix_weights",
    )(tab, wf)


def _mix_and_gate(fp, fq, mix_ref, gb, store):
    fp = fp.astype(BF16)
    fq = fq.astype(BF16)
    for g in range(N_FGROUPS):
        cols = slice(g * FGROUP_CH, (g + 1) * FGROUP_CH)
        z = _mm(jnp.concatenate([fp[:, cols], fq[:, cols]], axis=1), mix_ref[g])
        store(cols, (z * gb(cols).astype(F32)).astype(BF16))


def _fourier_kernel(seqs, ct_ref, st_ref, u_ref, gb_ref, mix_ref, o_ref):
    for bb in range(seqs):
        u = u_ref[bb]

        def store(cols, val, bb=bb):
            o_ref[bb, :, cols] = val

        _mix_and_gate(_mm(ct_ref[...], u), _mm(st_ref[...], u), mix_ref,
                      lambda cols, bb=bb: gb_ref[bb, :, cols], store)


def _fourier(ct, stn, u, gb, mix):
    B, T, _ = u.shape
    tf = min(T, FOURIER_ROWS_PER_STEP)
    seqs = FOURIER_SHORT_SEQS_PER_STEP if T < FOURIER_ROWS_PER_STEP else 1
    assert B % seqs == 0
    tab_spec = pl.BlockSpec((tf, T), lambda b, t: (t, 0))
    seq_spec = pl.BlockSpec((seqs, T, FOURIER_DIM), lambda b, t: (b, 0, 0))
    tile_spec = pl.BlockSpec((seqs, tf, FOURIER_DIM), lambda b, t: (b, t, 0))
    return pl.pallas_call(
        functools.partial(_fourier_kernel, seqs),
        grid=(B // seqs, T // tf),
        in_specs=[tab_spec, tab_spec, seq_spec, tile_spec,
                  _const_spec((N_FGROUPS, 2 * FGROUP_CH, FGROUP_CH))],
        out_specs=tile_spec,
        out_shape=jax.ShapeDtypeStruct((B, T, FOURIER_DIM), BF16),
        compiler_params=pltpu.CompilerParams(
            dimension_semantics=("arbitrary", "arbitrary"), vmem_limit_bytes=VMEM_LIMIT_BYTES),
        name="fourier",
    )(ct, stn, u, gb, mix)


def _fourier_split_kernel(ce_ref, se_ref, co_ref, so_ref, u_ref, gb_ref, mix_ref, o_ref, wide, u_eo):
    half = u_eo.shape[1]
    for g in range(N_FGROUPS):
        cols = slice(g * FGROUP_CH, (g + 1) * FGROUP_CH)
        wide[g] = u_ref[:, cols].astype(F32)
        for parity in range(2):
            u_eo[parity, :, cols] = wide[g, pl.ds(parity, half, stride=2), :].astype(BF16)

    for kb in range(half // FOURIER_ROWS_PER_STEP):
        rows = slice(kb * FOURIER_ROWS_PER_STEP, (kb + 1) * FOURIER_ROWS_PER_STEP)
        ep, eq = _mm(ce_ref[rows, :], u_eo[0]), _mm(se_ref[rows, :], u_eo[0])
        op, oq = _mm(co_ref[rows, :], u_eo[1]), _mm(so_ref[rows, :], u_eo[1])
        for upper, (fp, fq) in enumerate(((ep + op, eq + oq), (ep - op, eq - oq))):
            def store(cols, val, upper=upper):
                o_ref[upper, rows, cols] = val

            _mix_and_gate(fp, fq, mix_ref, lambda cols, upper=upper: gb_ref[upper, rows, cols], store)


def _fourier_split(u, gb, mix):
    B, T, _ = u.shape
    half = T // 2
    ce, sen = _time_dft_tables(T, half, 2 * np.arange(half))
    co, son = _time_dft_tables(T, half, 2 * np.arange(half) + 1)
    seq_spec = pl.BlockSpec((None, T, FOURIER_DIM), lambda b: (b, 0, 0))
    halves_spec = pl.BlockSpec((None, 2, half, FOURIER_DIM), lambda b: (b, 0, 0, 0))
    out = pl.pallas_call(
        _fourier_split_kernel,
        grid=(B,),
        in_specs=[_const_spec((half, half))] * 4 + [seq_spec, halves_spec,
                                                     _const_spec((N_FGROUPS, 2 * FGROUP_CH, FGROUP_CH))],
        out_specs=halves_spec,
        out_shape=jax.ShapeDtypeStruct((B, 2, half, FOURIER_DIM), BF16),
        scratch_shapes=[pltpu.VMEM((N_FGROUPS, T, FGROUP_CH), F32),
                        pltpu.VMEM((2, half, FOURIER_DIM), BF16)],
        compiler_params=pltpu.CompilerParams(
            dimension_semantics=("arbitrary",), vmem_limit_bytes=VMEM_LIMIT_BYTES),
        name="fourier_split",
    )(ce, sen, co, son, u, gb.reshape(B, 2, half, FOURIER_DIM), mix)
    return out.reshape(B, T, FOURIER_DIM)


def _out_proj_kernel(og_ref, fg_ref, sa_ref, sb_ref, x_ref, gate_ref, fng_ref, wpa_ref, wpb_ref, wo_ref, y_ref):
    for i in range(OUT_TILES_PER_STEP):
        rows = slice(i * TOKEN_TILE, (i + 1) * TOKEN_TILE)
        ya = _mm(og_ref[rows, :], wpa_ref[...])
        yb = _mm(fg_ref[rows, :], wpb_ref[...])
        merged = sa_ref[rows, :].astype(F32) * ya + sb_ref[rows, :].astype(F32) * yb
        xo = x_ref[rows, :] + gate_ref[...] * _mm(merged.astype(BF16), wo_ref[...])
        y_ref[rows, :] = xo * lax.rsqrt(jnp.mean(xo * xo, axis=-1, keepdims=True) + EPS) * fng_ref[...]


def _out_proj(og, fg, sa, sb, x, gate, final_norm_g, wpa, wpb, wo):
    B, T, _ = x.shape
    per_batch_mod = gate.shape[0] > 1
    step_rows = OUT_TILES_PER_STEP * TOKEN_TILE
    if T < step_rows:
        assert not per_batch_mod and (B * T) % step_rows == 0
        fold = lambda a: a.reshape(B * T // step_rows, step_rows, a.shape[-1])
        y = _out_proj(fold(og), fold(fg), fold(sa), fold(sb), fold(x), gate, final_norm_g, wpa, wpb, wo)
        return y.reshape(B, T, D_MODEL)

    def tok_spec(cols):
        return pl.BlockSpec((None, step_rows, cols), lambda b, t: (b, t, 0))

    mod_spec = pl.BlockSpec((None, 1, D_MODEL), (lambda b, t: (b, 0, 0)) if per_batch_mod else (lambda b, t: (0, 0, 0)))
    return pl.pallas_call(
        _out_proj_kernel,
        grid=(B, T // step_rows),
        in_specs=[tok_spec(VDIM), tok_spec(FOURIER_DIM), tok_spec(D_MODEL), tok_spec(D_MODEL), tok_spec(D_MODEL),
                  mod_spec, _const_spec((1, D_MODEL)),
                  _const_spec((VDIM, D_MODEL)), _const_spec((FOURIER_DIM, D_MODEL)), _const_spec((D_MODEL, D_MODEL))],
        out_specs=tok_spec(D_MODEL),
        out_shape=jax.ShapeDtypeStruct((B, T, D_MODEL), F32),
        compiler_params=pltpu.CompilerParams(
            dimension_semantics=("arbitrary", "arbitrary"), vmem_limit_bytes=VMEM_LIMIT_BYTES),
        name="out_proj",
    )(og, fg, sa, sb, x, gate, final_norm_g, wpa, wpb, wo)


def _time_dft_tables(T, n_rows, positions):
    lo = 32
    hi = n_rows // lo
    n = np.asarray(positions)
    ang_hi = 2.0 * np.pi * (((np.arange(hi)[:, None] * lo) * n[None, :]) % T) / T
    ang_lo = 2.0 * np.pi * ((np.arange(lo)[:, None] * n[None, :]) % T) / T
    scale = 1.0 / np.sqrt(T)
    ch = jnp.asarray(np.cos(ang_hi) * scale, F32)[:, None, :]
    sh = jnp.asarray(np.sin(ang_hi) * scale, F32)[:, None, :]
    cl = jnp.asarray(np.cos(ang_lo), F32)[None, :, :]
    sl = jnp.asarray(np.sin(ang_lo), F32)[None, :, :]
    ct = (ch * cl - sh * sl).astype(BF16).reshape(n_rows, n.size)
    stn = (-(sh * cl + ch * sl)).astype(BF16).reshape(n_rows, n.size)
    return ct, stn


def _rope_tables(T):
    rows = T // GRID_W
    r = np.repeat(np.arange(rows), GRID_W).astype(np.float64)
    c = np.tile(np.arange(GRID_W), rows).astype(np.float64)
    n_freq = HEAD_DK // 4
    freqs = ROPE_BASE ** (-np.arange(n_freq, dtype=np.float64) / n_freq)
    ang_r = r[:, None] * freqs
    ang_c = c[:, None] * freqs
    cos = np.concatenate([np.cos(ang_r), np.cos(ang_r), np.cos(ang_c), np.cos(ang_c)], axis=-1)
    sin = np.concatenate([-np.sin(ang_r), np.sin(ang_r), -np.sin(ang_c), np.sin(ang_c)], axis=-1)
    return jnp.asarray(cos, F32), jnp.asarray(sin, F32)


def _path(x, scale, shift, gate, init_states, rope, heads, wts):
    T = x.shape[1]
    (qdf, kif, kef, qdb, kib, keb, dtf, dtb, v, ga, u, gb, sa, sb) = _in_proj(
        x, scale, shift, wts["norm_g"], wts["w_head"], wts["w_tail"], wts["w_r"], wts["w_ab"], wts["b_ab"], rope)
    og, sf, sbw = _gla(qdf, kif, kef, qdb, kib, keb, dtf, dtb, v, ga, init_states, wts["gla_norm_g"], heads)
    if T >= 4 * FOURIER_ROWS_PER_STEP:
        fg = _fourier_split(u, gb, wts["four_mix"])
    else:
        ct, stn = _time_dft_tables(T, T, np.arange(T))
        fg = _fourier(ct, stn, u, gb, wts["four_mix"])
    y = _out_proj(og, fg, sa, sb, x, gate, wts["final_norm_g"], wts["w_proj_a"], wts["w_proj_b"], wts["w_out"])
    return y, sf, sbw


def kernel(x_prompt, x_sample, state_gla_fwd, state_gla_bwd, c, c_ctx, w_ada, b_ada, norm_g, w_in,
           w_alpha_fwd, b_alpha_fwd, w_alpha_bwd, b_alpha_bwd, gla_norm_g, w_four, w_proj_a, w_proj_b,
           w_out, final_norm_g):
    depth = w_in.shape[0]
    assert depth == 1, "single trunk layer"
    bs = x_sample.shape[0]

    n_cond = bs + 1
    cond_rows = -(-n_cond // 8) * 8
    cond = jnp.concatenate([c, c_ctx[None, :], jnp.zeros((cond_rows - n_cond, D_MODEL), F32)], axis=0)
    ada = _ada(cond, w_ada[0], b_ada[0][None, :])
    shift, scale, gate = ada[:, :D_MODEL], ada[:, D_MODEL:2 * D_MODEL], ada[:, 2 * D_MODEL:]
    mod = lambda m, lo, hi: m[lo:hi][:, None, :]

    w_head, w_r, w_tail = _split_w_in(w_in[0])
    w_ab = jnp.zeros((2 * GATE_RANK, 2 * KDIM), F32)
    w_ab = w_ab.at[:GATE_RANK, :KDIM].set(w_alpha_fwd[0]).at[GATE_RANK:, KDIM:].set(w_alpha_bwd[0])
    wts = dict(
        norm_g=norm_g[0][None, :], w_head=w_head, w_tail=w_tail, w_r=w_r, w_ab=w_ab.astype(BF16),
        b_ab=jnp.concatenate([b_alpha_fwd[0], b_alpha_bwd[0]])[None, :],
        gla_norm_g=gla_norm_g[0][None, :], four_mix=_mix_weights(w_four[0]),
        w_proj_a=w_proj_a[0].astype(BF16), w_proj_b=w_proj_b[0].astype(BF16), w_out=w_out[0].astype(BF16),
        final_norm_g=final_norm_g[None, :])

    y_prompt, sf, sb = _path(x_prompt, mod(scale, bs, bs + 1), mod(shift, bs, bs + 1), mod(gate, bs, bs + 1),
                             None, None, N_HEADS, wts)
    y_sample, _, _ = _path(x_sample, mod(scale, 0, bs), mod(shift, 0, bs), mod(gate, 0, bs),
                           (state_gla_fwd[:, 0], state_gla_bwd[:, 0]), _rope_tables(x_sample.shape[1]), 2, wts)
    return (y_prompt, y_sample, sf[:, None].astype(x_prompt.dtype), sb[:, None].astype(x_prompt.dtype))
```

```python
import functools

import numpy as np
import jax
import jax.numpy as jnp
from jax import lax
from jax.experimental import pallas as pl
from jax.experimental.pallas import tpu as pltpu

F32 = jnp.float32
BF16 = jnp.bfloat16

D_MODEL = 1024
N_HEADS = 4
HEAD_DK = 128
HEAD_DV = 256
KDIM = N_HEADS * HEAD_DK
VDIM = N_HEADS * HEAD_DV
GATE_RANK = 16
GATE_NORM = 16.0
CHUNK = 64
N_FGROUPS = 4
FGROUP_CH = 128
FOURIER_DIM = N_FGROUPS * FGROUP_CH
GRID_W = 64
ROPE_BASE = 10000.0
EPS = 1e-6

LANES = 128
MXU_COLS = 256
TOKEN_TILE = 256
CHUNKS_PER_TILE = TOKEN_TILE // CHUNK
IN_TILES_PER_STEP = 2
OUT_TILES_PER_STEP = 4
GLA_SHORT_SEQS_PER_STEP = 4
GLA_LONG_SEQ_HEADS_PER_STEP = 2
GLA_TILE_UNROLL = 4
FOURIER_ROWS_PER_STEP = 512
FOURIER_SHORT_SEQS_PER_STEP = 8
FOURIER_SPLIT_MIN_LEN = 2048
DFT_TABLE_ROW_FACTOR = 32
VMEM_LIMIT_BYTES = 56 * 1024 * 1024

_OFF_QK = 0
_OFF_V = _OFF_QK + 2 * KDIM
_OFF_GA = _OFF_V + VDIM
_W_HEAD_COLS = _OFF_GA + VDIM
_OFF_U = 0
_OFF_GB = _OFF_U + FOURIER_DIM
_OFF_MA = _OFF_GB + FOURIER_DIM
_OFF_MB = _OFF_MA + D_MODEL
_W_TAIL_COLS = _OFF_MB + D_MODEL


def _mm(a, b):
    return jnp.dot(a, b, preferred_element_type=F32)


def _mm_ta(a, b):
    return lax.dot_general(a, b, (((0,), (0,)), ((), ())), preferred_element_type=F32)


def _mm_tb(a, b):
    return lax.dot_general(a, b, (((1,), (1,)), ((), ())), preferred_element_type=F32)


def _mm_ta_tb(a, b):
    return lax.dot_general(a, b, (((0,), (1,)), ((), ())), preferred_element_type=F32)


def _split_bf16(x):
    hi = x.astype(BF16)
    lo = (x - hi.astype(F32)).astype(BF16)
    return hi, lo


def _const_spec(shape):
    nd = len(shape)
    return pl.BlockSpec(shape, lambda *_: (0,) * nd)


def _ada_kernel(c_ref, w_ref, b_ref, o_ref):
    c = c_ref[...]
    s = c * jax.nn.sigmoid(c)
    o_ref[...] = _mm(s.astype(BF16), w_ref[...].astype(BF16)) + b_ref[...]


def _ada(cond, w, b):
    rows = cond.shape[0]
    n_blocks = w.shape[1] // D_MODEL
    return pl.pallas_call(
        _ada_kernel,
        grid=(n_blocks,),
        in_specs=[pl.BlockSpec((rows, D_MODEL), lambda n: (0, 0)),
                  pl.BlockSpec((D_MODEL, D_MODEL), lambda n: (0, n)),
                  pl.BlockSpec((1, D_MODEL), lambda n: (0, n))],
        out_specs=pl.BlockSpec((rows, D_MODEL), lambda n: (0, n)),
        out_shape=jax.ShapeDtypeStruct((rows, n_blocks * D_MODEL), F32),
        compiler_params=pltpu.CompilerParams(
            dimension_semantics=("arbitrary",), vmem_limit_bytes=VMEM_LIMIT_BYTES),
        name="ada",
    )(cond, w, b)


def _in_proj_kernel(use_rope, *refs):
    n_shared = 8
    n_in = 1 + n_shared + (2 if use_rope else 0)
    dt_slots = (n_in + 6, n_in + 7)
    tiles = []
    for i in range(IN_TILES_PER_STEP):
        rows = pl.ds(i * TOKEN_TILE, TOKEN_TILE)
        tile_refs = []
        for idx, ref in enumerate(refs):
            if 1 <= idx <= n_shared:
                tile_refs.append(ref)
            elif idx in dt_slots:
                tile_refs.append(ref.at[i])
            else:
                tile_refs.append(ref.at[rows])
        tiles.append(_in_proj_tile(use_rope, *tile_refs))
    for _ in range(_IN_PROJ_STAGES):
        for tile in tiles:
            next(tile)


def _project(hb, w_ref, off, width, act, out_ref):
    for n in range(0, width, MXU_COLS):
        z = _mm(hb, w_ref[:, off + n:off + n + MXU_COLS])
        out_ref[:, n:n + MXU_COLS] = act(z).astype(BF16)


_IN_PROJ_STAGES = 3


def _in_proj_tile(use_rope, *refs):
    if use_rope:
        (x_ref, sc_ref, sh_ref, ng_ref, wa_ref, wb_ref, wr_ref, wab_ref, bab_ref, cos_ref, sin_ref,
         qdf_ref, kif_ref, kef_ref, qdb_ref, kib_ref, keb_ref, dtf_ref, dtb_ref,
         v_ref, ga_ref, u_ref, gb_ref, sa_ref, sb_ref) = refs
    else:
        (x_ref, sc_ref, sh_ref, ng_ref, wa_ref, wb_ref, wr_ref, wab_ref, bab_ref,
         qdf_ref, kif_ref, kef_ref, qdb_ref, kib_ref, keb_ref, dtf_ref, dtb_ref,
         v_ref, ga_ref, u_ref, gb_ref, sa_ref, sb_ref) = refs

    x = x_ref[...]
    xn = x * lax.rsqrt(jnp.mean(x * x, axis=-1, keepdims=True) + EPS)
    h = xn * (ng_ref[...] * (1.0 + sc_ref[...])) + sh_ref[...]
    hb = h.astype(BF16)

    r_t = _mm_ta_tb(wr_ref[...], hb)
    qk = _mm(hb, wa_ref[:, _OFF_QK:_OFF_QK + 2 * KDIM])
    xg = _mm_ta(r_t.astype(BF16), wab_ref[...]) + bab_ref[...]
    yield

    _project(hb, wa_ref, _OFF_V, VDIM, lambda z: z, v_ref)
    g_all = (jnp.minimum(xg, 0.0) - jnp.log(1.0 + jnp.exp(-jnp.abs(xg)))) * (1.0 / GATE_NORM)
    row = lax.broadcasted_iota(jnp.int32, (CHUNK, 2 * CHUNK), 0)
    col = lax.broadcasted_iota(jnp.int32, (CHUNK, 2 * CHUNK), 1) % CHUNK
    bcs = []
    for direction in range(2):
        g_hi, g_lo = _split_bf16(g_all[:, direction * KDIM:(direction + 1) * KDIM])
        tri = jnp.where((col <= row) if direction == 0 else (col >= row), 1.0, 0.0).astype(BF16)
        bc_chunks = []
        for c in range(CHUNKS_PER_TILE):
            cr = slice(c * CHUNK, (c + 1) * CHUNK)
            bc_chunks.append(_mm(tri, jnp.concatenate([g_hi[cr], g_lo[cr]], axis=0)))
        bcs.append(jnp.concatenate(bc_chunks, axis=0))
    yield

    q = qk[:, :KDIM] * (HEAD_DK ** -0.5)
    k = qk[:, KDIM:]
    if use_rope:
        cos = cos_ref[...]
        sin = sin_ref[...]
        lane = lax.broadcasted_iota(jnp.int32, (TOKEN_TILE, HEAD_DK), 1)
        first_half = (lane // (HEAD_DK // 4)) % 2 == 0

        def rope(t):
            outs = []
            for hh in range(N_HEADS):
                th = t[:, hh * HEAD_DK:(hh + 1) * HEAD_DK]
                partner = jnp.where(first_half,
                                    pltpu.roll(th, HEAD_DK - HEAD_DK // 4, axis=1),
                                    pltpu.roll(th, HEAD_DK // 4, axis=1))
                outs.append(th * cos + partner * sin)
            return jnp.concatenate(outs, axis=1)

        q = rope(q)
        k = rope(k)

    for direction, (qd_ref, ki_ref, ke_ref, dt_ref) in enumerate(
            ((qdf_ref, kif_ref, kef_ref, dtf_ref), (qdb_ref, kib_ref, keb_ref, dtb_ref))):
        bc = bcs[direction]
        edge = CHUNK - 1 if direction == 0 else 0
        bl_rows = bc.reshape(CHUNKS_PER_TILE, CHUNK, KDIM)[:, edge, :]
        bl = jnp.broadcast_to(bl_rows[:, None, :], (CHUNKS_PER_TILE, CHUNK, KDIM)).reshape(TOKEN_TILE, KDIM)
        qd_ref[...] = (q * jnp.exp(bc)).astype(BF16)
        ki_ref[...] = (k * jnp.exp(-bc)).astype(BF16)
        ke_ref[...] = (k * jnp.exp(bl - bc)).astype(BF16)
        padded = jnp.concatenate([bl_rows, jnp.zeros((LANES - CHUNKS_PER_TILE, KDIM), F32)], axis=0)
        dt_ref[...] = jnp.exp(padded.T)

    _project(hb, wa_ref, _OFF_GA, VDIM, jax.nn.silu, ga_ref)
    _project(hb, wb_ref, _OFF_U, FOURIER_DIM, lambda z: z, u_ref)
    _project(hb, wb_ref, _OFF_GB, FOURIER_DIM, jax.nn.silu, gb_ref)
    _project(hb, wb_ref, _OFF_MA, D_MODEL, jax.nn.sigmoid, sa_ref)
    _project(hb, wb_ref, _OFF_MB, D_MODEL, jax.nn.sigmoid, sb_ref)
    yield


def _in_proj(x, scale, shift, norm_g, w_head, w_tail, w_r, w_ab, b_ab, rope):
    B, T, _ = x.shape
    use_rope = rope is not None
    per_batch_mod = scale.shape[0] > 1
    step_rows = IN_TILES_PER_STEP * TOKEN_TILE
    if T < step_rows:
        assert not per_batch_mod and not use_rope and (B * T) % step_rows == 0
        outs = _in_proj(x.reshape(B * T // step_rows, step_rows, D_MODEL), scale, shift, norm_g,
                        w_head, w_tail, w_r, w_ab, b_ab, rope)
        return [o.reshape((B, T // TOKEN_TILE) + o.shape[2:]) if o.ndim == 4 else o.reshape(B, T, o.shape[-1])
                for o in outs]
    nt = T // TOKEN_TILE

    def tok_spec(cols):
        return pl.BlockSpec((None, step_rows, cols), lambda b, t: (b, t, 0))

    mod_spec = pl.BlockSpec((None, 1, D_MODEL), (lambda b, t: (b, 0, 0)) if per_batch_mod else (lambda b, t: (0, 0, 0)))
    in_specs = [
        tok_spec(D_MODEL), mod_spec, mod_spec, _const_spec((1, D_MODEL)),
        _const_spec((D_MODEL, _W_HEAD_COLS)), _const_spec((D_MODEL, _W_TAIL_COLS)),
        _const_spec((D_MODEL, 2 * GATE_RANK)),
        _const_spec((2 * GATE_RANK, 2 * KDIM)), _const_spec((1, 2 * KDIM)),
    ]
    args = [x, scale, shift, norm_g, w_head, w_tail, w_r, w_ab, b_ab]
    if use_rope:
        in_specs += [pl.BlockSpec((step_rows, HEAD_DK), lambda b, t: (t, 0))] * 2
        args += list(rope)

    dt_spec = pl.BlockSpec((None, IN_TILES_PER_STEP, KDIM, LANES), lambda b, t: (b, t, 0, 0))
    tok_bf = lambda cols: jax.ShapeDtypeStruct((B, T, cols), BF16)
    dt_shape = jax.ShapeDtypeStruct((B, nt, KDIM, LANES), F32)
    out_specs = [tok_spec(KDIM)] * 6 + [dt_spec] * 2 + [
        tok_spec(VDIM), tok_spec(VDIM), tok_spec(FOURIER_DIM),
        tok_spec(FOURIER_DIM), tok_spec(D_MODEL), tok_spec(D_MODEL)]
    out_shape = [tok_bf(KDIM)] * 6 + [dt_shape] * 2 + [
        tok_bf(VDIM), tok_bf(VDIM), tok_bf(FOURIER_DIM),
        tok_bf(FOURIER_DIM), tok_bf(D_MODEL), tok_bf(D_MODEL)]
    return pl.pallas_call(
        functools.partial(_in_proj_kernel, use_rope),
        grid=(B, T // step_rows),
        in_specs=in_specs,
        out_specs=out_specs,
        out_shape=out_shape,
        compiler_params=pltpu.CompilerParams(
            dimension_semantics=("arbitrary", "arbitrary"), vmem_limit_bytes=VMEM_LIMIT_BYTES),
        name="in_proj_rope" if use_rope else "in_proj",
    )(*args)


def _gla_kernel(heads, n_tiles, has_init, seqs, *refs):
    gain_pos = 10 + (2 if has_init else 0)
    for bb in range(seqs):
        _gla_seq(heads, n_tiles, has_init, *[r if i == gain_pos else r.at[bb] for i, r in enumerate(refs)])


def _gla_seq(heads, n_tiles, has_init, *refs):
    (qdf_ref, kif_ref, kef_ref, qdb_ref, kib_ref, keb_ref, dtf_ref, dtb_ref, v_ref, ga_ref) = refs[:10]
    refs = refs[10:]
    if has_init:
        s0f_ref, s0b_ref = refs[:2]
        refs = refs[2:]
    gn_ref, og_ref, sf_ref, sb_ref, state, d_state, of_acc, ob_acc = refs
    if has_init:
        state[0] = s0f_ref[...]
        state[1] = s0b_ref[...]
    else:
        state[...] = jnp.zeros_like(state)
    dir_refs = ((qdf_ref, kif_ref, kef_ref, dtf_ref, of_acc), (qdb_ref, kib_ref, keb_ref, dtb_ref, ob_acc))

    def tile_body(finalize, j, carry):
        row = lax.broadcasted_iota(jnp.int32, (TOKEN_TILE, TOKEN_TILE), 0)
        col = lax.broadcasted_iota(jnp.int32, (TOKEN_TILE, TOKEN_TILE), 1)
        same_chunk = (row // CHUNK) == (col // CHUNK)
        causal = (same_chunk & (col <= row), same_chunk & (col >= row))
        tiles = (j, n_tiles - 1 - j)
        bases = tuple(pl.multiple_of(t * TOKEN_TILE, TOKEN_TILE) for t in tiles)

        scores = {}
        for direction in range(2):
            qd_ref, ki_ref, ke_ref, _, o_acc = dir_refs[direction]
            trows = pl.ds(bases[direction], TOKEN_TILE)
            for hh in range(heads):
                kc = slice(hh * HEAD_DK, (hh + 1) * HEAD_DK)
                vc = slice(hh * HEAD_DV, (hh + 1) * HEAD_DV)
                scores[direction, hh] = jnp.where(
                    causal[direction], _mm_tb(qd_ref[trows, kc], ki_ref[trows, kc]), 0.0).astype(BF16)
                for c in range(CHUNKS_PER_TILE):
                    rows = pl.ds(bases[direction] + c * CHUNK, CHUNK)
                    d_state[direction, hh, c] = _mm_ta(ke_ref[rows, kc], v_ref[rows, vc])

        steps = [(step, direction) for step in range(CHUNKS_PER_TILE) for direction in range(2)]
        if finalize[0] != finalize[1]:
            steps.sort(key=lambda sd: finalize[sd[1]])
        for step, direction in steps:
            qd_ref, _, _, dt_ref, o_acc = dir_refs[direction]
            other_acc = dir_refs[1 - direction][4]
            c = step if direction == 0 else CHUNKS_PER_TILE - 1 - step
            rows = pl.ds(bases[direction] + c * CHUNK, CHUNK)
            for hh in range(heads):
                kc = slice(hh * HEAD_DK, (hh + 1) * HEAD_DK)
                vc = slice(hh * HEAD_DV, (hh + 1) * HEAD_DV)
                s = state[direction, hh]
                blk = (c * CHUNK) // LANES
                sc = scores[direction, hh][c * CHUNK:(c + 1) * CHUNK, blk * LANES:(blk + 1) * LANES]
                v_blk = v_ref[pl.ds(bases[direction] + blk * LANES, LANES), vc]
                o = _mm(jnp.concatenate([qd_ref[rows, kc], sc], axis=1),
                        jnp.concatenate([s.astype(BF16), v_blk], axis=0))
                if finalize[direction]:
                    o = o + other_acc[rows, vc]
                    on = o * lax.rsqrt(jnp.mean(o * o, axis=-1, keepdims=True) + EPS) * gn_ref[...]
                    og_ref[rows, vc] = (on * ga_ref[rows, vc].astype(F32)).astype(BF16)
                else:
                    o_acc[rows, vc] = o
                dec = dt_ref[tiles[direction], kc, c:c + 1]
                state[direction, hh] = dec * s + d_state[direction, hh, c]
        return carry

    first_half = n_tiles // 2
    unroll = GLA_TILE_UNROLL if first_half % GLA_TILE_UNROLL == 0 else 1
    lax.fori_loop(0, first_half, functools.partial(tile_body, (False, False)), 0, unroll=unroll)
    if n_tiles % 2:
        tile_body((False, True), first_half, 0)
    lax.fori_loop(n_tiles - first_half, n_tiles, functools.partial(tile_body, (True, True)), 0, unroll=unroll)
    sf_ref[...] = state[0]
    sb_ref[...] = state[1]


def _gla(qdf, kif, kef, qdb, kib, keb, dtf, dtb, v, ga, init_states, gla_norm_g):
    B, T, _ = v.shape
    nt = T // TOKEN_TILE
    has_init = init_states is not None
    seqs = GLA_SHORT_SEQS_PER_STEP if nt == 1 else 1
    assert B % seqs == 0
    heads = N_HEADS if nt == 1 else GLA_LONG_SEQ_HEADS_PER_STEP
    hg = N_HEADS // heads
    k_spec = pl.BlockSpec((seqs, T, heads * HEAD_DK), lambda b, h: (b, 0, h))
    v_spec = pl.BlockSpec((seqs, T, heads * HEAD_DV), lambda b, h: (b, 0, h))
    dt_spec = pl.BlockSpec((seqs, nt, heads * HEAD_DK, LANES), lambda b, h: (b, 0, h, 0))
    s_spec = pl.BlockSpec((seqs, heads, HEAD_DK, HEAD_DV), lambda b, h: (b, h, 0, 0))
    s_shape = jax.ShapeDtypeStruct((B, N_HEADS, HEAD_DK, HEAD_DV), F32)
    return pl.pallas_call(
        functools.partial(_gla_kernel, heads, nt, has_init, seqs),
        grid=(B // seqs, hg),
        in_specs=([k_spec] * 6 + [dt_spec] * 2 + [v_spec, v_spec] + [s_spec] * (2 if has_init else 0)
                  + [_const_spec((1, HEAD_DV))]),
        out_specs=[v_spec, s_spec, s_spec],
        out_shape=[jax.ShapeDtypeStruct((B, T, VDIM), BF16), s_shape, s_shape],
        scratch_shapes=[pltpu.VMEM((seqs, 2, heads, HEAD_DK, HEAD_DV), F32),
                        pltpu.VMEM((seqs, 2, heads, CHUNKS_PER_TILE, HEAD_DK, HEAD_DV), F32),
                        pltpu.VMEM((seqs, T, heads * HEAD_DV), F32),
                        pltpu.VMEM((seqs, T, heads * HEAD_DV), F32)],
        compiler_params=pltpu.CompilerParams(
            dimension_semantics=("arbitrary", "arbitrary"), vmem_limit_bytes=VMEM_LIMIT_BYTES),
        name="gla",
    )(qdf, kif, kef, qdb, kib, keb, dtf, dtb, v, ga, *(init_states or ()), gla_norm_g)


def _mix_weights_kernel(tab_ref, wf_ref, o_ref):
    for g in range(N_FGROUPS):
        o_ref[g] = jnp.dot(tab_ref[...], wf_ref[g], preferred_element_type=F32,
                           precision=lax.Precision.HIGHEST).astype(BF16)


def _mix_weights(wf):
    n = np.arange(FGROUP_CH)
    ang = 2.0 * np.pi * ((n[:, None] * n[None, :]) % FGROUP_CH) / FGROUP_CH
    tab = jnp.asarray(np.concatenate([np.cos(ang), np.sin(ang)], axis=0) / np.sqrt(FGROUP_CH), F32)
    return pl.pallas_call(
        _mix_weights_kernel,
        out_shape=jax.ShapeDtypeStruct((N_FGROUPS, 2 * FGROUP_CH, FGROUP_CH), BF16),
        name="mix_weights",
    )(tab, wf)


def _mix_and_gate(fp, fq, mix_ref, gb, store):
    fp = fp.astype(BF16)
    fq = fq.astype(BF16)
    for g in range(N_FGROUPS):
        cols = slice(g * FGROUP_CH, (g + 1) * FGROUP_CH)
        z = _mm(jnp.concatenate([fp[:, cols], fq[:, cols]], axis=1), mix_ref[g])
        store(cols, (z * gb(cols).astype(F32)).astype(BF16))


def _fourier_kernel(seqs, ct_ref, st_ref, u_ref, gb_ref, mix_ref, o_ref):
    for bb in range(seqs):
        u = u_ref[bb]

        def store(cols, val, bb=bb):
            o_ref[bb, :, cols] = val

        _mix_and_gate(_mm(ct_ref[...], u), _mm(st_ref[...], u), mix_ref,
                      lambda cols, bb=bb: gb_ref[bb, :, cols], store)


def _fourier(ct, stn, u, gb, mix):
    B, T, _ = u.shape
    tf = min(T, FOURIER_ROWS_PER_STEP)
    seqs = FOURIER_SHORT_SEQS_PER_STEP if T < FOURIER_ROWS_PER_STEP else 1
    assert B % seqs == 0
    tab_spec = pl.BlockSpec((tf, T), lambda b, t: (t, 0))
    seq_spec = pl.BlockSpec((seqs, T, FOURIER_DIM), lambda b, t: (b, 0, 0))
    tile_spec = pl.BlockSpec((seqs, tf, FOURIER_DIM), lambda b, t: (b, t, 0))
    return pl.pallas_call(
        functools.partial(_fourier_kernel, seqs),
        grid=(B // seqs, T // tf),
        in_specs=[tab_spec, tab_spec, seq_spec, tile_spec,
                  _const_spec((N_FGROUPS, 2 * FGROUP_CH, FGROUP_CH))],
        out_specs=tile_spec,
        out_shape=jax.ShapeDtypeStruct((B, T, FOURIER_DIM), BF16),
        compiler_params=pltpu.CompilerParams(
            dimension_semantics=("arbitrary", "arbitrary"), vmem_limit_bytes=VMEM_LIMIT_BYTES),
        name="fourier",
    )(ct, stn, u, gb, mix)


def _fourier_split_kernel(ce_ref, se_ref, co_ref, so_ref, u_ref, gb_ref, mix_ref, o_ref, wide, u_eo):
    half = u_eo.shape[1]
    for g in range(N_FGROUPS):
        cols = slice(g * FGROUP_CH, (g + 1) * FGROUP_CH)
        wide[g] = u_ref[:, cols].astype(F32)
        for parity in range(2):
            u_eo[parity, :, cols] = wide[g, pl.ds(parity, half, stride=2), :].astype(BF16)

    for kb in range(half // FOURIER_ROWS_PER_STEP):
        rows = slice(kb * FOURIER_ROWS_PER_STEP, (kb + 1) * FOURIER_ROWS_PER_STEP)
        ep, eq = _mm(ce_ref[rows, :], u_eo[0]), _mm(se_ref[rows, :], u_eo[0])
        op, oq = _mm(co_ref[rows, :], u_eo[1]), _mm(so_ref[rows, :], u_eo[1])
        for upper, (fp, fq) in enumerate(((ep + op, eq + oq), (ep - op, eq - oq))):
            def store(cols, val, upper=upper):
                o_ref[upper, rows, cols] = val

            _mix_and_gate(fp, fq, mix_ref, lambda cols, upper=upper: gb_ref[upper, rows, cols], store)


def _fourier_split(u, gb, mix):
    B, T, _ = u.shape
    half = T // 2
    ce, sen = _time_dft_tables(T, half, 2 * np.arange(half))
    co, son = _time_dft_tables(T, half, 2 * np.arange(half) + 1)
    seq_spec = pl.BlockSpec((None, T, FOURIER_DIM), lambda b: (b, 0, 0))
    halves_spec = pl.BlockSpec((None, 2, half, FOURIER_DIM), lambda b: (b, 0, 0, 0))
    out = pl.pallas_call(
        _fourier_split_kernel,
        grid=(B,),
        in_specs=[_const_spec((half, half))] * 4 + [seq_spec, halves_spec,
                                                     _const_spec((N_FGROUPS, 2 * FGROUP_CH, FGROUP_CH))],
        out_specs=halves_spec,
        out_shape=jax.ShapeDtypeStruct((B, 2, half, FOURIER_DIM), BF16),
        scratch_shapes=[pltpu.VMEM((N_FGROUPS, T, FGROUP_CH), F32),
                        pltpu.VMEM((2, half, FOURIER_DIM), BF16)],
        compiler_params=pltpu.CompilerParams(
            dimension_semantics=("arbitrary",), vmem_limit_bytes=VMEM_LIMIT_BYTES),
        name="fourier_split",
    )(ce, sen, co, son, u, gb.reshape(B, 2, half, FOURIER_DIM), mix)
    return out.reshape(B, T, FOURIER_DIM)


def _out_proj_kernel(og_ref, fg_ref, sa_ref, sb_ref, x_ref, gate_ref, fng_ref, wpa_ref, wpb_ref, wo_ref, y_ref):
    for i in range(OUT_TILES_PER_STEP):
        rows = slice(i * TOKEN_TILE, (i + 1) * TOKEN_TILE)
        ya = _mm(og_ref[rows, :], wpa_ref[...])
        yb = _mm(fg_ref[rows, :], wpb_ref[...])
        merged = sa_ref[rows, :].astype(F32) * ya + sb_ref[rows, :].astype(F32) * yb
        xo = x_ref[rows, :] + gate_ref[...] * _mm(merged.astype(BF16), wo_ref[...])
        y_ref[rows, :] = xo * lax.rsqrt(jnp.mean(xo * xo, axis=-1, keepdims=True) + EPS) * fng_ref[...]


def _out_proj(og, fg, sa, sb, x, gate, final_norm_g, wpa, wpb, wo):
    B, T, _ = x.shape
    per_batch_mod = gate.shape[0] > 1
    step_rows = OUT_TILES_PER_STEP * TOKEN_TILE
    if T < step_rows:
        assert not per_batch_mod and (B * T) % step_rows == 0
        fold = lambda a: a.reshape(B * T // step_rows, step_rows, a.shape[-1])
        y = _out_proj(fold(og), fold(fg), fold(sa), fold(sb), fold(x), gate, final_norm_g, wpa, wpb, wo)
        return y.reshape(B, T, D_MODEL)

    def tok_spec(cols):
        return pl.BlockSpec((None, step_rows, cols), lambda b, t: (b, t, 0))

    mod_spec = pl.BlockSpec((None, 1, D_MODEL), (lambda b, t: (b, 0, 0)) if per_batch_mod else (lambda b, t: (0, 0, 0)))
    return pl.pallas_call(
        _out_proj_kernel,
        grid=(B, T // step_rows),
        in_specs=[tok_spec(VDIM), tok_spec(FOURIER_DIM), tok_spec(D_MODEL), tok_spec(D_MODEL), tok_spec(D_MODEL),
                  mod_spec, _const_spec((1, D_MODEL)),
                  _const_spec((VDIM, D_MODEL)), _const_spec((FOURIER_DIM, D_MODEL)), _const_spec((D_MODEL, D_MODEL))],
        out_specs=tok_spec(D_MODEL),
        out_shape=jax.ShapeDtypeStruct((B, T, D_MODEL), F32),
        compiler_params=pltpu.CompilerParams(
            dimension_semantics=("arbitrary", "arbitrary"), vmem_limit_bytes=VMEM_LIMIT_BYTES),
        name="out_proj",
    )(og, fg, sa, sb, x, gate, final_norm_g, wpa, wpb, wo)


def _time_dft_tables(T, n_rows, positions):
    lo = DFT_TABLE_ROW_FACTOR
    hi = n_rows // lo
    n = np.asarray(positions)
    ang_hi = 2.0 * np.pi * (((np.arange(hi)[:, None] * lo) * n[None, :]) % T) / T
    ang_lo = 2.0 * np.pi * ((np.arange(lo)[:, None] * n[None, :]) % T) / T
    scale = 1.0 / np.sqrt(T)
    ch = jnp.asarray(np.cos(ang_hi) * scale, F32)[:, None, :]
    sh = jnp.asarray(np.sin(ang_hi) * scale, F32)[:, None, :]
    cl = jnp.asarray(np.cos(ang_lo), F32)[None, :, :]
    sl = jnp.asarray(np.sin(ang_lo), F32)[None, :, :]
    ct = (ch * cl - sh * sl).astype(BF16).reshape(n_rows, n.size)
    stn = (-(sh * cl + ch * sl)).astype(BF16).reshape(n_rows, n.size)
    return ct, stn


def _rope_tables(T):
    rows = T // GRID_W
    r = np.repeat(np.arange(rows), GRID_W).astype(np.float64)
    c = np.tile(np.arange(GRID_W), rows).astype(np.float64)
    n_freq = HEAD_DK // 4
    freqs = ROPE_BASE ** (-np.arange(n_freq, dtype=np.float64) / n_freq)
    ang_r = r[:, None] * freqs
    ang_c = c[:, None] * freqs
    cos = np.concatenate([np.cos(ang_r), np.cos(ang_r), np.cos(ang_c), np.cos(ang_c)], axis=-1)
    sin = np.concatenate([-np.sin(ang_r), np.sin(ang_r), -np.sin(ang_c), np.sin(ang_c)], axis=-1)
    return jnp.asarray(cos, F32), jnp.asarray(sin, F32)


def _path(x, scale, shift, gate, init_states, rope, wts):
    T = x.shape[1]
    (qdf, kif, kef, qdb, kib, keb, dtf, dtb, v, ga, u, gb, sa, sb) = _in_proj(
        x, scale, shift, wts["norm_g"], wts["w_head"], wts["w_tail"], wts["w_r"], wts["w_ab"], wts["b_ab"], rope)
    og, sf, sbw = _gla(qdf, kif, kef, qdb, kib, keb, dtf, dtb, v, ga, init_states, wts["gla_norm_g"])
    if T >= FOURIER_SPLIT_MIN_LEN:
        fg = _fourier_split(u, gb, wts["four_mix"])
    else:
        ct, stn = _time_dft_tables(T, T, np.arange(T))
        fg = _fourier(ct, stn, u, gb, wts["four_mix"])
    y = _out_proj(og, fg, sa, sb, x, gate, wts["final_norm_g"], wts["w_proj_a"], wts["w_proj_b"], wts["w_out"])
    return y, sf, sbw


def kernel(x_prompt, x_sample, state_gla_fwd, state_gla_bwd, c, c_ctx, w_ada, b_ada, norm_g, w_in,
           w_alpha_fwd, b_alpha_fwd, w_alpha_bwd, b_alpha_bwd, gla_norm_g, w_four, w_proj_a, w_proj_b,
           w_out, final_norm_g):
    depth = w_in.shape[0]
    assert depth == 1, "single trunk layer"
    bs = x_sample.shape[0]

    n_cond = bs + 1
    cond_rows = -(-n_cond // 8) * 8
    cond = jnp.concatenate([c, c_ctx[None, :], jnp.zeros((cond_rows - n_cond, D_MODEL), F32)], axis=0)
    ada = _ada(cond, w_ada[0], b_ada[0][None, :])
    shift, scale, gate = ada[:, :D_MODEL], ada[:, D_MODEL:2 * D_MODEL], ada[:, 2 * D_MODEL:]
    mod = lambda m, lo, hi: m[lo:hi][:, None, :]

    wi = w_in[0]
    o_r = _W_HEAD_COLS
    o_u = o_r + 2 * GATE_RANK
    w_r = wi[:, o_r:o_u].astype(BF16)
    w_ab = jnp.zeros((2 * GATE_RANK, 2 * KDIM), F32)
    w_ab = w_ab.at[:GATE_RANK, :KDIM].set(w_alpha_fwd[0]).at[GATE_RANK:, KDIM:].set(w_alpha_bwd[0])
    wts = dict(
        norm_g=norm_g[0][None, :], w_head=wi.astype(BF16), w_tail=wi[:, o_u:].astype(BF16),
        w_r=w_r, w_ab=w_ab.astype(BF16),
        b_ab=jnp.concatenate([b_alpha_fwd[0], b_alpha_bwd[0]])[None, :],
        gla_norm_g=gla_norm_g[0][None, :], four_mix=_mix_weights(w_four[0]),
        w_proj_a=w_proj_a[0].astype(BF16), w_proj_b=w_proj_b[0].astype(BF16), w_out=w_out[0].astype(BF16),
        final_norm_g=final_norm_g[None, :])

    y_prompt, sf, sb = _path(x_prompt, mod(scale, bs, bs + 1), mod(shift, bs, bs + 1), mod(gate, bs, bs + 1),
                             None, None, wts)
    y_sample, _, _ = _path(x_sample, mod(scale, 0, bs), mod(shift, 0, bs), mod(gate, 0, bs),
                           (state_gla_fwd[:, 0], state_gla_bwd[:, 0]), _rope_tables(x_sample.shape[1]), wts)
    return (y_prompt, y_sample, sf[:, None].astype(x_prompt.dtype), sb[:, None].astype(x_prompt.dtype))
```

```python
import functools

import numpy as np
import jax
import jax.numpy as jnp
from jax import lax
from jax.experimental import pallas as pl
from jax.experimental.pallas import tpu as pltpu

F32 = jnp.float32
BF16 = jnp.bfloat16

D_MODEL = 1024
N_HEADS = 4
HEAD_DK = 128
HEAD_DV = 256
KDIM = N_HEADS * HEAD_DK
VDIM = N_HEADS * HEAD_DV
GATE_RANK = 16
GATE_NORM = 16.0
CHUNK = 64
N_FGROUPS = 4
FGROUP_CH = 128
FOURIER_DIM = N_FGROUPS * FGROUP_CH
GRID_W = 64
ROPE_BASE = 10000.0
EPS = 1e-6
LOG2_E = float(np.log2(np.e))

LANES = 128
MXU_COLS = 256
TOKEN_TILE = 256
CHUNKS_PER_TILE = TOKEN_TILE // CHUNK
IN_TILES_PER_STEP = 2
OUT_TILES_PER_STEP = 4
GLA_SHORT_SEQS_PER_STEP = 4
GLA_LONG_SEQ_HEADS_PER_STEP = 2
GLA_TILE_UNROLL = 4
FOURIER_ROWS_PER_STEP = 512
FOURIER_SHORT_SEQS_PER_STEP = 8
FOURIER_SPLIT_MIN_LEN = 2048
DFT_TABLE_ROW_FACTOR = 32
VMEM_LIMIT_BYTES = 56 * 1024 * 1024

_OFF_QK = 0
_OFF_V = _OFF_QK + 2 * KDIM
_OFF_GA = _OFF_V + VDIM
_W_HEAD_COLS = _OFF_GA + VDIM
_OFF_U = 0
_OFF_GB = _OFF_U + FOURIER_DIM
_OFF_MA = _OFF_GB + FOURIER_DIM
_OFF_MB = _OFF_MA + D_MODEL
_W_TAIL_COLS = _OFF_MB + D_MODEL


def _mm(a, b):
    return jnp.dot(a, b, preferred_element_type=F32)


def _mm_ta(a, b):
    return lax.dot_general(a, b, (((0,), (0,)), ((), ())), preferred_element_type=F32)


def _mm_tb(a, b):
    return lax.dot_general(a, b, (((1,), (1,)), ((), ())), preferred_element_type=F32)


def _mm_ta_tb(a, b):
    return lax.dot_general(a, b, (((0,), (1,)), ((), ())), preferred_element_type=F32)


def _split_bf16(x):
    hi = x.astype(BF16)
    lo = (x - hi.astype(F32)).astype(BF16)
    return hi, lo


def _const_spec(shape):
    nd = len(shape)
    return pl.BlockSpec(shape, lambda *_: (0,) * nd)


def _ada_kernel(c_ref, w_ref, b_ref, o_ref):
    c = c_ref[...]
    s = c * jax.nn.sigmoid(c)
    o_ref[...] = _mm(s.astype(BF16), w_ref[...].astype(BF16)) + b_ref[...]


def _ada(cond, w, b):
    rows = cond.shape[0]
    n_blocks = w.shape[1] // D_MODEL
    return pl.pallas_call(
        _ada_kernel,
        grid=(n_blocks,),
        in_specs=[pl.BlockSpec((rows, D_MODEL), lambda n: (0, 0)),
                  pl.BlockSpec((D_MODEL, D_MODEL), lambda n: (0, n)),
                  pl.BlockSpec((1, D_MODEL), lambda n: (0, n))],
        out_specs=pl.BlockSpec((rows, D_MODEL), lambda n: (0, n)),
        out_shape=jax.ShapeDtypeStruct((rows, n_blocks * D_MODEL), F32),
        compiler_params=pltpu.CompilerParams(
            dimension_semantics=("arbitrary",), vmem_limit_bytes=VMEM_LIMIT_BYTES),
        name="ada",
    )(cond, w, b)


def _in_proj_kernel(use_rope, *refs):
    n_shared = 8
    n_in = 1 + n_shared + (2 if use_rope else 0)
    dt_slots = (n_in + 6, n_in + 7)
    tiles = []
    for i in range(IN_TILES_PER_STEP):
        rows = pl.ds(i * TOKEN_TILE, TOKEN_TILE)
        tile_refs = []
        for idx, ref in enumerate(refs):
            if 1 <= idx <= n_shared:
                tile_refs.append(ref)
            elif idx in dt_slots:
                tile_refs.append(ref.at[i])
            else:
                tile_refs.append(ref.at[rows])
        tiles.append(_in_proj_tile(use_rope, *tile_refs))
    for _ in range(_IN_PROJ_STAGES):
        for tile in tiles:
            next(tile)


def _project(hb, w_ref, off, width, act, out_ref):
    for n in range(0, width, MXU_COLS):
        z = _mm(hb, w_ref[:, off + n:off + n + MXU_COLS])
        out_ref[:, n:n + MXU_COLS] = act(z).astype(BF16)


_IN_PROJ_STAGES = 3


def _in_proj_tile(use_rope, *refs):
    if use_rope:
        (x_ref, sc_ref, sh_ref, ng_ref, wa_ref, wb_ref, wr_ref, wab_ref, bab_ref, cos_ref, sin_ref,
         qdf_ref, kif_ref, kef_ref, qdb_ref, kib_ref, keb_ref, dtf_ref, dtb_ref,
         v_ref, ga_ref, u_ref, gb_ref, sa_ref, sb_ref) = refs
    else:
        (x_ref, sc_ref, sh_ref, ng_ref, wa_ref, wb_ref, wr_ref, wab_ref, bab_ref,
         qdf_ref, kif_ref, kef_ref, qdb_ref, kib_ref, keb_ref, dtf_ref, dtb_ref,
         v_ref, ga_ref, u_ref, gb_ref, sa_ref, sb_ref) = refs

    x = x_ref[...]
    xn = x * lax.rsqrt(jnp.mean(x * x, axis=-1, keepdims=True) + EPS)
    h = xn * (ng_ref[...] * (1.0 + sc_ref[...])) + sh_ref[...]
    hb = h.astype(BF16)

    r_t = _mm_ta_tb(wr_ref[...], hb)
    qk = _mm(hb, wa_ref[:, _OFF_QK:_OFF_QK + 2 * KDIM])
    xg = _mm_ta(r_t.astype(BF16), wab_ref[...]) + bab_ref[...]
    yield

    _project(hb, wa_ref, _OFF_V, VDIM, lambda z: z, v_ref)
    g_all = (jnp.minimum(xg, 0.0) - jnp.log(1.0 + jnp.exp(-jnp.abs(xg)))) * (LOG2_E / GATE_NORM)
    row = lax.broadcasted_iota(jnp.int32, (CHUNK, 2 * CHUNK), 0)
    col = lax.broadcasted_iota(jnp.int32, (CHUNK, 2 * CHUNK), 1) % CHUNK
    bcs = []
    for direction in range(2):
        g_hi, g_lo = _split_bf16(g_all[:, direction * KDIM:(direction + 1) * KDIM])
        tri = jnp.where((col <= row) if direction == 0 else (col >= row), 1.0, 0.0).astype(BF16)
        bc_chunks = []
        for c in range(CHUNKS_PER_TILE):
            cr = slice(c * CHUNK, (c + 1) * CHUNK)
            bc_chunks.append(_mm(tri, jnp.concatenate([g_hi[cr], g_lo[cr]], axis=0)))
        bcs.append(jnp.concatenate(bc_chunks, axis=0))
    yield

    q = qk[:, :KDIM] * (HEAD_DK ** -0.5)
    k = qk[:, KDIM:]
    if use_rope:
        cos = cos_ref[...]
        sin = sin_ref[...]
        lane = lax.broadcasted_iota(jnp.int32, (TOKEN_TILE, HEAD_DK), 1)
        first_half = (lane // (HEAD_DK // 4)) % 2 == 0

        def rope(t):
            outs = []
            for hh in range(N_HEADS):
                th = t[:, hh * HEAD_DK:(hh + 1) * HEAD_DK]
                partner = jnp.where(first_half,
                                    pltpu.roll(th, HEAD_DK - HEAD_DK // 4, axis=1),
                                    pltpu.roll(th, HEAD_DK // 4, axis=1))
                outs.append(th * cos + partner * sin)
            return jnp.concatenate(outs, axis=1)

        q = rope(q)
        k = rope(k)

    for direction, (qd_ref, ki_ref, ke_ref, dt_ref) in enumerate(
            ((qdf_ref, kif_ref, kef_ref, dtf_ref), (qdb_ref, kib_ref, keb_ref, dtb_ref))):
        bc = bcs[direction]
        edge = CHUNK - 1 if direction == 0 else 0
        bl_rows = bc.reshape(CHUNKS_PER_TILE, CHUNK, KDIM)[:, edge, :]
        bl = jnp.broadcast_to(bl_rows[:, None, :], (CHUNKS_PER_TILE, CHUNK, KDIM)).reshape(TOKEN_TILE, KDIM)
        qd_ref[...] = (q * jnp.exp2(bc)).astype(BF16)
        ki_ref[...] = (k * jnp.exp2(-bc)).astype(BF16)
        ke_ref[...] = (k * jnp.exp2(bl - bc)).astype(BF16)
        padded = jnp.concatenate([bl_rows, jnp.zeros((LANES - CHUNKS_PER_TILE, KDIM), F32)], axis=0)
        dt_ref[...] = jnp.exp2(padded.T)

    _project(hb, wa_ref, _OFF_GA, VDIM, jax.nn.silu, ga_ref)
    _project(hb, wb_ref, _OFF_U, FOURIER_DIM, lambda z: z, u_ref)
    _project(hb, wb_ref, _OFF_GB, FOURIER_DIM, jax.nn.silu, gb_ref)
    _project(hb, wb_ref, _OFF_MA, D_MODEL, jax.nn.sigmoid, sa_ref)
    _project(hb, wb_ref, _OFF_MB, D_MODEL, jax.nn.sigmoid, sb_ref)
    yield


def _in_proj(x, scale, shift, norm_g, w_head, w_tail, w_r, w_ab, b_ab, rope):
    B, T, _ = x.shape
    use_rope = rope is not None
    per_batch_mod = scale.shape[0] > 1
    step_rows = IN_TILES_PER_STEP * TOKEN_TILE
    if T < step_rows:
        assert not per_batch_mod and not use_rope and (B * T) % step_rows == 0
        outs = _in_proj(x.reshape(B * T // step_rows, step_rows, D_MODEL), scale, shift, norm_g,
                        w_head, w_tail, w_r, w_ab, b_ab, rope)
        return [o.reshape((B, T // TOKEN_TILE) + o.shape[2:]) if o.ndim == 4 else o.reshape(B, T, o.shape[-1])
                for o in outs]
    nt = T // TOKEN_TILE

    def tok_spec(cols):
        return pl.BlockSpec((None, step_rows, cols), lambda b, t: (b, t, 0))

    mod_spec = pl.BlockSpec((None, 1, D_MODEL), (lambda b, t: (b, 0, 0)) if per_batch_mod else (lambda b, t: (0, 0, 0)))
    in_specs = [
        tok_spec(D_MODEL), mod_spec, mod_spec, _const_spec((1, D_MODEL)),
        _const_spec((D_MODEL, _W_HEAD_COLS)), _const_spec((D_MODEL, _W_TAIL_COLS)),
        _const_spec((D_MODEL, 2 * GATE_RANK)),
        _const_spec((2 * GATE_RANK, 2 * KDIM)), _const_spec((1, 2 * KDIM)),
    ]
    args = [x, scale, shift, norm_g, w_head, w_tail, w_r, w_ab, b_ab]
    if use_rope:
        in_specs += [pl.BlockSpec((step_rows, HEAD_DK), lambda b, t: (t, 0))] * 2
        args += list(rope)

    dt_spec = pl.BlockSpec((None, IN_TILES_PER_STEP, KDIM, LANES), lambda b, t: (b, t, 0, 0))
    tok_bf = lambda cols: jax.ShapeDtypeStruct((B, T, cols), BF16)
    dt_shape = jax.ShapeDtypeStruct((B, nt, KDIM, LANES), F32)
    out_specs = [tok_spec(KDIM)] * 6 + [dt_spec] * 2 + [
        tok_spec(VDIM), tok_spec(VDIM), tok_spec(FOURIER_DIM),
        tok_spec(FOURIER_DIM), tok_spec(D_MODEL), tok_spec(D_MODEL)]
    out_shape = [tok_bf(KDIM)] * 6 + [dt_shape] * 2 + [
        tok_bf(VDIM), tok_bf(VDIM), tok_bf(FOURIER_DIM),
        tok_bf(FOURIER_DIM), tok_bf(D_MODEL), tok_bf(D_MODEL)]
    return pl.pallas_call(
        functools.partial(_in_proj_kernel, use_rope),
        grid=(B, T // step_rows),
        in_specs=in_specs,
        out_specs=out_specs,
        out_shape=out_shape,
        compiler_params=pltpu.CompilerParams(
            dimension_semantics=("arbitrary", "arbitrary"), vmem_limit_bytes=VMEM_LIMIT_BYTES),
        name="in_proj_rope" if use_rope else "in_proj",
    )(*args)


def _gla_kernel(heads, n_tiles, has_init, seqs, *refs):
    gain_pos = 10 + (2 if has_init else 0)
    for bb in range(seqs):
        _gla_seq(heads, n_tiles, has_init, *[r if i == gain_pos else r.at[bb] for i, r in enumerate(refs)])


def _gla_seq(heads, n_tiles, has_init, *refs):
    (qdf_ref, kif_ref, kef_ref, qdb_ref, kib_ref, keb_ref, dtf_ref, dtb_ref, v_ref, ga_ref) = refs[:10]
    refs = refs[10:]
    if has_init:
        s0f_ref, s0b_ref = refs[:2]
        refs = refs[2:]
    gn_ref, og_ref, sf_ref, sb_ref, state, d_state, of_acc, ob_acc = refs
    if has_init:
        state[0] = s0f_ref[...]
        state[1] = s0b_ref[...]
    else:
        state[...] = jnp.zeros_like(state)
    dir_refs = ((qdf_ref, kif_ref, kef_ref, dtf_ref, of_acc), (qdb_ref, kib_ref, keb_ref, dtb_ref, ob_acc))

    def tile_body(finalize, j, carry):
        row = lax.broadcasted_iota(jnp.int32, (TOKEN_TILE, TOKEN_TILE), 0)
        col = lax.broadcasted_iota(jnp.int32, (TOKEN_TILE, TOKEN_TILE), 1)
        same_chunk = (row // CHUNK) == (col // CHUNK)
        causal = (same_chunk & (col <= row), same_chunk & (col >= row))
        tiles = (j, n_tiles - 1 - j)
        bases = tuple(pl.multiple_of(t * TOKEN_TILE, TOKEN_TILE) for t in tiles)

        scores = {}
        for direction in range(2):
            qd_ref, ki_ref, ke_ref, _, o_acc = dir_refs[direction]
            trows = pl.ds(bases[direction], TOKEN_TILE)
            for hh in range(heads):
                kc = slice(hh * HEAD_DK, (hh + 1) * HEAD_DK)
                vc = slice(hh * HEAD_DV, (hh + 1) * HEAD_DV)
                scores[direction, hh] = jnp.where(
                    causal[direction], _mm_tb(qd_ref[trows, kc], ki_ref[trows, kc]), 0.0).astype(BF16)
                for c in range(CHUNKS_PER_TILE):
                    rows = pl.ds(bases[direction] + c * CHUNK, CHUNK)
                    d_state[direction, hh, c] = _mm_ta(ke_ref[rows, kc], v_ref[rows, vc])

        steps = [(step, direction) for step in range(CHUNKS_PER_TILE) for direction in range(2)]
        if finalize[0] != finalize[1]:
            steps.sort(key=lambda sd: finalize[sd[1]])
        for step, direction in steps:
            qd_ref, _, _, dt_ref, o_acc = dir_refs[direction]
            other_acc = dir_refs[1 - direction][4]
            c = step if direction == 0 else CHUNKS_PER_TILE - 1 - step
            rows = pl.ds(bases[direction] + c * CHUNK, CHUNK)
            for hh in range(heads):
                kc = slice(hh * HEAD_DK, (hh + 1) * HEAD_DK)
                vc = slice(hh * HEAD_DV, (hh + 1) * HEAD_DV)
                s = state[direction, hh]
                blk = (c * CHUNK) // LANES
                sc = scores[direction, hh][c * CHUNK:(c + 1) * CHUNK, blk * LANES:(blk + 1) * LANES]
                v_blk = v_ref[pl.ds(bases[direction] + blk * LANES, LANES), vc]
                o = _mm(jnp.concatenate([qd_ref[rows, kc], sc], axis=1),
                        jnp.concatenate([s.astype(BF16), v_blk], axis=0))
                if finalize[direction]:
                    o = o + other_acc[rows, vc]
                    on = o * lax.rsqrt(jnp.mean(o * o, axis=-1, keepdims=True) + EPS) * gn_ref[...]
                    og_ref[rows, vc] = (on * ga_ref[rows, vc].astype(F32)).astype(BF16)
                else:
                    o_acc[rows, vc] = o
                dec = dt_ref[tiles[direction], kc, c:c + 1]
                state[direction, hh] = dec * s + d_state[direction, hh, c]
        return carry

    first_half = n_tiles // 2
    unroll = GLA_TILE_UNROLL if first_half % GLA_TILE_UNROLL == 0 else 1
    lax.fori_loop(0, first_half, functools.partial(tile_body, (False, False)), 0, unroll=unroll)
    if n_tiles % 2:
        tile_body((False, True), first_half, 0)
    lax.fori_loop(n_tiles - first_half, n_tiles, functools.partial(tile_body, (True, True)), 0, unroll=unroll)
    sf_ref[...] = state[0]
    sb_ref[...] = state[1]


def _gla(qdf, kif, kef, qdb, kib, keb, dtf, dtb, v, ga, init_states, gla_norm_g):
    B, T, _ = v.shape
    nt = T // TOKEN_TILE
    has_init = init_states is not None
    seqs = GLA_SHORT_SEQS_PER_STEP if nt == 1 else 1
    assert B % seqs == 0
    heads = N_HEADS if nt == 1 else GLA_LONG_SEQ_HEADS_PER_STEP
    hg = N_HEADS // heads
    k_spec = pl.BlockSpec((seqs, T, heads * HEAD_DK), lambda b, h: (b, 0, h))
    v_spec = pl.BlockSpec((seqs, T, heads * HEAD_DV), lambda b, h: (b, 0, h))
    dt_spec = pl.BlockSpec((seqs, nt, heads * HEAD_DK, LANES), lambda b, h: (b, 0, h, 0))
    s_spec = pl.BlockSpec((seqs, heads, HEAD_DK, HEAD_DV), lambda b, h: (b, h, 0, 0))
    s_shape = jax.ShapeDtypeStruct((B, N_HEADS, HEAD_DK, HEAD_DV), F32)
    return pl.pallas_call(
        functools.partial(_gla_kernel, heads, nt, has_init, seqs),
        grid=(B // seqs, hg),
        in_specs=([k_spec] * 6 + [dt_spec] * 2 + [v_spec, v_spec] + [s_spec] * (2 if has_init else 0)
                  + [_const_spec((1, HEAD_DV))]),
        out_specs=[v_spec, s_spec, s_spec],
        out_shape=[jax.ShapeDtypeStruct((B, T, VDIM), BF16), s_shape, s_shape],
        scratch_shapes=[pltpu.VMEM((seqs, 2, heads, HEAD_DK, HEAD_DV), F32),
                        pltpu.VMEM((seqs, 2, heads, CHUNKS_PER_TILE, HEAD_DK, HEAD_DV), F32),
                        pltpu.VMEM((seqs, T, heads * HEAD_DV), F32),
                        pltpu.VMEM((seqs, T, heads * HEAD_DV), F32)],
        compiler_params=pltpu.CompilerParams(
            dimension_semantics=("arbitrary", "arbitrary"), vmem_limit_bytes=VMEM_LIMIT_BYTES),
        name="gla",
    )(qdf, kif, kef, qdb, kib, keb, dtf, dtb, v, ga, *(init_states or ()), gla_norm_g)


def _mix_weights_kernel(tab_ref, wf_ref, o_ref):
    for g in range(N_FGROUPS):
        o_ref[g] = jnp.dot(tab_ref[...], wf_ref[g], preferred_element_type=F32,
                           precision=lax.Precision.HIGHEST).astype(BF16)


def _mix_weights(wf):
    n = np.arange(FGROUP_CH)
    ang = 2.0 * np.pi * ((n[:, None] * n[None, :]) % FGROUP_CH) / FGROUP_CH
    tab = jnp.asarray(np.concatenate([np.cos(ang), np.sin(ang)], axis=0) / np.sqrt(FGROUP_CH), F32)
    return pl.pallas_call(
        _mix_weights_kernel,
        out_shape=jax.ShapeDtypeStruct((N_FGROUPS, 2 * FGROUP_CH, FGROUP_CH), BF16),
        name="mix_weights",
    )(tab, wf)


def _mix_and_gate(fp, fq, mix_ref, gb, store):
    fp = fp.astype(BF16)
    fq = fq.astype(BF16)
    for g in range(N_FGROUPS):
        cols = slice(g * FGROUP_CH, (g + 1) * FGROUP_CH)
        z = _mm(jnp.concatenate([fp[:, cols], fq[:, cols]], axis=1), mix_ref[g])
        store(cols, (z * gb(cols).astype(F32)).astype(BF16))


def _fourier_kernel(seqs, ct_ref, st_ref, u_ref, gb_ref, mix_ref, o_ref):
    for bb in range(seqs):
        u = u_ref[bb]

        def store(cols, val, bb=bb):
            o_ref[bb, :, cols] = val

        _mix_and_gate(_mm(ct_ref[...], u), _mm(st_ref[...], u), mix_ref,
                      lambda cols, bb=bb: gb_ref[bb, :, cols], store)


def _fourier(ct, stn, u, gb, mix):
    B, T, _ = u.shape
    tf = min(T, FOURIER_ROWS_PER_STEP)
    seqs = FOURIER_SHORT_SEQS_PER_STEP if T < FOURIER_ROWS_PER_STEP else 1
    assert B % seqs == 0
    tab_spec = pl.BlockSpec((tf, T), lambda b, t: (t, 0))
    seq_spec = pl.BlockSpec((seqs, T, FOURIER_DIM), lambda b, t: (b, 0, 0))
    tile_spec = pl.BlockSpec((seqs, tf, FOURIER_DIM), lambda b, t: (b, t, 0))
    return pl.pallas_call(
        functools.partial(_fourier_kernel, seqs),
        grid=(B // seqs, T // tf),
        in_specs=[tab_spec, tab_spec, seq_spec, tile_spec,
                  _const_spec((N_FGROUPS, 2 * FGROUP_CH, FGROUP_CH))],
        out_specs=tile_spec,
        out_shape=jax.ShapeDtypeStruct((B, T, FOURIER_DIM), BF16),
        compiler_params=pltpu.CompilerParams(
            dimension_semantics=("arbitrary", "arbitrary"), vmem_limit_bytes=VMEM_LIMIT_BYTES),
        name="fourier",
    )(ct, stn, u, gb, mix)


def _fourier_split_kernel(cee_ref, see_ref, ceo_ref, seo_ref, co_ref, so_ref, u_ref, gb_ref, mix_ref, o_ref,
                          wide, u_odd, u_even):
    half, quarter = u_odd.shape[0], u_even.shape[1]
    for g in range(N_FGROUPS):
        cols = slice(g * FGROUP_CH, (g + 1) * FGROUP_CH)
        wide[g] = u_ref[:, cols].astype(F32)
        u_odd[:, cols] = wide[g, pl.ds(1, half, stride=2), :].astype(BF16)
        for r in range(2):
            u_even[r, :, cols] = wide[g, pl.ds(2 * r, quarter, stride=4), :].astype(BF16)

    eep, eeq = _mm(cee_ref[...], u_even[0]), _mm(see_ref[...], u_even[0])
    eop, eoq = _mm(ceo_ref[...], u_even[1]), _mm(seo_ref[...], u_even[1])
    e_blocks = ((eep + eop, eeq + eoq), (eep - eop, eeq - eoq))
    for kb, (ep, eq) in enumerate(e_blocks):
        rows = slice(kb * quarter, (kb + 1) * quarter)
        op, oq = _mm(co_ref[rows, :], u_odd[...]), _mm(so_ref[rows, :], u_odd[...])
        for upper, (fp, fq) in enumerate(((ep + op, eq + oq), (ep - op, eq - oq))):
            def store(cols, val, upper=upper):
                o_ref[upper, rows, cols] = val

            _mix_and_gate(fp, fq, mix_ref, lambda cols, upper=upper: gb_ref[upper, rows, cols], store)


def _fourier_split(u, gb, mix):
    B, T, _ = u.shape
    half, quarter = T // 2, T // 4
    cee, seen = _time_dft_tables(T, quarter, 4 * np.arange(quarter))
    ceo, seon = _time_dft_tables(T, quarter, 4 * np.arange(quarter) + 2)
    co, son = _time_dft_tables(T, half, 2 * np.arange(half) + 1)
    seq_spec = pl.BlockSpec((None, T, FOURIER_DIM), lambda b: (b, 0, 0))
    halves_spec = pl.BlockSpec((None, 2, half, FOURIER_DIM), lambda b: (b, 0, 0, 0))
    out = pl.pallas_call(
        _fourier_split_kernel,
        grid=(B,),
        in_specs=[_const_spec((quarter, quarter))] * 4 + [_const_spec((half, half))] * 2 + [
            seq_spec, halves_spec, _const_spec((N_FGROUPS, 2 * FGROUP_CH, FGROUP_CH))],
        out_specs=halves_spec,
        out_shape=jax.ShapeDtypeStruct((B, 2, half, FOURIER_DIM), BF16),
        scratch_shapes=[pltpu.VMEM((N_FGROUPS, T, FGROUP_CH), F32),
                        pltpu.VMEM((half, FOURIER_DIM), BF16),
                        pltpu.VMEM((2, quarter, FOURIER_DIM), BF16)],
        compiler_params=pltpu.CompilerParams(
            dimension_semantics=("arbitrary",), vmem_limit_bytes=VMEM_LIMIT_BYTES),
        name="fourier_split",
    )(cee, seen, ceo, seon, co, son, u, gb.reshape(B, 2, half, FOURIER_DIM), mix)
    return out.reshape(B, T, FOURIER_DIM)


def _out_proj_kernel(og_ref, fg_ref, sa_ref, sb_ref, x_ref, gate_ref, fng_ref, wpa_ref, wpb_ref, wo_ref, y_ref):
    for i in range(OUT_TILES_PER_STEP):
        rows = slice(i * TOKEN_TILE, (i + 1) * TOKEN_TILE)
        ya = _mm(og_ref[rows, :], wpa_ref[...])
        yb = _mm(fg_ref[rows, :], wpb_ref[...])
        merged = sa_ref[rows, :].astype(F32) * ya + sb_ref[rows, :].astype(F32) * yb
        xo = x_ref[rows, :] + gate_ref[...] * _mm(merged.astype(BF16), wo_ref[...])
        y_ref[rows, :] = xo * lax.rsqrt(jnp.mean(xo * xo, axis=-1, keepdims=True) + EPS) * fng_ref[...]


def _out_proj(og, fg, sa, sb, x, gate, final_norm_g, wpa, wpb, wo):
    B, T, _ = x.shape
    per_batch_mod = gate.shape[0] > 1
    step_rows = OUT_TILES_PER_STEP * TOKEN_TILE
    if T < step_rows:
        assert not per_batch_mod and (B * T) % step_rows == 0
        fold = lambda a: a.reshape(B * T // step_rows, step_rows, a.shape[-1])
        y = _out_proj(fold(og), fold(fg), fold(sa), fold(sb), fold(x), gate, final_norm_g, wpa, wpb, wo)
        return y.reshape(B, T, D_MODEL)

    def tok_spec(cols):
        return pl.BlockSpec((None, step_rows, cols), lambda b, t: (b, t, 0))

    mod_spec = pl.BlockSpec((None, 1, D_MODEL), (lambda b, t: (b, 0, 0)) if per_batch_mod else (lambda b, t: (0, 0, 0)))
    return pl.pallas_call(
        _out_proj_kernel,
        grid=(B, T // step_rows),
        in_specs=[tok_spec(VDIM), tok_spec(FOURIER_DIM), tok_spec(D_MODEL), tok_spec(D_MODEL), tok_spec(D_MODEL),
                  mod_spec, _const_spec((1, D_MODEL)),
                  _const_spec((VDIM, D_MODEL)), _const_spec((FOURIER_DIM, D_MODEL)), _const_spec((D_MODEL, D_MODEL))],
        out_specs=tok_spec(D_MODEL),
        out_shape=jax.ShapeDtypeStruct((B, T, D_MODEL), F32),
        compiler_params=pltpu.CompilerParams(
            dimension_semantics=("arbitrary", "arbitrary"), vmem_limit_bytes=VMEM_LIMIT_BYTES),
        name="out_proj",
    )(og, fg, sa, sb, x, gate, final_norm_g, wpa, wpb, wo)


def _time_dft_tables(T, n_rows, positions):
    lo = DFT_TABLE_ROW_FACTOR
    hi = n_rows // lo
    n = np.asarray(positions)
    ang_hi = 2.0 * np.pi * (((np.arange(hi)[:, None] * lo) * n[None, :]) % T) / T
    ang_lo = 2.0 * np.pi * ((np.arange(lo)[:, None] * n[None, :]) % T) / T
    scale = 1.0 / np.sqrt(T)
    ch = jnp.asarray(np.cos(ang_hi) * scale, F32)[:, None, :]
    sh = jnp.asarray(np.sin(ang_hi) * scale, F32)[:, None, :]
    cl = jnp.asarray(np.cos(ang_lo), F32)[None, :, :]
    sl = jnp.asarray(np.sin(ang_lo), F32)[None, :, :]
    ct = (ch * cl - sh * sl).astype(BF16).reshape(n_rows, n.size)
    stn = (-(sh * cl + ch * sl)).astype(BF16).reshape(n_rows, n.size)
    return ct, stn


def _rope_tables(T):
    rows = T // GRID_W
    r = np.repeat(np.arange(rows), GRID_W).astype(np.float64)
    c = np.tile(np.arange(GRID_W), rows).astype(np.float64)
    n_freq = HEAD_DK // 4
    freqs = ROPE_BASE ** (-np.arange(n_freq, dtype=np.float64) / n_freq)
    ang_r = r[:, None] * freqs
    ang_c = c[:, None] * freqs
    cos = np.concatenate([np.cos(ang_r), np.cos(ang_r), np.cos(ang_c), np.cos(ang_c)], axis=-1)
    sin = np.concatenate([-np.sin(ang_r), np.sin(ang_r), -np.sin(ang_c), np.sin(ang_c)], axis=-1)
    return jnp.asarray(cos, F32), jnp.asarray(sin, F32)


def _path(x, scale, shift, gate, init_states, rope, wts):
    T = x.shape[1]
    (qdf, kif, kef, qdb, kib, keb, dtf, dtb, v, ga, u, gb, sa, sb) = _in_proj(
        x, scale, shift, wts["norm_g"], wts["w_head"], wts["w_tail"], wts["w_r"], wts["w_ab"], wts["b_ab"], rope)
    og, sf, sbw = _gla(qdf, kif, kef, qdb, kib, keb, dtf, dtb, v, ga, init_states, wts["gla_norm_g"])
    if T >= FOURIER_SPLIT_MIN_LEN:
        fg = _fourier_split(u, gb, wts["four_mix"])
    else:
        ct, stn = _time_dft_tables(T, T, np.arange(T))
        fg = _fourier(ct, stn, u, gb, wts["four_mix"])
    y = _out_proj(og, fg, sa, sb, x, gate, wts["final_norm_g"], wts["w_proj_a"], wts["w_proj_b"], wts["w_out"])
    return y, sf, sbw


def kernel(x_prompt, x_sample, state_gla_fwd, state_gla_bwd, c, c_ctx, w_ada, b_ada, norm_g, w_in,
           w_alpha_fwd, b_alpha_fwd, w_alpha_bwd, b_alpha_bwd, gla_norm_g, w_four, w_proj_a, w_proj_b,
           w_out, final_norm_g):
    depth = w_in.shape[0]
    assert depth == 1, "single trunk layer"
    bs = x_sample.shape[0]

    n_cond = bs + 1
    cond_rows = -(-n_cond // 8) * 8
    cond = jnp.concatenate([c, c_ctx[None, :], jnp.zeros((cond_rows - n_cond, D_MODEL), F32)], axis=0)
    ada = _ada(cond, w_ada[0], b_ada[0][None, :])
    shift, scale, gate = ada[:, :D_MODEL], ada[:, D_MODEL:2 * D_MODEL], ada[:, 2 * D_MODEL:]
    mod = lambda m, lo, hi: m[lo:hi][:, None, :]

    wi = w_in[0]
    o_r = _W_HEAD_COLS
    o_u = o_r + 2 * GATE_RANK
    w_r = wi[:, o_r:o_u].astype(BF16)
    w_ab = jnp.zeros((2 * GATE_RANK, 2 * KDIM), F32)
    w_ab = w_ab.at[:GATE_RANK, :KDIM].set(w_alpha_fwd[0]).at[GATE_RANK:, KDIM:].set(w_alpha_bwd[0])
    wts = dict(
        norm_g=norm_g[0][None, :], w_head=wi.astype(BF16), w_tail=wi[:, o_u:].astype(BF16),
        w_r=w_r, w_ab=w_ab.astype(BF16),
        b_ab=jnp.concatenate([b_alpha_fwd[0], b_alpha_bwd[0]])[None, :],
        gla_norm_g=gla_norm_g[0][None, :], four_mix=_mix_weights(w_four[0]),
        w_proj_a=w_proj_a[0].astype(BF16), w_proj_b=w_proj_b[0].astype(BF16), w_out=w_out[0].astype(BF16),
        final_norm_g=final_norm_g[None, :])

    y_prompt, sf, sb = _path(x_prompt, mod(scale, bs, bs + 1), mod(shift, bs, bs + 1), mod(gate, bs, bs + 1),
                             None, None, wts)
    y_sample, _, _ = _path(x_sample, mod(scale, 0, bs), mod(shift, 0, bs), mod(gate, 0, bs),
                           (state_gla_fwd[:, 0], state_gla_bwd[:, 0]), _rope_tables(x_sample.shape[1]), wts)
    return (y_prompt, y_sample, sf[:, None].astype(x_prompt.dtype), sb[:, None].astype(x_prompt.dtype))
```

```python
import functools

import numpy as np
import jax
import jax.numpy as jnp
from jax import lax
from jax.experimental import pallas as pl
from jax.experimental.pallas import tpu as pltpu

F32 = jnp.float32
BF16 = jnp.bfloat16

D_MODEL = 1024
N_HEADS = 4
HEAD_DK = 128
HEAD_DV = 256
KDIM = N_HEADS * HEAD_DK
VDIM = N_HEADS * HEAD_DV
GATE_RANK = 16
GATE_NORM = 16.0
CHUNK = 64
N_FGROUPS = 4
FGROUP_CH = 128
FOURIER_DIM = N_FGROUPS * FGROUP_CH
GRID_W = 64
ROPE_BASE = 10000.0
EPS = 1e-6
LOG2_E = float(np.log2(np.e))

LANES = 128
SUBLANES = 8
MXU_COLS = 256
TOKEN_TILE = 256
CHUNKS_PER_TILE = TOKEN_TILE // CHUNK
IN_TILES_PER_STEP = 2
OUT_TILES_PER_STEP = 4
GLA_SHORT_SEQS_PER_STEP = 4
GLA_LONG_SEQ_HEADS_PER_STEP = 2
GLA_TILE_UNROLL = 4
FOURIER_ROWS_PER_STEP = 512
FOURIER_SHORT_SEQS_PER_STEP = 8
FOURIER_SPLIT_MIN_LEN = 2048
VMEM_LIMIT_BYTES = 56 * 1024 * 1024

_OFF_QK = 0
_OFF_V = _OFF_QK + 2 * KDIM
_OFF_GA = _OFF_V + VDIM
_W_HEAD_COLS = _OFF_GA + VDIM
_OFF_U = 0
_OFF_GB = _OFF_U + FOURIER_DIM
_OFF_MA = _OFF_GB + FOURIER_DIM
_OFF_MB = _OFF_MA + D_MODEL
_W_TAIL_COLS = _OFF_MB + D_MODEL


def _mm(a, b):
    return jnp.dot(a, b, preferred_element_type=F32)


def _mm_ta(a, b):
    return lax.dot_general(a, b, (((0,), (0,)), ((), ())), preferred_element_type=F32)


def _mm_tb(a, b):
    return lax.dot_general(a, b, (((1,), (1,)), ((), ())), preferred_element_type=F32)


def _mm_ta_tb(a, b):
    return lax.dot_general(a, b, (((0,), (1,)), ((), ())), preferred_element_type=F32)


def _split_bf16(x):
    hi = x.astype(BF16)
    lo = (x - hi.astype(F32)).astype(BF16)
    return hi, lo


def _const_spec(shape):
    nd = len(shape)
    return pl.BlockSpec(shape, lambda *_: (0,) * nd)


def _ada_kernel(c_ref, w_ref, b_ref, o_ref):
    c = c_ref[...]
    s = c * jax.nn.sigmoid(c)
    o_ref[...] = _mm(s.astype(BF16), w_ref[...].astype(BF16)) + b_ref[...]


def _ada(cond, w, b):
    rows = cond.shape[0]
    n_blocks = w.shape[1] // D_MODEL
    return pl.pallas_call(
        _ada_kernel,
        grid=(n_blocks,),
        in_specs=[pl.BlockSpec((rows, D_MODEL), lambda n: (0, 0)),
                  pl.BlockSpec((D_MODEL, D_MODEL), lambda n: (0, n)),
                  pl.BlockSpec((1, D_MODEL), lambda n: (0, n))],
        out_specs=pl.BlockSpec((rows, D_MODEL), lambda n: (0, n)),
        out_shape=jax.ShapeDtypeStruct((rows, n_blocks * D_MODEL), F32),
        compiler_params=pltpu.CompilerParams(
            dimension_semantics=("arbitrary",), vmem_limit_bytes=VMEM_LIMIT_BYTES),
        name="ada",
    )(cond, w, b)


def _in_proj_kernel(use_rope, *refs):
    n_shared = 8
    n_in = 1 + n_shared + (2 if use_rope else 0)
    dt_slots = (n_in + 6, n_in + 7)
    tiles = []
    for i in range(IN_TILES_PER_STEP):
        rows = pl.ds(i * TOKEN_TILE, TOKEN_TILE)
        tile_refs = []
        for idx, ref in enumerate(refs):
            if 1 <= idx <= n_shared:
                tile_refs.append(ref)
            elif idx in dt_slots:
                tile_refs.append(ref.at[i])
            else:
                tile_refs.append(ref.at[rows])
        tiles.append(_in_proj_tile(use_rope, *tile_refs))
    for _ in range(_IN_PROJ_STAGES):
        for tile in tiles:
            next(tile)


def _project(hb, w_ref, off, width, act, out_ref):
    for n in range(0, width, MXU_COLS):
        z = _mm(hb, w_ref[:, off + n:off + n + MXU_COLS])
        out_ref[:, n:n + MXU_COLS] = act(z).astype(BF16)


_IN_PROJ_STAGES = 3


def _in_proj_tile(use_rope, *refs):
    if use_rope:
        (x_ref, sc_ref, sh_ref, ng_ref, wa_ref, wb_ref, wr_ref, wab_ref, bab_ref, cos_ref, sin_ref,
         qdf_ref, kif_ref, kef_ref, qdb_ref, kib_ref, keb_ref, dtf_ref, dtb_ref,
         v_ref, ga_ref, u_ref, gb_ref, sa_ref, sb_ref) = refs
    else:
        (x_ref, sc_ref, sh_ref, ng_ref, wa_ref, wb_ref, wr_ref, wab_ref, bab_ref,
         qdf_ref, kif_ref, kef_ref, qdb_ref, kib_ref, keb_ref, dtf_ref, dtb_ref,
         v_ref, ga_ref, u_ref, gb_ref, sa_ref, sb_ref) = refs

    x = x_ref[...]
    xn = x * lax.rsqrt(jnp.mean(x * x, axis=-1, keepdims=True) + EPS)
    h = xn * (ng_ref[...] * (1.0 + sc_ref[...])) + sh_ref[...]
    hb = h.astype(BF16)

    r_t = _mm_ta_tb(wr_ref[...], hb)
    qk = _mm(hb, wa_ref[:, _OFF_QK:_OFF_QK + 2 * KDIM])
    xg = _mm_ta(r_t.astype(BF16), wab_ref[...]) + bab_ref[...]
    yield

    _project(hb, wa_ref, _OFF_V, VDIM, lambda z: z, v_ref)
    g_all = (jnp.minimum(xg, 0.0) - jnp.log(1.0 + jnp.exp(-jnp.abs(xg)))) * (LOG2_E / GATE_NORM)
    row = lax.broadcasted_iota(jnp.int32, (CHUNK, 2 * CHUNK), 0)
    col = lax.broadcasted_iota(jnp.int32, (CHUNK, 2 * CHUNK), 1) % CHUNK
    bcs = []
    for direction in range(2):
        g_hi, g_lo = _split_bf16(g_all[:, direction * KDIM:(direction + 1) * KDIM])
        tri = jnp.where((col <= row) if direction == 0 else (col >= row), 1.0, 0.0).astype(BF16)
        bc_chunks = []
        for c in range(CHUNKS_PER_TILE):
            cr = slice(c * CHUNK, (c + 1) * CHUNK)
            bc_chunks.append(_mm(tri, jnp.concatenate([g_hi[cr], g_lo[cr]], axis=0)))
        bcs.append(jnp.concatenate(bc_chunks, axis=0))
    yield

    q = qk[:, :KDIM] * (HEAD_DK ** -0.5)
    k = qk[:, KDIM:]
    if use_rope:
        cos = cos_ref[...]
        sin = sin_ref[...]
        lane = lax.broadcasted_iota(jnp.int32, (TOKEN_TILE, HEAD_DK), 1)
        first_half = (lane // (HEAD_DK // 4)) % 2 == 0

        def rope(t):
            outs = []
            for hh in range(N_HEADS):
                th = t[:, hh * HEAD_DK:(hh + 1) * HEAD_DK]
                partner = jnp.where(first_half,
                                    pltpu.roll(th, HEAD_DK - HEAD_DK // 4, axis=1),
                                    pltpu.roll(th, HEAD_DK // 4, axis=1))
                outs.append(th * cos + partner * sin)
            return jnp.concatenate(outs, axis=1)

        q = rope(q)
        k = rope(k)

    for direction, (qd_ref, ki_ref, ke_ref, dt_ref) in enumerate(
            ((qdf_ref, kif_ref, kef_ref, dtf_ref), (qdb_ref, kib_ref, keb_ref, dtb_ref))):
        bc = bcs[direction]
        edge = CHUNK - 1 if direction == 0 else 0
        bl_rows = bc.reshape(CHUNKS_PER_TILE, CHUNK, KDIM)[:, edge, :]
        bl = jnp.broadcast_to(bl_rows[:, None, :], (CHUNKS_PER_TILE, CHUNK, KDIM)).reshape(TOKEN_TILE, KDIM)
        qd_ref[...] = (q * jnp.exp2(bc)).astype(BF16)
        ki_ref[...] = (k * jnp.exp2(-bc)).astype(BF16)
        ke_ref[...] = (k * jnp.exp2(bl - bc)).astype(BF16)
        padded = jnp.concatenate([bl_rows, jnp.zeros((LANES - CHUNKS_PER_TILE, KDIM), F32)], axis=0)
        dt_ref[...] = jnp.exp2(padded.T)

    _project(hb, wa_ref, _OFF_GA, VDIM, jax.nn.silu, ga_ref)
    _project(hb, wb_ref, _OFF_U, FOURIER_DIM, lambda z: z, u_ref)
    _project(hb, wb_ref, _OFF_GB, FOURIER_DIM, jax.nn.silu, gb_ref)
    _project(hb, wb_ref, _OFF_MA, D_MODEL, jax.nn.sigmoid, sa_ref)
    _project(hb, wb_ref, _OFF_MB, D_MODEL, jax.nn.sigmoid, sb_ref)
    yield


def _in_proj(x, scale, shift, norm_g, w_head, w_tail, w_r, w_ab, b_ab, rope):
    B, T, _ = x.shape
    use_rope = rope is not None
    per_batch_mod = scale.shape[0] > 1
    step_rows = IN_TILES_PER_STEP * TOKEN_TILE
    if T < step_rows:
        assert not per_batch_mod and not use_rope and (B * T) % step_rows == 0
        outs = _in_proj(x.reshape(B * T // step_rows, step_rows, D_MODEL), scale, shift, norm_g,
                        w_head, w_tail, w_r, w_ab, b_ab, rope)
        return [o.reshape((B, T // TOKEN_TILE) + o.shape[2:]) if o.ndim == 4 else o.reshape(B, T, o.shape[-1])
                for o in outs]
    nt = T // TOKEN_TILE

    def tok_spec(cols):
        return pl.BlockSpec((None, step_rows, cols), lambda b, t: (b, t, 0))

    mod_spec = pl.BlockSpec((None, 1, D_MODEL), (lambda b, t: (b, 0, 0)) if per_batch_mod else (lambda b, t: (0, 0, 0)))
    in_specs = [
        tok_spec(D_MODEL), mod_spec, mod_spec, _const_spec((1, D_MODEL)),
        _const_spec((D_MODEL, _W_HEAD_COLS)), _const_spec((D_MODEL, _W_TAIL_COLS)),
        _const_spec((D_MODEL, 2 * GATE_RANK)),
        _const_spec((2 * GATE_RANK, 2 * KDIM)), _const_spec((1, 2 * KDIM)),
    ]
    args = [x, scale, shift, norm_g, w_head, w_tail, w_r, w_ab, b_ab]
    if use_rope:
        in_specs += [pl.BlockSpec((step_rows, HEAD_DK), lambda b, t: (t, 0))] * 2
        args += list(rope)

    dt_spec = pl.BlockSpec((None, IN_TILES_PER_STEP, KDIM, LANES), lambda b, t: (b, t, 0, 0))
    tok_bf = lambda cols: jax.ShapeDtypeStruct((B, T, cols), BF16)
    dt_shape = jax.ShapeDtypeStruct((B, nt, KDIM, LANES), F32)
    out_specs = [tok_spec(KDIM)] * 6 + [dt_spec] * 2 + [
        tok_spec(VDIM), tok_spec(VDIM), tok_spec(FOURIER_DIM),
        tok_spec(FOURIER_DIM), tok_spec(D_MODEL), tok_spec(D_MODEL)]
    out_shape = [tok_bf(KDIM)] * 6 + [dt_shape] * 2 + [
        tok_bf(VDIM), tok_bf(VDIM), tok_bf(FOURIER_DIM),
        tok_bf(FOURIER_DIM), tok_bf(D_MODEL), tok_bf(D_MODEL)]
    return pl.pallas_call(
        functools.partial(_in_proj_kernel, use_rope),
        grid=(B, T // step_rows),
        in_specs=in_specs,
        out_specs=out_specs,
        out_shape=out_shape,
        compiler_params=pltpu.CompilerParams(
            dimension_semantics=("arbitrary", "arbitrary"), vmem_limit_bytes=VMEM_LIMIT_BYTES),
        name="in_proj_rope" if use_rope else "in_proj",
    )(*args)


def _gla_kernel(heads, n_tiles, has_init, seqs, *refs):
    gain_pos = 10 + (2 if has_init else 0)
    for bb in range(seqs):
        _gla_seq(heads, n_tiles, has_init, *[r if i == gain_pos else r.at[bb] for i, r in enumerate(refs)])


def _gla_seq(heads, n_tiles, has_init, *refs):
    (qdf_ref, kif_ref, kef_ref, qdb_ref, kib_ref, keb_ref, dtf_ref, dtb_ref, v_ref, ga_ref) = refs[:10]
    refs = refs[10:]
    if has_init:
        s0f_ref, s0b_ref = refs[:2]
        refs = refs[2:]
    gn_ref, og_ref, sf_ref, sb_ref, state, d_state, of_acc, ob_acc = refs
    if has_init:
        state[0] = s0f_ref[...]
        state[1] = s0b_ref[...]
    else:
        state[...] = jnp.zeros_like(state)
    dir_refs = ((qdf_ref, kif_ref, kef_ref, dtf_ref, of_acc), (qdb_ref, kib_ref, keb_ref, dtb_ref, ob_acc))

    def tile_body(finalize, j, carry):
        row = lax.broadcasted_iota(jnp.int32, (TOKEN_TILE, TOKEN_TILE), 0)
        col = lax.broadcasted_iota(jnp.int32, (TOKEN_TILE, TOKEN_TILE), 1)
        same_chunk = (row // CHUNK) == (col // CHUNK)
        causal = (same_chunk & (col <= row), same_chunk & (col >= row))
        tiles = (j, n_tiles - 1 - j)
        bases = tuple(pl.multiple_of(t * TOKEN_TILE, TOKEN_TILE) for t in tiles)

        scores = {}
        for direction in range(2):
            qd_ref, ki_ref, ke_ref, _, o_acc = dir_refs[direction]
            trows = pl.ds(bases[direction], TOKEN_TILE)
            for hh in range(heads):
                kc = slice(hh * HEAD_DK, (hh + 1) * HEAD_DK)
                vc = slice(hh * HEAD_DV, (hh + 1) * HEAD_DV)
                scores[direction, hh] = jnp.where(
                    causal[direction], _mm_tb(qd_ref[trows, kc], ki_ref[trows, kc]), 0.0).astype(BF16)
                for c in range(CHUNKS_PER_TILE):
                    rows = pl.ds(bases[direction] + c * CHUNK, CHUNK)
                    d_state[direction, hh, c] = _mm_ta(ke_ref[rows, kc], v_ref[rows, vc])

        steps = [(step, direction) for step in range(CHUNKS_PER_TILE) for direction in range(2)]
        if finalize[0] != finalize[1]:
            steps.sort(key=lambda sd: finalize[sd[1]])
        for step, direction in steps:
            qd_ref, _, _, dt_ref, o_acc = dir_refs[direction]
            other_acc = dir_refs[1 - direction][4]
            c = step if direction == 0 else CHUNKS_PER_TILE - 1 - step
            rows = pl.ds(bases[direction] + c * CHUNK, CHUNK)
            for hh in range(heads):
                kc = slice(hh * HEAD_DK, (hh + 1) * HEAD_DK)
                vc = slice(hh * HEAD_DV, (hh + 1) * HEAD_DV)
                s = state[direction, hh]
                blk = (c * CHUNK) // LANES
                sc = scores[direction, hh][c * CHUNK:(c + 1) * CHUNK, blk * LANES:(blk + 1) * LANES]
                v_blk = v_ref[pl.ds(bases[direction] + blk * LANES, LANES), vc]
                o = _mm(jnp.concatenate([qd_ref[rows, kc], sc], axis=1),
                        jnp.concatenate([s.astype(BF16), v_blk], axis=0))
                if finalize[direction]:
                    o = o + other_acc[rows, vc]
                    on = o * lax.rsqrt(jnp.mean(o * o, axis=-1, keepdims=True) + EPS) * gn_ref[...]
                    og_ref[rows, vc] = (on * ga_ref[rows, vc].astype(F32)).astype(BF16)
                else:
                    o_acc[rows, vc] = o
                dec = dt_ref[tiles[direction], kc, c:c + 1]
                state[direction, hh] = dec * s + d_state[direction, hh, c]
        return carry

    first_half = n_tiles // 2
    unroll = GLA_TILE_UNROLL if first_half % GLA_TILE_UNROLL == 0 else 1
    lax.fori_loop(0, first_half, functools.partial(tile_body, (False, False)), 0, unroll=unroll)
    if n_tiles % 2:
        tile_body((False, True), first_half, 0)
    lax.fori_loop(n_tiles - first_half, n_tiles, functools.partial(tile_body, (True, True)), 0, unroll=unroll)
    sf_ref[...] = state[0]
    sb_ref[...] = state[1]


def _gla(qdf, kif, kef, qdb, kib, keb, dtf, dtb, v, ga, init_states, gla_norm_g):
    B, T, _ = v.shape
    nt = T // TOKEN_TILE
    has_init = init_states is not None
    seqs = GLA_SHORT_SEQS_PER_STEP if nt == 1 else 1
    assert B % seqs == 0
    heads = N_HEADS if nt == 1 else GLA_LONG_SEQ_HEADS_PER_STEP
    hg = N_HEADS // heads
    k_spec = pl.BlockSpec((seqs, T, heads * HEAD_DK), lambda b, h: (b, 0, h))
    v_spec = pl.BlockSpec((seqs, T, heads * HEAD_DV), lambda b, h: (b, 0, h))
    dt_spec = pl.BlockSpec((seqs, nt, heads * HEAD_DK, LANES), lambda b, h: (b, 0, h, 0))
    s_spec = pl.BlockSpec((seqs, heads, HEAD_DK, HEAD_DV), lambda b, h: (b, h, 0, 0))
    s_shape = jax.ShapeDtypeStruct((B, N_HEADS, HEAD_DK, HEAD_DV), F32)
    return pl.pallas_call(
        functools.partial(_gla_kernel, heads, nt, has_init, seqs),
        grid=(B // seqs, hg),
        in_specs=([k_spec] * 6 + [dt_spec] * 2 + [v_spec, v_spec] + [s_spec] * (2 if has_init else 0)
                  + [_const_spec((1, HEAD_DV))]),
        out_specs=[v_spec, s_spec, s_spec],
        out_shape=[jax.ShapeDtypeStruct((B, T, VDIM), BF16), s_shape, s_shape],
        scratch_shapes=[pltpu.VMEM((seqs, 2, heads, HEAD_DK, HEAD_DV), F32),
                        pltpu.VMEM((seqs, 2, heads, CHUNKS_PER_TILE, HEAD_DK, HEAD_DV), F32),
                        pltpu.VMEM((seqs, T, heads * HEAD_DV), F32),
                        pltpu.VMEM((seqs, T, heads * HEAD_DV), F32)],
        compiler_params=pltpu.CompilerParams(
            dimension_semantics=("arbitrary", "arbitrary"), vmem_limit_bytes=VMEM_LIMIT_BYTES),
        name="gla",
    )(qdf, kif, kef, qdb, kib, keb, dtf, dtb, v, ga, *(init_states or ()), gla_norm_g)


def _mix_weights_kernel(tab_ref, wf_ref, o_ref):
    for g in range(N_FGROUPS):
        o_ref[g] = jnp.dot(tab_ref[...], wf_ref[g], preferred_element_type=F32,
                           precision=lax.Precision.HIGHEST).astype(BF16)


def _mix_weights(wf):
    n = np.arange(FGROUP_CH)
    ang = 2.0 * np.pi * ((n[:, None] * n[None, :]) % FGROUP_CH) / FGROUP_CH
    tab = jnp.asarray(np.concatenate([np.cos(ang), np.sin(ang)], axis=0) / np.sqrt(FGROUP_CH), F32)
    return pl.pallas_call(
        _mix_weights_kernel,
        out_shape=jax.ShapeDtypeStruct((N_FGROUPS, 2 * FGROUP_CH, FGROUP_CH), BF16),
        name="mix_weights",
    )(tab, wf)


def _mix_and_gate(fp, fq, mix_ref, gb, store):
    fp = fp.astype(BF16)
    fq = fq.astype(BF16)
    for g in range(N_FGROUPS):
        cols = slice(g * FGROUP_CH, (g + 1) * FGROUP_CH)
        z = _mm(jnp.concatenate([fp[:, cols], fq[:, cols]], axis=1), mix_ref[g])
        store(cols, (z * gb(cols).astype(F32)).astype(BF16))


def _fourier_kernel(seqs, ct_ref, st_ref, u_ref, gb_ref, mix_ref, o_ref):
    for bb in range(seqs):
        u = u_ref[bb]

        def store(cols, val, bb=bb):
            o_ref[bb, :, cols] = val

        _mix_and_gate(_mm(ct_ref[...], u), _mm(st_ref[...], u), mix_ref,
                      lambda cols, bb=bb: gb_ref[bb, :, cols], store)


def _fourier(ct, stn, u, gb, mix):
    B, T, _ = u.shape
    tf = min(T, FOURIER_ROWS_PER_STEP)
    seqs = FOURIER_SHORT_SEQS_PER_STEP if T < FOURIER_ROWS_PER_STEP else 1
    assert B % seqs == 0
    tab_spec = pl.BlockSpec((tf, T), lambda b, t: (t, 0))
    seq_spec = pl.BlockSpec((seqs, T, FOURIER_DIM), lambda b, t: (b, 0, 0))
    tile_spec = pl.BlockSpec((seqs, tf, FOURIER_DIM), lambda b, t: (b, t, 0))
    return pl.pallas_call(
        functools.partial(_fourier_kernel, seqs),
        grid=(B // seqs, T // tf),
        in_specs=[tab_spec, tab_spec, seq_spec, tile_spec,
                  _const_spec((N_FGROUPS, 2 * FGROUP_CH, FGROUP_CH))],
        out_specs=tile_spec,
        out_shape=jax.ShapeDtypeStruct((B, T, FOURIER_DIM), BF16),
        compiler_params=pltpu.CompilerParams(
            dimension_semantics=("arbitrary", "arbitrary"), vmem_limit_bytes=VMEM_LIMIT_BYTES),
        name="fourier",
    )(ct, stn, u, gb, mix)


def _fourier_split_kernel(cee_ref, see_ref, ceo_ref, seo_ref, co_ref, so_ref, u_ref, gb_ref, mix_ref, o_ref,
                          wide, u_odd, u_even):
    half, quarter = u_odd.shape[0], u_even.shape[1]
    for g in range(N_FGROUPS):
        cols = slice(g * FGROUP_CH, (g + 1) * FGROUP_CH)
        wide[g] = u_ref[:, cols].astype(F32)
        u_odd[:, cols] = wide[g, pl.ds(1, half, stride=2), :].astype(BF16)
        for r in range(2):
            u_even[r, :, cols] = wide[g, pl.ds(2 * r, quarter, stride=4), :].astype(BF16)

    eep, eeq = _mm(cee_ref[...], u_even[0]), _mm(see_ref[...], u_even[0])
    eop, eoq = _mm(ceo_ref[...], u_even[1]), _mm(seo_ref[...], u_even[1])
    e_blocks = ((eep + eop, eeq + eoq), (eep - eop, eeq - eoq))
    for kb, (ep, eq) in enumerate(e_blocks):
        rows = slice(kb * quarter, (kb + 1) * quarter)
        op, oq = _mm(co_ref[rows, :], u_odd[...]), _mm(so_ref[rows, :], u_odd[...])
        for upper, (fp, fq) in enumerate(((ep + op, eq + oq), (ep - op, eq - oq))):
            def store(cols, val, upper=upper):
                o_ref[upper, rows, cols] = val

            _mix_and_gate(fp, fq, mix_ref, lambda cols, upper=upper: gb_ref[upper, rows, cols], store)


def _fourier_split(u, gb, mix):
    B, T, _ = u.shape
    half, quarter = T // 2, T // 4
    cee, seen = _time_dft_tables(T, quarter, 4 * np.arange(quarter))
    ceo, seon = _time_dft_tables(T, quarter, 4 * np.arange(quarter) + 2)
    co, son = _time_dft_tables(T, half, 2 * np.arange(half) + 1)
    seq_spec = pl.BlockSpec((None, T, FOURIER_DIM), lambda b: (b, 0, 0))
    halves_spec = pl.BlockSpec((None, 2, half, FOURIER_DIM), lambda b: (b, 0, 0, 0))
    out = pl.pallas_call(
        _fourier_split_kernel,
        grid=(B,),
        in_specs=[_const_spec((quarter, quarter))] * 4 + [_const_spec((half, half))] * 2 + [
            seq_spec, halves_spec, _const_spec((N_FGROUPS, 2 * FGROUP_CH, FGROUP_CH))],
        out_specs=halves_spec,
        out_shape=jax.ShapeDtypeStruct((B, 2, half, FOURIER_DIM), BF16),
        scratch_shapes=[pltpu.VMEM((N_FGROUPS, T, FGROUP_CH), F32),
                        pltpu.VMEM((half, FOURIER_DIM), BF16),
                        pltpu.VMEM((2, quarter, FOURIER_DIM), BF16)],
        compiler_params=pltpu.CompilerParams(
            dimension_semantics=("arbitrary",), vmem_limit_bytes=VMEM_LIMIT_BYTES),
        name="fourier_split",
    )(cee, seen, ceo, seon, co, son, u, gb.reshape(B, 2, half, FOURIER_DIM), mix)
    return out.reshape(B, T, FOURIER_DIM)


def _out_proj_kernel(og_ref, fg_ref, sa_ref, sb_ref, x_ref, gate_ref, fng_ref, wpa_ref, wpb_ref, wo_ref, y_ref):
    for i in range(OUT_TILES_PER_STEP):
        rows = slice(i * TOKEN_TILE, (i + 1) * TOKEN_TILE)
        ya = _mm(og_ref[rows, :], wpa_ref[...])
        yb = _mm(fg_ref[rows, :], wpb_ref[...])
        merged = sa_ref[rows, :].astype(F32) * ya + sb_ref[rows, :].astype(F32) * yb
        xo = x_ref[rows, :] + gate_ref[...] * _mm(merged.astype(BF16), wo_ref[...])
        y_ref[rows, :] = xo * lax.rsqrt(jnp.mean(xo * xo, axis=-1, keepdims=True) + EPS) * fng_ref[...]


def _out_proj(og, fg, sa, sb, x, gate, final_norm_g, wpa, wpb, wo):
    B, T, _ = x.shape
    per_batch_mod = gate.shape[0] > 1
    step_rows = OUT_TILES_PER_STEP * TOKEN_TILE
    if T < step_rows:
        assert not per_batch_mod and (B * T) % step_rows == 0
        fold = lambda a: a.reshape(B * T // step_rows, step_rows, a.shape[-1])
        y = _out_proj(fold(og), fold(fg), fold(sa), fold(sb), fold(x), gate, final_norm_g, wpa, wpb, wo)
        return y.reshape(B, T, D_MODEL)

    def tok_spec(cols):
        return pl.BlockSpec((None, step_rows, cols), lambda b, t: (b, t, 0))

    mod_spec = pl.BlockSpec((None, 1, D_MODEL), (lambda b, t: (b, 0, 0)) if per_batch_mod else (lambda b, t: (0, 0, 0)))
    return pl.pallas_call(
        _out_proj_kernel,
        grid=(B, T // step_rows),
        in_specs=[tok_spec(VDIM), tok_spec(FOURIER_DIM), tok_spec(D_MODEL), tok_spec(D_MODEL), tok_spec(D_MODEL),
                  mod_spec, _const_spec((1, D_MODEL)),
                  _const_spec((VDIM, D_MODEL)), _const_spec((FOURIER_DIM, D_MODEL)), _const_spec((D_MODEL, D_MODEL))],
        out_specs=tok_spec(D_MODEL),
        out_shape=jax.ShapeDtypeStruct((B, T, D_MODEL), F32),
        compiler_params=pltpu.CompilerParams(
            dimension_semantics=("arbitrary", "arbitrary"), vmem_limit_bytes=VMEM_LIMIT_BYTES),
        name="out_proj",
    )(og, fg, sa, sb, x, gate, final_norm_g, wpa, wpb, wo)


def _time_dft_tables(T, n_rows, positions):
    n = np.asarray(positions)
    ang = 2.0 * np.pi * ((np.arange(n_rows)[:, None] * n[None, :]) % T) / T
    scale = 1.0 / np.sqrt(T)
    return jnp.asarray(np.cos(ang) * scale, F32).astype(BF16), jnp.asarray(-np.sin(ang) * scale, F32).astype(BF16)


def _rope_tables(T):
    rows = T // GRID_W
    r = np.repeat(np.arange(rows), GRID_W).astype(np.float64)
    c = np.tile(np.arange(GRID_W), rows).astype(np.float64)
    n_freq = HEAD_DK // 4
    freqs = ROPE_BASE ** (-np.arange(n_freq, dtype=np.float64) / n_freq)
    ang_r = r[:, None] * freqs
    ang_c = c[:, None] * freqs
    cos = np.concatenate([np.cos(ang_r), np.cos(ang_r), np.cos(ang_c), np.cos(ang_c)], axis=-1)
    sin = np.concatenate([-np.sin(ang_r), np.sin(ang_r), -np.sin(ang_c), np.sin(ang_c)], axis=-1)
    return jnp.asarray(cos, F32), jnp.asarray(sin, F32)


def _path(x, scale, shift, gate, init_states, rope, wts):
    T = x.shape[1]
    (qdf, kif, kef, qdb, kib, keb, dtf, dtb, v, ga, u, gb, sa, sb) = _in_proj(
        x, scale, shift, wts["norm_g"], wts["w_head"], wts["w_tail"], wts["w_r"], wts["w_ab"], wts["b_ab"], rope)
    og, sf, sbw = _gla(qdf, kif, kef, qdb, kib, keb, dtf, dtb, v, ga, init_states, wts["gla_norm_g"])
    if T >= FOURIER_SPLIT_MIN_LEN:
        fg = _fourier_split(u, gb, wts["four_mix"])
    else:
        ct, stn = _time_dft_tables(T, T, np.arange(T))
        fg = _fourier(ct, stn, u, gb, wts["four_mix"])
    y = _out_proj(og, fg, sa, sb, x, gate, wts["final_norm_g"], wts["w_proj_a"], wts["w_proj_b"], wts["w_out"])
    return y, sf, sbw


def kernel(x_prompt, x_sample, state_gla_fwd, state_gla_bwd, c, c_ctx, w_ada, b_ada, norm_g, w_in,
           w_alpha_fwd, b_alpha_fwd, w_alpha_bwd, b_alpha_bwd, gla_norm_g, w_four, w_proj_a, w_proj_b,
           w_out, final_norm_g):
    depth = w_in.shape[0]
    assert depth == 1, "single trunk layer"
    bs = x_sample.shape[0]

    n_cond = bs + 1
    cond_rows = -(-n_cond // SUBLANES) * SUBLANES
    cond = jnp.concatenate([c, c_ctx[None, :], jnp.zeros((cond_rows - n_cond, D_MODEL), F32)], axis=0)
    ada = _ada(cond, w_ada[0], b_ada[0][None, :])
    shift, scale, gate = ada[:, :D_MODEL], ada[:, D_MODEL:2 * D_MODEL], ada[:, 2 * D_MODEL:]
    mod = lambda m, lo, hi: m[lo:hi][:, None, :]

    wi = w_in[0]
    o_r = _W_HEAD_COLS
    o_u = o_r + 2 * GATE_RANK
    w_r = wi[:, o_r:o_u].astype(BF16)
    w_ab = jnp.zeros((2 * GATE_RANK, 2 * KDIM), F32)
    w_ab = w_ab.at[:GATE_RANK, :KDIM].set(w_alpha_fwd[0]).at[GATE_RANK:, KDIM:].set(w_alpha_bwd[0])
    wts = dict(
        norm_g=norm_g[0][None, :], w_head=wi.astype(BF16), w_tail=wi[:, o_u:].astype(BF16),
        w_r=w_r, w_ab=w_ab.astype(BF16),
        b_ab=jnp.concatenate([b_alpha_fwd[0], b_alpha_bwd[0]])[None, :],
        gla_norm_g=gla_norm_g[0][None, :], four_mix=_mix_weights(w_four[0]),
        w_proj_a=w_proj_a[0].astype(BF16), w_proj_b=w_proj_b[0].astype(BF16), w_out=w_out[0].astype(BF16),
        final_norm_g=final_norm_g[None, :])

    y_prompt, sf, sb = _path(x_prompt, mod(scale, bs, bs + 1), mod(shift, bs, bs + 1), mod(gate, bs, bs + 1),
                             None, None, wts)
    y_sample, _, _ = _path(x_sample, mod(scale, 0, bs), mod(shift, 0, bs), mod(gate, 0, bs),
                           (state_gla_fwd[:, 0], state_gla_bwd[:, 0]), _rope_tables(x_sample.shape[1]), wts)
    return (y_prompt, y_sample, sf[:, None].astype(x_prompt.dtype), sb[:, None].astype(x_prompt.dtype))
```

```python
import functools

import numpy as np
import jax
import jax.numpy as jnp
from jax import lax
from jax.experimental import pallas as pl
from jax.experimental.pallas import tpu as pltpu

F32 = jnp.float32
BF16 = jnp.bfloat16

D_MODEL = 1024
N_HEADS = 4
HEAD_DK = 128
HEAD_DV = 256
KDIM = N_HEADS * HEAD_DK
VDIM = N_HEADS * HEAD_DV
GATE_RANK = 16
GATE_NORM = 16.0
CHUNK = 64
N_FGROUPS = 4
FGROUP_CH = 128
FOURIER_DIM = N_FGROUPS * FGROUP_CH
GRID_W = 64
ROPE_BASE = 10000.0
EPS = 1e-6
LOG2_E = float(np.log2(np.e))

LANES = 128
SUBLANES = 8
MXU_COLS = 256
TOKEN_TILE = 256
CHUNKS_PER_TILE = TOKEN_TILE // CHUNK
IN_TILES_PER_STEP = 2
OUT_TILES_PER_STEP = 4
GLA_SHORT_SEQS_PER_STEP = 4
GLA_LONG_SEQ_HEADS_PER_STEP = 2
GLA_TILE_UNROLL = 4
FOURIER_ROWS_PER_STEP = 512
FOURIER_SHORT_SEQS_PER_STEP = 8
FOURIER_SPLIT_MIN_LEN = 2048
VMEM_LIMIT_BYTES = 56 * 1024 * 1024

_OFF_QK = 0
_OFF_V = _OFF_QK + 2 * KDIM
_OFF_GA = _OFF_V + VDIM
_W_HEAD_COLS = _OFF_GA + VDIM
_OFF_U = 0
_OFF_GB = _OFF_U + FOURIER_DIM
_OFF_MA = _OFF_GB + FOURIER_DIM
_OFF_MB = _OFF_MA + D_MODEL
_W_TAIL_COLS = _OFF_MB + D_MODEL


def _mm(a, b):
    return jnp.dot(a, b, preferred_element_type=F32)


def _mm_ta(a, b):
    return lax.dot_general(a, b, (((0,), (0,)), ((), ())), preferred_element_type=F32)


def _mm_tb(a, b):
    return lax.dot_general(a, b, (((1,), (1,)), ((), ())), preferred_element_type=F32)


def _mm_ta_tb(a, b):
    return lax.dot_general(a, b, (((0,), (1,)), ((), ())), preferred_element_type=F32)


def _split_bf16(x):
    hi = x.astype(BF16)
    lo = (x - hi.astype(F32)).astype(BF16)
    return hi, lo


def _const_spec(shape):
    nd = len(shape)
    return pl.BlockSpec(shape, lambda *_: (0,) * nd)


_ADA_SHIFT, _ADA_SCALE, _ADA_GATE = range(3)


def _mod_spec(mod, col_block):
    row0, per_seq = mod
    if per_seq:
        return pl.BlockSpec((SUBLANES, D_MODEL), lambda b, t: ((row0 + b) // SUBLANES, col_block))
    return pl.BlockSpec((SUBLANES, D_MODEL), lambda b, t: (row0 // SUBLANES, col_block))


def _mod_row(mod, ref):
    row0, per_seq = mod
    row = (row0 + pl.program_id(0)) % SUBLANES if per_seq else row0 % SUBLANES
    return ref.at[pl.ds(row, 1)]


def _ada_kernel(c_ref, w_ref, b_ref, o_ref):
    c = c_ref[...]
    s = c * jax.nn.sigmoid(c)
    o_ref[...] = _mm(s.astype(BF16), w_ref[...].astype(BF16)) + b_ref[...]


def _ada(cond, w, b):
    rows = cond.shape[0]
    n_blocks = w.shape[1] // D_MODEL
    return pl.pallas_call(
        _ada_kernel,
        grid=(n_blocks,),
        in_specs=[pl.BlockSpec((rows, D_MODEL), lambda n: (0, 0)),
                  pl.BlockSpec((D_MODEL, D_MODEL), lambda n: (0, n)),
                  pl.BlockSpec((1, D_MODEL), lambda n: (0, n))],
        out_specs=pl.BlockSpec((rows, D_MODEL), lambda n: (0, n)),
        out_shape=jax.ShapeDtypeStruct((rows, n_blocks * D_MODEL), F32),
        compiler_params=pltpu.CompilerParams(
            dimension_semantics=("arbitrary",), vmem_limit_bytes=VMEM_LIMIT_BYTES),
        name="ada",
    )(cond, w, b)


def _in_proj_kernel(use_rope, mod, *refs):
    n_shared = 8
    n_in = 1 + n_shared + (2 if use_rope else 0)
    dt_slots = (n_in + 6, n_in + 7)
    tiles = []
    for i in range(IN_TILES_PER_STEP):
        rows = pl.ds(i * TOKEN_TILE, TOKEN_TILE)
        tile_refs = []
        for idx, ref in enumerate(refs):
            if idx in (1, 2):
                tile_refs.append(_mod_row(mod, ref))
            elif 1 <= idx <= n_shared:
                tile_refs.append(ref)
            elif idx in dt_slots:
                tile_refs.append(ref.at[i])
            else:
                tile_refs.append(ref.at[rows])
        tiles.append(_in_proj_tile(use_rope, *tile_refs))
    for _ in range(_IN_PROJ_STAGES):
        for tile in tiles:
            next(tile)


def _project(hb, w_ref, off, width, act, out_ref):
    for n in range(0, width, MXU_COLS):
        z = _mm(hb, w_ref[:, off + n:off + n + MXU_COLS])
        out_ref[:, n:n + MXU_COLS] = act(z).astype(BF16)


_IN_PROJ_STAGES = 3


def _in_proj_tile(use_rope, *refs):
    if use_rope:
        (x_ref, sc_ref, sh_ref, ng_ref, wa_ref, wb_ref, wr_ref, wab_ref, bab_ref, cos_ref, sin_ref,
         qdf_ref, kif_ref, kef_ref, qdb_ref, kib_ref, keb_ref, dtf_ref, dtb_ref,
         v_ref, ga_ref, u_ref, gb_ref, sa_ref, sb_ref) = refs
    else:
        (x_ref, sc_ref, sh_ref, ng_ref, wa_ref, wb_ref, wr_ref, wab_ref, bab_ref,
         qdf_ref, kif_ref, kef_ref, qdb_ref, kib_ref, keb_ref, dtf_ref, dtb_ref,
         v_ref, ga_ref, u_ref, gb_ref, sa_ref, sb_ref) = refs

    x = x_ref[...]
    xn = x * lax.rsqrt(jnp.mean(x * x, axis=-1, keepdims=True) + EPS)
    h = xn * (ng_ref[...] * (1.0 + sc_ref[...])) + sh_ref[...]
    hb = h.astype(BF16)

    r_t = _mm_ta_tb(wr_ref[:, :2 * GATE_RANK], hb)
    qk = _mm(hb, wa_ref[:, _OFF_QK:_OFF_QK + 2 * KDIM])
    xg = _mm_ta(r_t.astype(BF16), wab_ref[...]) + bab_ref[...]
    yield

    _project(hb, wa_ref, _OFF_V, VDIM, lambda z: z, v_ref)
    g_all = (jnp.minimum(xg, 0.0) - jnp.log(1.0 + jnp.exp(-jnp.abs(xg)))) * (LOG2_E / GATE_NORM)
    row = lax.broadcasted_iota(jnp.int32, (CHUNK, 2 * CHUNK), 0)
    col = lax.broadcasted_iota(jnp.int32, (CHUNK, 2 * CHUNK), 1) % CHUNK
    bcs = []
    for direction in range(2):
        g_hi, g_lo = _split_bf16(g_all[:, direction * KDIM:(direction + 1) * KDIM])
        tri = jnp.where((col <= row) if direction == 0 else (col >= row), 1.0, 0.0).astype(BF16)
        bc_chunks = []
        for c in range(CHUNKS_PER_TILE):
            cr = slice(c * CHUNK, (c + 1) * CHUNK)
            bc_chunks.append(_mm(tri, jnp.concatenate([g_hi[cr], g_lo[cr]], axis=0)))
        bcs.append(jnp.concatenate(bc_chunks, axis=0))
    yield

    q = qk[:, :KDIM] * (HEAD_DK ** -0.5)
    k = qk[:, KDIM:]
    if use_rope:
        cos = cos_ref[...]
        sin = sin_ref[...]
        lane = lax.broadcasted_iota(jnp.int32, (TOKEN_TILE, HEAD_DK), 1)
        first_half = (lane // (HEAD_DK // 4)) % 2 == 0

        def rope(t):
            outs = []
            for hh in range(N_HEADS):
                th = t[:, hh * HEAD_DK:(hh + 1) * HEAD_DK]
                partner = jnp.where(first_half,
                                    pltpu.roll(th, HEAD_DK - HEAD_DK // 4, axis=1),
                                    pltpu.roll(th, HEAD_DK // 4, axis=1))
                outs.append(th * cos + partner * sin)
            return jnp.concatenate(outs, axis=1)

        q = rope(q)
        k = rope(k)

    for direction, (qd_ref, ki_ref, ke_ref, dt_ref) in enumerate(
            ((qdf_ref, kif_ref, kef_ref, dtf_ref), (qdb_ref, kib_ref, keb_ref, dtb_ref))):
        bc = bcs[direction]
        edge = CHUNK - 1 if direction == 0 else 0
        bl_rows = bc.reshape(CHUNKS_PER_TILE, CHUNK, KDIM)[:, edge, :]
        bl = jnp.broadcast_to(bl_rows[:, None, :], (CHUNKS_PER_TILE, CHUNK, KDIM)).reshape(TOKEN_TILE, KDIM)
        qd_ref[...] = (q * jnp.exp2(bc)).astype(BF16)
        ki_ref[...] = (k * jnp.exp2(-bc)).astype(BF16)
        ke_ref[...] = (k * jnp.exp2(bl - bc)).astype(BF16)
        padded = jnp.concatenate([bl_rows, jnp.zeros((LANES - CHUNKS_PER_TILE, KDIM), F32)], axis=0)
        dt_ref[...] = jnp.exp2(padded.T)

    _project(hb, wa_ref, _OFF_GA, VDIM, jax.nn.silu, ga_ref)
    _project(hb, wb_ref, _OFF_U, FOURIER_DIM, lambda z: z, u_ref)
    _project(hb, wb_ref, _OFF_GB, FOURIER_DIM, jax.nn.silu, gb_ref)
    _project(hb, wb_ref, _OFF_MA, D_MODEL, jax.nn.sigmoid, sa_ref)
    _project(hb, wb_ref, _OFF_MB, D_MODEL, jax.nn.sigmoid, sb_ref)
    yield


def _in_proj(x, ada, mod, norm_g, w_in, w_tail, w_ab, b_ab, rope):
    B, T, _ = x.shape
    use_rope = rope is not None
    step_rows = IN_TILES_PER_STEP * TOKEN_TILE
    if T < step_rows:
        assert not mod[1] and not use_rope and (B * T) % step_rows == 0
        outs = _in_proj(x.reshape(B * T // step_rows, step_rows, D_MODEL), ada, mod, norm_g,
                        w_in, w_tail, w_ab, b_ab, rope)
        return [o.reshape((B, T // TOKEN_TILE) + o.shape[2:]) if o.ndim == 4 else o.reshape(B, T, o.shape[-1])
                for o in outs]
    nt = T // TOKEN_TILE

    def tok_spec(cols):
        return pl.BlockSpec((None, step_rows, cols), lambda b, t: (b, t, 0))

    assert _W_HEAD_COLS % LANES == 0 and 2 * GATE_RANK <= LANES
    in_specs = [
        tok_spec(D_MODEL), _mod_spec(mod, _ADA_SCALE), _mod_spec(mod, _ADA_SHIFT), _const_spec((1, D_MODEL)),
        _const_spec((D_MODEL, _W_HEAD_COLS)), _const_spec((D_MODEL, _W_TAIL_COLS)),
        pl.BlockSpec((D_MODEL, LANES), lambda b, t: (0, _W_HEAD_COLS // LANES)),
        _const_spec((2 * GATE_RANK, 2 * KDIM)), _const_spec((1, 2 * KDIM)),
    ]
    args = [x, ada, ada, norm_g, w_in, w_tail, w_in, w_ab, b_ab]
    if use_rope:
        in_specs += [pl.BlockSpec((step_rows, HEAD_DK), lambda b, t: (t, 0))] * 2
        args += list(rope)

    dt_spec = pl.BlockSpec((None, IN_TILES_PER_STEP, KDIM, LANES), lambda b, t: (b, t, 0, 0))
    tok_bf = lambda cols: jax.ShapeDtypeStruct((B, T, cols), BF16)
    dt_shape = jax.ShapeDtypeStruct((B, nt, KDIM, LANES), F32)
    out_specs = [tok_spec(KDIM)] * 6 + [dt_spec] * 2 + [
        tok_spec(VDIM), tok_spec(VDIM), tok_spec(FOURIER_DIM),
        tok_spec(FOURIER_DIM), tok_spec(D_MODEL), tok_spec(D_MODEL)]
    out_shape = [tok_bf(KDIM)] * 6 + [dt_shape] * 2 + [
        tok_bf(VDIM), tok_bf(VDIM), tok_bf(FOURIER_DIM),
        tok_bf(FOURIER_DIM), tok_bf(D_MODEL), tok_bf(D_MODEL)]
    return pl.pallas_call(
        functools.partial(_in_proj_kernel, use_rope, mod),
        grid=(B, T // step_rows),
        in_specs=in_specs,
        out_specs=out_specs,
        out_shape=out_shape,
        compiler_params=pltpu.CompilerParams(
            dimension_semantics=("arbitrary", "arbitrary"), vmem_limit_bytes=VMEM_LIMIT_BYTES),
        name="in_proj_rope" if use_rope else "in_proj",
    )(*args)


def _gla_kernel(heads, n_tiles, has_init, seqs, *refs):
    gain_pos = 10 + (2 if has_init else 0)
    for bb in range(seqs):
        _gla_seq(heads, n_tiles, has_init, *[r if i == gain_pos else r.at[bb] for i, r in enumerate(refs)])


def _gla_seq(heads, n_tiles, has_init, *refs):
    (qdf_ref, kif_ref, kef_ref, qdb_ref, kib_ref, keb_ref, dtf_ref, dtb_ref, v_ref, ga_ref) = refs[:10]
    refs = refs[10:]
    if has_init:
        s0f_ref, s0b_ref = refs[:2]
        refs = refs[2:]
    gn_ref, og_ref, sf_ref, sb_ref, state, d_state, of_acc, ob_acc = refs
    if has_init:
        state[0] = s0f_ref[...]
        state[1] = s0b_ref[...]
    else:
        state[...] = jnp.zeros_like(state)
    dir_refs = ((qdf_ref, kif_ref, kef_ref, dtf_ref, of_acc), (qdb_ref, kib_ref, keb_ref, dtb_ref, ob_acc))

    def tile_body(finalize, j, carry):
        row = lax.broadcasted_iota(jnp.int32, (TOKEN_TILE, TOKEN_TILE), 0)
        col = lax.broadcasted_iota(jnp.int32, (TOKEN_TILE, TOKEN_TILE), 1)
        same_chunk = (row // CHUNK) == (col // CHUNK)
        causal = (same_chunk & (col <= row), same_chunk & (col >= row))
        tiles = (j, n_tiles - 1 - j)
        bases = tuple(pl.multiple_of(t * TOKEN_TILE, TOKEN_TILE) for t in tiles)

        scores = {}
        for direction in range(2):
            qd_ref, ki_ref, ke_ref, _, o_acc = dir_refs[direction]
            trows = pl.ds(bases[direction], TOKEN_TILE)
            for hh in range(heads):
                kc = slice(hh * HEAD_DK, (hh + 1) * HEAD_DK)
                vc = slice(hh * HEAD_DV, (hh + 1) * HEAD_DV)
                scores[direction, hh] = jnp.where(
                    causal[direction], _mm_tb(qd_ref[trows, kc], ki_ref[trows, kc]), 0.0).astype(BF16)
                for c in range(CHUNKS_PER_TILE):
                    rows = pl.ds(bases[direction] + c * CHUNK, CHUNK)
                    d_state[direction, hh, c] = _mm_ta(ke_ref[rows, kc], v_ref[rows, vc])

        steps = [(step, direction) for step in range(CHUNKS_PER_TILE) for direction in range(2)]
        if finalize[0] != finalize[1]:
            steps.sort(key=lambda sd: finalize[sd[1]])
        for step, direction in steps:
            qd_ref, _, _, dt_ref, o_acc = dir_refs[direction]
            other_acc = dir_refs[1 - direction][4]
            c = step if direction == 0 else CHUNKS_PER_TILE - 1 - step
            rows = pl.ds(bases[direction] + c * CHUNK, CHUNK)
            for hh in range(heads):
                kc = slice(hh * HEAD_DK, (hh + 1) * HEAD_DK)
                vc = slice(hh * HEAD_DV, (hh + 1) * HEAD_DV)
                s = state[direction, hh]
                blk = (c * CHUNK) // LANES
                sc = scores[direction, hh][c * CHUNK:(c + 1) * CHUNK, blk * LANES:(blk + 1) * LANES]
                v_blk = v_ref[pl.ds(bases[direction] + blk * LANES, LANES), vc]
                o = _mm(jnp.concatenate([qd_ref[rows, kc], sc], axis=1),
                        jnp.concatenate([s.astype(BF16), v_blk], axis=0))
                if finalize[direction]:
                    o = o + other_acc[rows, vc]
                    on = o * lax.rsqrt(jnp.mean(o * o, axis=-1, keepdims=True) + EPS) * gn_ref[...]
                    og_ref[rows, vc] = (on * ga_ref[rows, vc].astype(F32)).astype(BF16)
                else:
                    o_acc[rows, vc] = o
                dec = dt_ref[tiles[direction], kc, c:c + 1]
                state[direction, hh] = dec * s + d_state[direction, hh, c]
        return carry

    first_half = n_tiles // 2
    unroll = GLA_TILE_UNROLL if first_half % GLA_TILE_UNROLL == 0 else 1
    lax.fori_loop(0, first_half, functools.partial(tile_body, (False, False)), 0, unroll=unroll)
    if n_tiles % 2:
        tile_body((False, True), first_half, 0)
    lax.fori_loop(n_tiles - first_half, n_tiles, functools.partial(tile_body, (True, True)), 0, unroll=unroll)
    sf_ref[...] = state[0]
    sb_ref[...] = state[1]


def _gla(qdf, kif, kef, qdb, kib, keb, dtf, dtb, v, ga, init_states, gla_norm_g):
    B, T, _ = v.shape
    nt = T // TOKEN_TILE
    has_init = init_states is not None
    seqs = GLA_SHORT_SEQS_PER_STEP if nt == 1 else 1
    assert B % seqs == 0
    heads = N_HEADS if nt == 1 else GLA_LONG_SEQ_HEADS_PER_STEP
    hg = N_HEADS // heads
    k_spec = pl.BlockSpec((seqs, T, heads * HEAD_DK), lambda b, h: (b, 0, h))
    v_spec = pl.BlockSpec((seqs, T, heads * HEAD_DV), lambda b, h: (b, 0, h))
    dt_spec = pl.BlockSpec((seqs, nt, heads * HEAD_DK, LANES), lambda b, h: (b, 0, h, 0))
    s_spec = pl.BlockSpec((seqs, heads, HEAD_DK, HEAD_DV), lambda b, h: (b, h, 0, 0))
    s_shape = jax.ShapeDtypeStruct((B, N_HEADS, HEAD_DK, HEAD_DV), F32)
    return pl.pallas_call(
        functools.partial(_gla_kernel, heads, nt, has_init, seqs),
        grid=(B // seqs, hg),
        in_specs=([k_spec] * 6 + [dt_spec] * 2 + [v_spec, v_spec] + [s_spec] * (2 if has_init else 0)
                  + [_const_spec((1, HEAD_DV))]),
        out_specs=[v_spec, s_spec, s_spec],
        out_shape=[jax.ShapeDtypeStruct((B, T, VDIM), BF16), s_shape, s_shape],
        scratch_shapes=[pltpu.VMEM((seqs, 2, heads, HEAD_DK, HEAD_DV), F32),
                        pltpu.VMEM((seqs, 2, heads, CHUNKS_PER_TILE, HEAD_DK, HEAD_DV), F32),
                        pltpu.VMEM((seqs, T, heads * HEAD_DV), F32),
                        pltpu.VMEM((seqs, T, heads * HEAD_DV), F32)],
        compiler_params=pltpu.CompilerParams(
            dimension_semantics=("arbitrary", "arbitrary"), vmem_limit_bytes=VMEM_LIMIT_BYTES),
        name="gla",
    )(qdf, kif, kef, qdb, kib, keb, dtf, dtb, v, ga, *(init_states or ()), gla_norm_g)


def _mix_weights_kernel(tab_ref, wf_ref, o_ref):
    for g in range(N_FGROUPS):
        o_ref[g] = jnp.dot(tab_ref[...], wf_ref[g], preferred_element_type=F32,
                           precision=lax.Precision.HIGHEST).astype(BF16)


def _mix_weights(wf):
    n = np.arange(FGROUP_CH)
    ang = 2.0 * np.pi * ((n[:, None] * n[None, :]) % FGROUP_CH) / FGROUP_CH
    tab = jnp.asarray(np.concatenate([np.cos(ang), np.sin(ang)], axis=0) / np.sqrt(FGROUP_CH), F32)
    return pl.pallas_call(
        _mix_weights_kernel,
        out_shape=jax.ShapeDtypeStruct((N_FGROUPS, 2 * FGROUP_CH, FGROUP_CH), BF16),
        name="mix_weights",
    )(tab, wf)


def _mix_and_gate(fp, fq, mix_ref, gb, store):
    fp = fp.astype(BF16)
    fq = fq.astype(BF16)
    for g in range(N_FGROUPS):
        cols = slice(g * FGROUP_CH, (g + 1) * FGROUP_CH)
        z = _mm(jnp.concatenate([fp[:, cols], fq[:, cols]], axis=1), mix_ref[g])
        store(cols, (z * gb(cols).astype(F32)).astype(BF16))


def _fourier_kernel(seqs, ct_ref, st_ref, u_ref, gb_ref, mix_ref, o_ref):
    for bb in range(seqs):
        u = u_ref[bb]

        def store(cols, val, bb=bb):
            o_ref[bb, :, cols] = val

        _mix_and_gate(_mm(ct_ref[...], u), _mm(st_ref[...], u), mix_ref,
                      lambda cols, bb=bb: gb_ref[bb, :, cols], store)


def _fourier(ct, stn, u, gb, mix):
    B, T, _ = u.shape
    tf = min(T, FOURIER_ROWS_PER_STEP)
    seqs = FOURIER_SHORT_SEQS_PER_STEP if T < FOURIER_ROWS_PER_STEP else 1
    assert B % seqs == 0
    tab_spec = pl.BlockSpec((tf, T), lambda b, t: (t, 0))
    seq_spec = pl.BlockSpec((seqs, T, FOURIER_DIM), lambda b, t: (b, 0, 0))
    tile_spec = pl.BlockSpec((seqs, tf, FOURIER_DIM), lambda b, t: (b, t, 0))
    return pl.pallas_call(
        functools.partial(_fourier_kernel, seqs),
        grid=(B // seqs, T // tf),
        in_specs=[tab_spec, tab_spec, seq_spec, tile_spec,
                  _const_spec((N_FGROUPS, 2 * FGROUP_CH, FGROUP_CH))],
        out_specs=tile_spec,
        out_shape=jax.ShapeDtypeStruct((B, T, FOURIER_DIM), BF16),
        compiler_params=pltpu.CompilerParams(
            dimension_semantics=("arbitrary", "arbitrary"), vmem_limit_bytes=VMEM_LIMIT_BYTES),
        name="fourier",
    )(ct, stn, u, gb, mix)


def _fourier_split_kernel(cee_ref, see_ref, ceo_ref, seo_ref, co_ref, so_ref, u_ref, gb_ref, mix_ref, o_ref,
                          wide, u_odd, u_even):
    half, quarter = u_odd.shape[0], u_even.shape[1]
    for g in range(N_FGROUPS):
        cols = slice(g * FGROUP_CH, (g + 1) * FGROUP_CH)
        wide[g] = u_ref[:, cols].astype(F32)
        u_odd[:, cols] = wide[g, pl.ds(1, half, stride=2), :].astype(BF16)
        for r in range(2):
            u_even[r, :, cols] = wide[g, pl.ds(2 * r, quarter, stride=4), :].astype(BF16)

    eep, eeq = _mm(cee_ref[...], u_even[0]), _mm(see_ref[...], u_even[0])
    eop, eoq = _mm(ceo_ref[...], u_even[1]), _mm(seo_ref[...], u_even[1])
    e_blocks = ((eep + eop, eeq + eoq), (eep - eop, eeq - eoq))
    for kb, (ep, eq) in enumerate(e_blocks):
        rows = slice(kb * quarter, (kb + 1) * quarter)
        op, oq = _mm(co_ref[rows, :], u_odd[...]), _mm(so_ref[rows, :], u_odd[...])
        for upper, (fp, fq) in enumerate(((ep + op, eq + oq), (ep - op, eq - oq))):
            def store(cols, val, upper=upper):
                o_ref[upper, rows, cols] = val

            _mix_and_gate(fp, fq, mix_ref, lambda cols, upper=upper: gb_ref[upper, rows, cols], store)


def _fourier_split(u, gb, mix):
    B, T, _ = u.shape
    half, quarter = T // 2, T // 4
    cee, seen = _time_dft_tables(T, quarter, 4 * np.arange(quarter))
    ceo, seon = _time_dft_tables(T, quarter, 4 * np.arange(quarter) + 2)
    co, son = _time_dft_tables(T, half, 2 * np.arange(half) + 1)
    seq_spec = pl.BlockSpec((None, T, FOURIER_DIM), lambda b: (b, 0, 0))
    halves_spec = pl.BlockSpec((None, 2, half, FOURIER_DIM), lambda b: (b, 0, 0, 0))
    out = pl.pallas_call(
        _fourier_split_kernel,
        grid=(B,),
        in_specs=[_const_spec((quarter, quarter))] * 4 + [_const_spec((half, half))] * 2 + [
            seq_spec, halves_spec, _const_spec((N_FGROUPS, 2 * FGROUP_CH, FGROUP_CH))],
        out_specs=halves_spec,
        out_shape=jax.ShapeDtypeStruct((B, 2, half, FOURIER_DIM), BF16),
        scratch_shapes=[pltpu.VMEM((N_FGROUPS, T, FGROUP_CH), F32),
                        pltpu.VMEM((half, FOURIER_DIM), BF16),
                        pltpu.VMEM((2, quarter, FOURIER_DIM), BF16)],
        compiler_params=pltpu.CompilerParams(
            dimension_semantics=("arbitrary",), vmem_limit_bytes=VMEM_LIMIT_BYTES),
        name="fourier_split",
    )(cee, seen, ceo, seon, co, son, u, gb.reshape(B, 2, half, FOURIER_DIM), mix)
    return out.reshape(B, T, FOURIER_DIM)


def _out_proj_kernel(mod, og_ref, fg_ref, sa_ref, sb_ref, x_ref, gate_ref, fng_ref, wpa_ref, wpb_ref, wo_ref,
                     y_ref):
    gate_ref = _mod_row(mod, gate_ref)
    for i in range(OUT_TILES_PER_STEP):
        rows = slice(i * TOKEN_TILE, (i + 1) * TOKEN_TILE)
        ya = _mm(og_ref[rows, :], wpa_ref[...])
        yb = _mm(fg_ref[rows, :], wpb_ref[...])
        merged = sa_ref[rows, :].astype(F32) * ya + sb_ref[rows, :].astype(F32) * yb
        xo = x_ref[rows, :] + gate_ref[...] * _mm(merged.astype(BF16), wo_ref[...])
        y_ref[rows, :] = xo * lax.rsqrt(jnp.mean(xo * xo, axis=-1, keepdims=True) + EPS) * fng_ref[...]


def _out_proj(og, fg, sa, sb, x, ada, mod, final_norm_g, wpa, wpb, wo):
    B, T, _ = x.shape
    step_rows = OUT_TILES_PER_STEP * TOKEN_TILE
    if T < step_rows:
        assert not mod[1] and (B * T) % step_rows == 0
        fold = lambda a: a.reshape(B * T // step_rows, step_rows, a.shape[-1])
        y = _out_proj(fold(og), fold(fg), fold(sa), fold(sb), fold(x), ada, mod, final_norm_g, wpa, wpb, wo)
        return y.reshape(B, T, D_MODEL)

    def tok_spec(cols):
        return pl.BlockSpec((None, step_rows, cols), lambda b, t: (b, t, 0))

    return pl.pallas_call(
        functools.partial(_out_proj_kernel, mod),
        grid=(B, T // step_rows),
        in_specs=[tok_spec(VDIM), tok_spec(FOURIER_DIM), tok_spec(D_MODEL), tok_spec(D_MODEL), tok_spec(D_MODEL),
                  _mod_spec(mod, _ADA_GATE), _const_spec((1, D_MODEL)),
                  _const_spec((VDIM, D_MODEL)), _const_spec((FOURIER_DIM, D_MODEL)), _const_spec((D_MODEL, D_MODEL))],
        out_specs=tok_spec(D_MODEL),
        out_shape=jax.ShapeDtypeStruct((B, T, D_MODEL), F32),
        compiler_params=pltpu.CompilerParams(
            dimension_semantics=("arbitrary", "arbitrary"), vmem_limit_bytes=VMEM_LIMIT_BYTES),
        name="out_proj",
    )(og, fg, sa, sb, x, ada, final_norm_g, wpa, wpb, wo)


def _time_dft_tables(T, n_rows, positions):
    n = np.asarray(positions)
    ang = 2.0 * np.pi * ((np.arange(n_rows)[:, None] * n[None, :]) % T) / T
    scale = 1.0 / np.sqrt(T)
    return jnp.asarray(np.cos(ang) * scale, F32).astype(BF16), jnp.asarray(-np.sin(ang) * scale, F32).astype(BF16)


def _rope_tables(T):
    rows = T // GRID_W
    r = np.repeat(np.arange(rows), GRID_W).astype(np.float64)
    c = np.tile(np.arange(GRID_W), rows).astype(np.float64)
    n_freq = HEAD_DK // 4
    freqs = ROPE_BASE ** (-np.arange(n_freq, dtype=np.float64) / n_freq)
    ang_r = r[:, None] * freqs
    ang_c = c[:, None] * freqs
    cos = np.concatenate([np.cos(ang_r), np.cos(ang_r), np.cos(ang_c), np.cos(ang_c)], axis=-1)
    sin = np.concatenate([-np.sin(ang_r), np.sin(ang_r), -np.sin(ang_c), np.sin(ang_c)], axis=-1)
    return jnp.asarray(cos, F32), jnp.asarray(sin, F32)


def _path(x, ada, mod, init_states, rope, wts):
    T = x.shape[1]
    (qdf, kif, kef, qdb, kib, keb, dtf, dtb, v, ga, u, gb, sa, sb) = _in_proj(
        x, ada, mod, wts["norm_g"], wts["w_in"], wts["w_tail"], wts["w_ab"], wts["b_ab"], rope)
    og, sf, sbw = _gla(qdf, kif, kef, qdb, kib, keb, dtf, dtb, v, ga, init_states, wts["gla_norm_g"])
    if T >= FOURIER_SPLIT_MIN_LEN:
        fg = _fourier_split(u, gb, wts["four_mix"])
    else:
        ct, stn = _time_dft_tables(T, T, np.arange(T))
        fg = _fourier(ct, stn, u, gb, wts["four_mix"])
    y = _out_proj(og, fg, sa, sb, x, ada, mod, wts["final_norm_g"], wts["w_proj_a"], wts["w_proj_b"], wts["w_out"])
    return y, sf, sbw


def kernel(x_prompt, x_sample, state_gla_fwd, state_gla_bwd, c, c_ctx, w_ada, b_ada, norm_g, w_in,
           w_alpha_fwd, b_alpha_fwd, w_alpha_bwd, b_alpha_bwd, gla_norm_g, w_four, w_proj_a, w_proj_b,
           w_out, final_norm_g):
    depth = w_in.shape[0]
    assert depth == 1, "single trunk layer"
    bs = x_sample.shape[0]

    n_cond = bs + 1
    cond_rows = -(-n_cond // SUBLANES) * SUBLANES
    cond = jnp.concatenate([c, c_ctx[None, :], jnp.zeros((cond_rows - n_cond, D_MODEL), F32)], axis=0)
    ada = _ada(cond, w_ada[0], b_ada[0][None, :])

    wi = w_in[0]
    o_u = _W_HEAD_COLS + 2 * GATE_RANK
    w_ab = jnp.zeros((2 * GATE_RANK, 2 * KDIM), F32)
    w_ab = w_ab.at[:GATE_RANK, :KDIM].set(w_alpha_fwd[0]).at[GATE_RANK:, KDIM:].set(w_alpha_bwd[0])
    wts = dict(
        norm_g=norm_g[0][None, :], w_in=wi.astype(BF16), w_tail=wi[:, o_u:].astype(BF16),
        w_ab=w_ab.astype(BF16),
        b_ab=jnp.concatenate([b_alpha_fwd[0], b_alpha_bwd[0]])[None, :],
        gla_norm_g=gla_norm_g[0][None, :], four_mix=_mix_weights(w_four[0]),
        w_proj_a=w_proj_a[0].astype(BF16), w_proj_b=w_proj_b[0].astype(BF16), w_out=w_out[0].astype(BF16),
        final_norm_g=final_norm_g[None, :])

    y_prompt, sf, sb = _path(x_prompt, ada, (bs, False), None, None, wts)
    y_sample, _, _ = _path(x_sample, ada, (0, True),
                           (state_gla_fwd[:, 0], state_gla_bwd[:, 0]), _rope_tables(x_sample.shape[1]), wts)
    return (y_prompt, y_sample, sf[:, None].astype(x_prompt.dtype), sb[:, None].astype(x_prompt.dtype))
```

```python
import functools

import numpy as np
import jax
import jax.numpy as jnp
from jax import lax
from jax.experimental import pallas as pl
from jax.experimental.pallas import tpu as pltpu

F32 = jnp.float32
BF16 = jnp.bfloat16

D_MODEL = 1024
N_HEADS = 4
HEAD_DK = 128
HEAD_DV = 256
KDIM = N_HEADS * HEAD_DK
VDIM = N_HEADS * HEAD_DV
GATE_RANK = 16
GATE_NORM = 16.0
CHUNK = 64
N_FGROUPS = 4
FGROUP_CH = 128
FOURIER_DIM = N_FGROUPS * FGROUP_CH
GRID_W = 64
ROPE_BASE = 10000.0
EPS = 1e-6
LOG2_E = float(np.log2(np.e))

LANES = 128
SUBLANES = 8
MXU_COLS = 256
TOKEN_TILE = 256
CHUNKS_PER_TILE = TOKEN_TILE // CHUNK
IN_TILES_PER_STEP = 2
OUT_TILES_PER_STEP = 4
GLA_SHORT_SEQS_PER_STEP = 4
GLA_LONG_SEQ_HEADS_PER_STEP = 2
GLA_TILE_UNROLL = 4
FOURIER_ROWS_PER_STEP = 512
FOURIER_SHORT_SEQS_PER_STEP = 8
FOURIER_SPLIT_MIN_LEN = 2048
VMEM_LIMIT_BYTES = 56 * 1024 * 1024

_OFF_QK = 0
_OFF_V = _OFF_QK + 2 * KDIM
_OFF_GA = _OFF_V + VDIM
_W_HEAD_COLS = _OFF_GA + VDIM
_OFF_U = 0
_OFF_GB = _OFF_U + FOURIER_DIM
_OFF_MA = _OFF_GB + FOURIER_DIM
_OFF_MB = _OFF_MA + D_MODEL
_W_TAIL_COLS = _OFF_MB + D_MODEL


def _mm(a, b):
    return jnp.dot(a, b, preferred_element_type=F32)


def _mm_ta(a, b):
    return lax.dot_general(a, b, (((0,), (0,)), ((), ())), preferred_element_type=F32)


def _mm_tb(a, b):
    return lax.dot_general(a, b, (((1,), (1,)), ((), ())), preferred_element_type=F32)


def _mm_ta_tb(a, b):
    return lax.dot_general(a, b, (((0,), (1,)), ((), ())), preferred_element_type=F32)


def _split_bf16(x):
    hi = x.astype(BF16)
    lo = (x - hi.astype(F32)).astype(BF16)
    return hi, lo


def _const_spec(shape):
    nd = len(shape)
    return pl.BlockSpec(shape, lambda *_: (0,) * nd)


_ADA_SHIFT, _ADA_SCALE, _ADA_GATE = range(3)


def _mod_spec(mod, col_block):
    row0, per_seq = mod
    if per_seq:
        return pl.BlockSpec((SUBLANES, D_MODEL), lambda b, t: ((row0 + b) // SUBLANES, col_block))
    return pl.BlockSpec((SUBLANES, D_MODEL), lambda b, t: (row0 // SUBLANES, col_block))


def _mod_row(mod, ref):
    row0, per_seq = mod
    row = (row0 + pl.program_id(0)) % SUBLANES if per_seq else row0 % SUBLANES
    return ref.at[pl.ds(row, 1)]


def _ada_kernel(c_ref, w_ref, b_ref, o_ref):
    c = c_ref[...]
    s = c * jax.nn.sigmoid(c)
    o_ref[...] = _mm(s.astype(BF16), w_ref[...].astype(BF16)) + b_ref[...]


def _ada(cond, w, b):
    rows = cond.shape[0]
    n_blocks = w.shape[1] // D_MODEL
    return pl.pallas_call(
        _ada_kernel,
        grid=(n_blocks,),
        in_specs=[pl.BlockSpec((rows, D_MODEL), lambda n: (0, 0)),
                  pl.BlockSpec((D_MODEL, D_MODEL), lambda n: (0, n)),
                  pl.BlockSpec((1, D_MODEL), lambda n: (0, n))],
        out_specs=pl.BlockSpec((rows, D_MODEL), lambda n: (0, n)),
        out_shape=jax.ShapeDtypeStruct((rows, n_blocks * D_MODEL), F32),
        compiler_params=pltpu.CompilerParams(
            dimension_semantics=("arbitrary",), vmem_limit_bytes=VMEM_LIMIT_BYTES),
        name="ada",
    )(cond, w, b)


def _in_proj_kernel(use_rope, mod, *refs):
    n_shared = 8
    n_in = 1 + n_shared + (2 if use_rope else 0)
    per_tile_slots = (n_in + 2, n_in + 5, n_in + 6, n_in + 7)
    tiles = []
    for i in range(IN_TILES_PER_STEP):
        rows = pl.ds(i * TOKEN_TILE, TOKEN_TILE)
        tile_refs = []
        for idx, ref in enumerate(refs):
            if idx in (1, 2):
                tile_refs.append(_mod_row(mod, ref))
            elif 1 <= idx <= n_shared:
                tile_refs.append(ref)
            elif idx in per_tile_slots:
                tile_refs.append(ref.at[i])
            else:
                tile_refs.append(ref.at[rows])
        tiles.append(_in_proj_tile(use_rope, *tile_refs))
    for _ in range(_IN_PROJ_STAGES):
        for tile in tiles:
            next(tile)


def _project(hb, w_ref, off, width, act, out_ref):
    for n in range(0, width, MXU_COLS):
        z = _mm(hb, w_ref[:, off + n:off + n + MXU_COLS])
        out_ref[:, n:n + MXU_COLS] = act(z).astype(BF16)


_IN_PROJ_STAGES = 3


def _in_proj_tile(use_rope, *refs):
    if use_rope:
        (x_ref, sc_ref, sh_ref, ng_ref, wa_ref, wb_ref, wr_ref, wab_ref, bab_ref, cos_ref, sin_ref,
         qdf_ref, kif_ref, drf_ref, qdb_ref, kib_ref, drb_ref, dtf_ref, dtb_ref,
         v_ref, ga_ref, u_ref, gb_ref, sa_ref, sb_ref) = refs
    else:
        (x_ref, sc_ref, sh_ref, ng_ref, wa_ref, wb_ref, wr_ref, wab_ref, bab_ref,
         qdf_ref, kif_ref, drf_ref, qdb_ref, kib_ref, drb_ref, dtf_ref, dtb_ref,
         v_ref, ga_ref, u_ref, gb_ref, sa_ref, sb_ref) = refs

    x = x_ref[...]
    xn = x * lax.rsqrt(jnp.mean(x * x, axis=-1, keepdims=True) + EPS)
    h = xn * (ng_ref[...] * (1.0 + sc_ref[...])) + sh_ref[...]
    hb = h.astype(BF16)

    r_t = _mm_ta_tb(wr_ref[:, :2 * GATE_RANK], hb)
    qk = _mm(hb, wa_ref[:, _OFF_QK:_OFF_QK + 2 * KDIM])
    xg = _mm_ta(r_t.astype(BF16), wab_ref[...]) + bab_ref[...]
    yield

    _project(hb, wa_ref, _OFF_V, VDIM, lambda z: z, v_ref)
    g_all = (jnp.minimum(xg, 0.0) - jnp.log(1.0 + jnp.exp(-jnp.abs(xg)))) * (LOG2_E / GATE_NORM)
    row = lax.broadcasted_iota(jnp.int32, (CHUNK, 2 * CHUNK), 0)
    col = lax.broadcasted_iota(jnp.int32, (CHUNK, 2 * CHUNK), 1) % CHUNK
    bcs = []
    for direction in range(2):
        g_hi, g_lo = _split_bf16(g_all[:, direction * KDIM:(direction + 1) * KDIM])
        tri = jnp.where((col <= row) if direction == 0 else (col >= row), 1.0, 0.0).astype(BF16)
        bc_chunks = []
        for c in range(CHUNKS_PER_TILE):
            cr = slice(c * CHUNK, (c + 1) * CHUNK)
            bc_chunks.append(_mm(tri, jnp.concatenate([g_hi[cr], g_lo[cr]], axis=0)))
        bcs.append(jnp.concatenate(bc_chunks, axis=0))
    yield

    q = qk[:, :KDIM] * (HEAD_DK ** -0.5)
    k = qk[:, KDIM:]
    if use_rope:
        cos = cos_ref[...]
        sin = sin_ref[...]
        lane = lax.broadcasted_iota(jnp.int32, (TOKEN_TILE, HEAD_DK), 1)
        first_half = (lane // (HEAD_DK // 4)) % 2 == 0

        def rope(t):
            outs = []
            for hh in range(N_HEADS):
                th = t[:, hh * HEAD_DK:(hh + 1) * HEAD_DK]
                partner = jnp.where(first_half,
                                    pltpu.roll(th, HEAD_DK - HEAD_DK // 4, axis=1),
                                    pltpu.roll(th, HEAD_DK // 4, axis=1))
                outs.append(th * cos + partner * sin)
            return jnp.concatenate(outs, axis=1)

        q = rope(q)
        k = rope(k)

    for direction, (qd_ref, ki_ref, dr_ref, dt_ref) in enumerate(
            ((qdf_ref, kif_ref, drf_ref, dtf_ref), (qdb_ref, kib_ref, drb_ref, dtb_ref))):
        bc = bcs[direction]
        edge = CHUNK - 1 if direction == 0 else 0
        bl_rows = bc.reshape(CHUNKS_PER_TILE, CHUNK, KDIM)[:, edge, :]
        qd_ref[...] = (q * jnp.exp2(bc)).astype(BF16)
        ki_ref[...] = (k * jnp.exp2(-bc)).astype(BF16)
        dr_ref[...] = jnp.exp2(jnp.concatenate([bl_rows, jnp.zeros((SUBLANES - CHUNKS_PER_TILE, KDIM), F32)], axis=0))
        padded = jnp.concatenate([bl_rows, jnp.zeros((LANES - CHUNKS_PER_TILE, KDIM), F32)], axis=0)
        dt_ref[...] = jnp.exp2(padded.T)

    _project(hb, wa_ref, _OFF_GA, VDIM, jax.nn.silu, ga_ref)
    _project(hb, wb_ref, _OFF_U, FOURIER_DIM, lambda z: z, u_ref)
    _project(hb, wb_ref, _OFF_GB, FOURIER_DIM, jax.nn.silu, gb_ref)
    _project(hb, wb_ref, _OFF_MA, D_MODEL, jax.nn.sigmoid, sa_ref)
    _project(hb, wb_ref, _OFF_MB, D_MODEL, jax.nn.sigmoid, sb_ref)
    yield


def _in_proj(x, ada, mod, norm_g, w_in, w_tail, w_ab, b_ab, rope):
    B, T, _ = x.shape
    use_rope = rope is not None
    step_rows = IN_TILES_PER_STEP * TOKEN_TILE
    if T < step_rows:
        assert not mod[1] and not use_rope and (B * T) % step_rows == 0
        outs = _in_proj(x.reshape(B * T // step_rows, step_rows, D_MODEL), ada, mod, norm_g,
                        w_in, w_tail, w_ab, b_ab, rope)
        return [o.reshape((B, T // TOKEN_TILE) + o.shape[2:]) if o.ndim == 4 else o.reshape(B, T, o.shape[-1])
                for o in outs]
    nt = T // TOKEN_TILE

    def tok_spec(cols):
        return pl.BlockSpec((None, step_rows, cols), lambda b, t: (b, t, 0))

    assert _W_HEAD_COLS % LANES == 0 and 2 * GATE_RANK <= LANES
    in_specs = [
        tok_spec(D_MODEL), _mod_spec(mod, _ADA_SCALE), _mod_spec(mod, _ADA_SHIFT), _const_spec((1, D_MODEL)),
        _const_spec((D_MODEL, _W_HEAD_COLS)), _const_spec((D_MODEL, _W_TAIL_COLS)),
        pl.BlockSpec((D_MODEL, LANES), lambda b, t: (0, _W_HEAD_COLS // LANES)),
        _const_spec((2 * GATE_RANK, 2 * KDIM)), _const_spec((1, 2 * KDIM)),
    ]
    args = [x, ada, ada, norm_g, w_in, w_tail, w_in, w_ab, b_ab]
    if use_rope:
        in_specs += [pl.BlockSpec((step_rows, HEAD_DK), lambda b, t: (t, 0))] * 2
        args += list(rope)

    dt_spec = pl.BlockSpec((None, IN_TILES_PER_STEP, KDIM, LANES), lambda b, t: (b, t, 0, 0))
    tok_bf = lambda cols: jax.ShapeDtypeStruct((B, T, cols), BF16)
    dt_shape = jax.ShapeDtypeStruct((B, nt, KDIM, LANES), F32)
    dr_spec = pl.BlockSpec((None, IN_TILES_PER_STEP, SUBLANES, KDIM), lambda b, t: (b, t, 0, 0))
    dr_shape = jax.ShapeDtypeStruct((B, nt, SUBLANES, KDIM), F32)
    out_specs = [tok_spec(KDIM), tok_spec(KDIM), dr_spec] * 2 + [dt_spec] * 2 + [
        tok_spec(VDIM), tok_spec(VDIM), tok_spec(FOURIER_DIM),
        tok_spec(FOURIER_DIM), tok_spec(D_MODEL), tok_spec(D_MODEL)]
    out_shape = [tok_bf(KDIM), tok_bf(KDIM), dr_shape] * 2 + [dt_shape] * 2 + [
        tok_bf(VDIM), tok_bf(VDIM), tok_bf(FOURIER_DIM),
        tok_bf(FOURIER_DIM), tok_bf(D_MODEL), tok_bf(D_MODEL)]
    return pl.pallas_call(
        functools.partial(_in_proj_kernel, use_rope, mod),
        grid=(B, T // step_rows),
        in_specs=in_specs,
        out_specs=out_specs,
        out_shape=out_shape,
        compiler_params=pltpu.CompilerParams(
            dimension_semantics=("arbitrary", "arbitrary"), vmem_limit_bytes=VMEM_LIMIT_BYTES),
        name="in_proj_rope" if use_rope else "in_proj",
    )(*args)


def _gla_kernel(heads, n_tiles, has_init, seqs, *refs):
    gain_pos = 10 + (2 if has_init else 0)
    for bb in range(seqs):
        _gla_seq(heads, n_tiles, has_init, *[r if i == gain_pos else r.at[bb] for i, r in enumerate(refs)])


def _gla_seq(heads, n_tiles, has_init, *refs):
    (qdf_ref, kif_ref, drf_ref, qdb_ref, kib_ref, drb_ref, dtf_ref, dtb_ref, v_ref, ga_ref) = refs[:10]
    refs = refs[10:]
    if has_init:
        s0f_ref, s0b_ref = refs[:2]
        refs = refs[2:]
    gn_ref, og_ref, sf_ref, sb_ref, state, d_state, of_acc, ob_acc = refs
    if has_init:
        state[0] = s0f_ref[...]
        state[1] = s0b_ref[...]
    else:
        state[...] = jnp.zeros_like(state)
    dir_refs = ((qdf_ref, kif_ref, drf_ref, dtf_ref, of_acc), (qdb_ref, kib_ref, drb_ref, dtb_ref, ob_acc))

    def tile_body(finalize, j, carry):
        row = lax.broadcasted_iota(jnp.int32, (TOKEN_TILE, TOKEN_TILE), 0)
        col = lax.broadcasted_iota(jnp.int32, (TOKEN_TILE, TOKEN_TILE), 1)
        same_chunk = (row // CHUNK) == (col // CHUNK)
        causal = (same_chunk & (col <= row), same_chunk & (col >= row))
        tiles = (j, n_tiles - 1 - j)
        bases = tuple(pl.multiple_of(t * TOKEN_TILE, TOKEN_TILE) for t in tiles)

        scores = {}
        for direction in range(2):
            qd_ref, ki_ref, dr_ref, _, o_acc = dir_refs[direction]
            trows = pl.ds(bases[direction], TOKEN_TILE)
            for hh in range(heads):
                kc = slice(hh * HEAD_DK, (hh + 1) * HEAD_DK)
                vc = slice(hh * HEAD_DV, (hh + 1) * HEAD_DV)
                scores[direction, hh] = jnp.where(
                    causal[direction], _mm_tb(qd_ref[trows, kc], ki_ref[trows, kc]), 0.0).astype(BF16)
                chunk_decay = dr_ref[tiles[direction], :, kc]
                for c in range(CHUNKS_PER_TILE):
                    rows = pl.ds(bases[direction] + c * CHUNK, CHUNK)
                    ke = (ki_ref[rows, kc].astype(F32) * chunk_decay[c:c + 1, :]).astype(BF16)
                    d_state[direction, hh, c] = _mm_ta(ke, v_ref[rows, vc])

        steps = [(step, direction) for step in range(CHUNKS_PER_TILE) for direction in range(2)]
        if finalize[0] != finalize[1]:
            steps.sort(key=lambda sd: finalize[sd[1]])
        for step, direction in steps:
            qd_ref, _, _, dt_ref, o_acc = dir_refs[direction]
            other_acc = dir_refs[1 - direction][4]
            c = step if direction == 0 else CHUNKS_PER_TILE - 1 - step
            rows = pl.ds(bases[direction] + c * CHUNK, CHUNK)
            for hh in range(heads):
                kc = slice(hh * HEAD_DK, (hh + 1) * HEAD_DK)
                vc = slice(hh * HEAD_DV, (hh + 1) * HEAD_DV)
                s = state[direction, hh]
                blk = (c * CHUNK) // LANES
                sc = scores[direction, hh][c * CHUNK:(c + 1) * CHUNK, blk * LANES:(blk + 1) * LANES]
                v_blk = v_ref[pl.ds(bases[direction] + blk * LANES, LANES), vc]
                o = _mm(jnp.concatenate([qd_ref[rows, kc], sc], axis=1),
                        jnp.concatenate([s.astype(BF16), v_blk], axis=0))
                if finalize[direction]:
                    o = o + other_acc[rows, vc]
                    on = o * lax.rsqrt(jnp.mean(o * o, axis=-1, keepdims=True) + EPS) * gn_ref[...]
                    og_ref[rows, vc] = (on * ga_ref[rows, vc].astype(F32)).astype(BF16)
                else:
                    o_acc[rows, vc] = o
                dec = dt_ref[tiles[direction], kc, c:c + 1]
                state[direction, hh] = dec * s + d_state[direction, hh, c]
        return carry

    first_half = n_tiles // 2
    unroll = GLA_TILE_UNROLL if first_half % GLA_TILE_UNROLL == 0 else 1
    lax.fori_loop(0, first_half, functools.partial(tile_body, (False, False)), 0, unroll=unroll)
    if n_tiles % 2:
        tile_body((False, True), first_half, 0)
    lax.fori_loop(n_tiles - first_half, n_tiles, functools.partial(tile_body, (True, True)), 0, unroll=unroll)
    sf_ref[...] = state[0]
    sb_ref[...] = state[1]


def _gla(qdf, kif, drf, qdb, kib, drb, dtf, dtb, v, ga, init_states, gla_norm_g):
    B, T, _ = v.shape
    nt = T // TOKEN_TILE
    has_init = init_states is not None
    seqs = GLA_SHORT_SEQS_PER_STEP if nt == 1 else 1
    assert B % seqs == 0
    heads = N_HEADS if nt == 1 else GLA_LONG_SEQ_HEADS_PER_STEP
    hg = N_HEADS // heads
    k_spec = pl.BlockSpec((seqs, T, heads * HEAD_DK), lambda b, h: (b, 0, h))
    v_spec = pl.BlockSpec((seqs, T, heads * HEAD_DV), lambda b, h: (b, 0, h))
    dt_spec = pl.BlockSpec((seqs, nt, heads * HEAD_DK, LANES), lambda b, h: (b, 0, h, 0))
    dr_spec = pl.BlockSpec((seqs, nt, SUBLANES, heads * HEAD_DK), lambda b, h: (b, 0, 0, h))
    s_spec = pl.BlockSpec((seqs, heads, HEAD_DK, HEAD_DV), lambda b, h: (b, h, 0, 0))
    s_shape = jax.ShapeDtypeStruct((B, N_HEADS, HEAD_DK, HEAD_DV), F32)
    return pl.pallas_call(
        functools.partial(_gla_kernel, heads, nt, has_init, seqs),
        grid=(B // seqs, hg),
        in_specs=([k_spec, k_spec, dr_spec] * 2 + [dt_spec] * 2 + [v_spec, v_spec] + [s_spec] * (2 if has_init else 0)
                  + [_const_spec((1, HEAD_DV))]),
        out_specs=[v_spec, s_spec, s_spec],
        out_shape=[jax.ShapeDtypeStruct((B, T, VDIM), BF16), s_shape, s_shape],
        scratch_shapes=[pltpu.VMEM((seqs, 2, heads, HEAD_DK, HEAD_DV), F32),
                        pltpu.VMEM((seqs, 2, heads, CHUNKS_PER_TILE, HEAD_DK, HEAD_DV), F32),
                        pltpu.VMEM((seqs, T, heads * HEAD_DV), F32),
                        pltpu.VMEM((seqs, T, heads * HEAD_DV), F32)],
        compiler_params=pltpu.CompilerParams(
            dimension_semantics=("arbitrary", "arbitrary"), vmem_limit_bytes=VMEM_LIMIT_BYTES),
        name="gla",
    )(qdf, kif, drf, qdb, kib, drb, dtf, dtb, v, ga, *(init_states or ()), gla_norm_g)


def _mix_weights_kernel(tab_ref, wf_ref, o_ref):
    for g in range(N_FGROUPS):
        o_ref[g] = jnp.dot(tab_ref[...], wf_ref[g], preferred_element_type=F32,
                           precision=lax.Precision.HIGHEST).astype(BF16)


def _mix_weights(wf):
    n = np.arange(FGROUP_CH)
    ang = 2.0 * np.pi * ((n[:, None] * n[None, :]) % FGROUP_CH) / FGROUP_CH
    tab = jnp.asarray(np.concatenate([np.cos(ang), np.sin(ang)], axis=0) / np.sqrt(FGROUP_CH), F32)
    return pl.pallas_call(
        _mix_weights_kernel,
        out_shape=jax.ShapeDtypeStruct((N_FGROUPS, 2 * FGROUP_CH, FGROUP_CH), BF16),
        name="mix_weights",
    )(tab, wf)


def _mix_and_gate(fp, fq, mix_ref, gb, store):
    fp = fp.astype(BF16)
    fq = fq.astype(BF16)
    for g in range(N_FGROUPS):
        cols = slice(g * FGROUP_CH, (g + 1) * FGROUP_CH)
        z = _mm(jnp.concatenate([fp[:, cols], fq[:, cols]], axis=1), mix_ref[g])
        store(cols, (z * gb(cols).astype(F32)).astype(BF16))


def _fourier_kernel(seqs, ct_ref, st_ref, u_ref, gb_ref, mix_ref, o_ref):
    for bb in range(seqs):
        u = u_ref[bb]

        def store(cols, val, bb=bb):
            o_ref[bb, :, cols] = val

        _mix_and_gate(_mm(ct_ref[...], u), _mm(st_ref[...], u), mix_ref,
                      lambda cols, bb=bb: gb_ref[bb, :, cols], store)


def _fourier(ct, stn, u, gb, mix):
    B, T, _ = u.shape
    tf = min(T, FOURIER_ROWS_PER_STEP)
    seqs = FOURIER_SHORT_SEQS_PER_STEP if T < FOURIER_ROWS_PER_STEP else 1
    assert B % seqs == 0
    tab_spec = pl.BlockSpec((tf, T), lambda b, t: (t, 0))
    seq_spec = pl.BlockSpec((seqs, T, FOURIER_DIM), lambda b, t: (b, 0, 0))
    tile_spec = pl.BlockSpec((seqs, tf, FOURIER_DIM), lambda b, t: (b, t, 0))
    return pl.pallas_call(
        functools.partial(_fourier_kernel, seqs),
        grid=(B // seqs, T // tf),
        in_specs=[tab_spec, tab_spec, seq_spec, tile_spec,
                  _const_spec((N_FGROUPS, 2 * FGROUP_CH, FGROUP_CH))],
        out_specs=tile_spec,
        out_shape=jax.ShapeDtypeStruct((B, T, FOURIER_DIM), BF16),
        compiler_params=pltpu.CompilerParams(
            dimension_semantics=("arbitrary", "arbitrary"), vmem_limit_bytes=VMEM_LIMIT_BYTES),
        name="fourier",
    )(ct, stn, u, gb, mix)


def _fourier_split_kernel(cee_ref, see_ref, ceo_ref, seo_ref, co_ref, so_ref, u_ref, gb_ref, mix_ref, o_ref,
                          wide, u_odd, u_even):
    half, quarter = u_odd.shape[0], u_even.shape[1]
    for g in range(N_FGROUPS):
        cols = slice(g * FGROUP_CH, (g + 1) * FGROUP_CH)
        wide[g] = u_ref[:, cols].astype(F32)
        u_odd[:, cols] = wide[g, pl.ds(1, half, stride=2), :].astype(BF16)
        for r in range(2):
            u_even[r, :, cols] = wide[g, pl.ds(2 * r, quarter, stride=4), :].astype(BF16)

    eep, eeq = _mm(cee_ref[...], u_even[0]), _mm(see_ref[...], u_even[0])
    eop, eoq = _mm(ceo_ref[...], u_even[1]), _mm(seo_ref[...], u_even[1])
    e_blocks = ((eep + eop, eeq + eoq), (eep - eop, eeq - eoq))
    for kb, (ep, eq) in enumerate(e_blocks):
        rows = slice(kb * quarter, (kb + 1) * quarter)
        op, oq = _mm(co_ref[rows, :], u_odd[...]), _mm(so_ref[rows, :], u_odd[...])
        for upper, (fp, fq) in enumerate(((ep + op, eq + oq), (ep - op, eq - oq))):
            def store(cols, val, upper=upper):
                o_ref[upper, rows, cols] = val

            _mix_and_gate(fp, fq, mix_ref, lambda cols, upper=upper: gb_ref[upper, rows, cols], store)


def _fourier_split(u, gb, mix):
    B, T, _ = u.shape
    half, quarter = T // 2, T // 4
    cee, seen = _time_dft_tables(T, quarter, 4 * np.arange(quarter))
    ceo, seon = _time_dft_tables(T, quarter, 4 * np.arange(quarter) + 2)
    co, son = _time_dft_tables(T, half, 2 * np.arange(half) + 1)
    seq_spec = pl.BlockSpec((None, T, FOURIER_DIM), lambda b: (b, 0, 0))
    halves_spec = pl.BlockSpec((None, 2, half, FOURIER_DIM), lambda b: (b, 0, 0, 0))
    out = pl.pallas_call(
        _fourier_split_kernel,
        grid=(B,),
        in_specs=[_const_spec((quarter, quarter))] * 4 + [_const_spec((half, half))] * 2 + [
            seq_spec, halves_spec, _const_spec((N_FGROUPS, 2 * FGROUP_CH, FGROUP_CH))],
        out_specs=halves_spec,
        out_shape=jax.ShapeDtypeStruct((B, 2, half, FOURIER_DIM), BF16),
        scratch_shapes=[pltpu.VMEM((N_FGROUPS, T, FGROUP_CH), F32),
                        pltpu.VMEM((half, FOURIER_DIM), BF16),
                        pltpu.VMEM((2, quarter, FOURIER_DIM), BF16)],
        compiler_params=pltpu.CompilerParams(
            dimension_semantics=("arbitrary",), vmem_limit_bytes=VMEM_LIMIT_BYTES),
        name="fourier_split",
    )(cee, seen, ceo, seon, co, son, u, gb.reshape(B, 2, half, FOURIER_DIM), mix)
    return out.reshape(B, T, FOURIER_DIM)


def _out_proj_kernel(mod, og_ref, fg_ref, sa_ref, sb_ref, x_ref, gate_ref, fng_ref, wpa_ref, wpb_ref, wo_ref,
                     y_ref):
    gate_ref = _mod_row(mod, gate_ref)
    for i in range(OUT_TILES_PER_STEP):
        rows = slice(i * TOKEN_TILE, (i + 1) * TOKEN_TILE)
        ya = _mm(og_ref[rows, :], wpa_ref[...])
        yb = _mm(fg_ref[rows, :], wpb_ref[...])
        merged = sa_ref[rows, :].astype(F32) * ya + sb_ref[rows, :].astype(F32) * yb
        xo = x_ref[rows, :] + gate_ref[...] * _mm(merged.astype(BF16), wo_ref[...])
        y_ref[rows, :] = xo * lax.rsqrt(jnp.mean(xo * xo, axis=-1, keepdims=True) + EPS) * fng_ref[...]


def _out_proj(og, fg, sa, sb, x, ada, mod, final_norm_g, wpa, wpb, wo):
    B, T, _ = x.shape
    step_rows = OUT_TILES_PER_STEP * TOKEN_TILE
    if T < step_rows:
        assert not mod[1] and (B * T) % step_rows == 0
        fold = lambda a: a.reshape(B * T // step_rows, step_rows, a.shape[-1])
        y = _out_proj(fold(og), fold(fg), fold(sa), fold(sb), fold(x), ada, mod, final_norm_g, wpa, wpb, wo)
        return y.reshape(B, T, D_MODEL)

    def tok_spec(cols):
        return pl.BlockSpec((None, step_rows, cols), lambda b, t: (b, t, 0))

    return pl.pallas_call(
        functools.partial(_out_proj_kernel, mod),
        grid=(B, T // step_rows),
        in_specs=[tok_spec(VDIM), tok_spec(FOURIER_DIM), tok_spec(D_MODEL), tok_spec(D_MODEL), tok_spec(D_MODEL),
                  _mod_spec(mod, _ADA_GATE), _const_spec((1, D_MODEL)),
                  _const_spec((VDIM, D_MODEL)), _const_spec((FOURIER_DIM, D_MODEL)), _const_spec((D_MODEL, D_MODEL))],
        out_specs=tok_spec(D_MODEL),
        out_shape=jax.ShapeDtypeStruct((B, T, D_MODEL), F32),
        compiler_params=pltpu.CompilerParams(
            dimension_semantics=("arbitrary", "arbitrary"), vmem_limit_bytes=VMEM_LIMIT_BYTES),
        name="out_proj",
    )(og, fg, sa, sb, x, ada, final_norm_g, wpa, wpb, wo)


def _time_dft_tables(T, n_rows, positions):
    n = np.asarray(positions)
    ang = 2.0 * np.pi * ((np.arange(n_rows)[:, None] * n[None, :]) % T) / T
    scale = 1.0 / np.sqrt(T)
    return jnp.asarray(np.cos(ang) * scale, F32).astype(BF16), jnp.asarray(-np.sin(ang) * scale, F32).astype(BF16)


def _rope_tables(T):
    rows = T // GRID_W
    r = np.repeat(np.arange(rows), GRID_W).astype(np.float64)
    c = np.tile(np.arange(GRID_W), rows).astype(np.float64)
    n_freq = HEAD_DK // 4
    freqs = ROPE_BASE ** (-np.arange(n_freq, dtype=np.float64) / n_freq)
    ang_r = r[:, None] * freqs
    ang_c = c[:, None] * freqs
    cos = np.concatenate([np.cos(ang_r), np.cos(ang_r), np.cos(ang_c), np.cos(ang_c)], axis=-1)
    sin = np.concatenate([-np.sin(ang_r), np.sin(ang_r), -np.sin(ang_c), np.sin(ang_c)], axis=-1)
    return jnp.asarray(cos, F32), jnp.asarray(sin, F32)


def _path(x, ada, mod, init_states, rope, wts):
    T = x.shape[1]
    (qdf, kif, drf, qdb, kib, drb, dtf, dtb, v, ga, u, gb, sa, sb) = _in_proj(
        x, ada, mod, wts["norm_g"], wts["w_in"], wts["w_tail"], wts["w_ab"], wts["b_ab"], rope)
    og, sf, sbw = _gla(qdf, kif, drf, qdb, kib, drb, dtf, dtb, v, ga, init_states, wts["gla_norm_g"])
    if T >= FOURIER_SPLIT_MIN_LEN:
        fg = _fourier_split(u, gb, wts["four_mix"])
    else:
        ct, stn = _time_dft_tables(T, T, np.arange(T))
        fg = _fourier(ct, stn, u, gb, wts["four_mix"])
    y = _out_proj(og, fg, sa, sb, x, ada, mod, wts["final_norm_g"], wts["w_proj_a"], wts["w_proj_b"], wts["w_out"])
    return y, sf, sbw


def kernel(x_prompt, x_sample, state_gla_fwd, state_gla_bwd, c, c_ctx, w_ada, b_ada, norm_g, w_in,
           w_alpha_fwd, b_alpha_fwd, w_alpha_bwd, b_alpha_bwd, gla_norm_g, w_four, w_proj_a, w_proj_b,
           w_out, final_norm_g):
    depth = w_in.shape[0]
    assert depth == 1, "single trunk layer"
    bs = x_sample.shape[0]

    n_cond = bs + 1
    cond_rows = -(-n_cond // SUBLANES) * SUBLANES
    cond = jnp.concatenate([c, c_ctx[None, :], jnp.zeros((cond_rows - n_cond, D_MODEL), F32)], axis=0)
    ada = _ada(cond, w_ada[0], b_ada[0][None, :])

    wi = w_in[0]
    o_u = _W_HEAD_COLS + 2 * GATE_RANK
    w_ab = jnp.zeros((2 * GATE_RANK, 2 * KDIM), F32)
    w_ab = w_ab.at[:GATE_RANK, :KDIM].set(w_alpha_fwd[0]).at[GATE_RANK:, KDIM:].set(w_alpha_bwd[0])
    wts = dict(
        norm_g=norm_g[0][None, :], w_in=wi.astype(BF16), w_tail=wi[:, o_u:].astype(BF16),
        w_ab=w_ab.astype(BF16),
        b_ab=jnp.concatenate([b_alpha_fwd[0], b_alpha_bwd[0]])[None, :],
        gla_norm_g=gla_norm_g[0][None, :], four_mix=_mix_weights(w_four[0]),
        w_proj_a=w_proj_a[0].astype(BF16), w_proj_b=w_proj_b[0].astype(BF16), w_out=w_out[0].astype(BF16),
        final_norm_g=final_norm_g[None, :])

    y_prompt, sf, sb = _path(x_prompt, ada, (bs, False), None, None, wts)
    y_sample, _, _ = _path(x_sample, ada, (0, True),
                           (state_gla_fwd[:, 0], state_gla_bwd[:, 0]), _rope_tables(x_sample.shape[1]), wts)
    return (y_prompt, y_sample, sf[:, None].astype(x_prompt.dtype), sb[:, None].astype(x_prompt.dtype))
```

```python
import functools

import numpy as np
import jax
import jax.numpy as jnp
from jax import lax
from jax.experimental import pallas as pl
from jax.experimental.pallas import tpu as pltpu

F32 = jnp.float32
BF16 = jnp.bfloat16

D_MODEL = 1024
N_HEADS = 4
HEAD_DK = 128
HEAD_DV = 256
KDIM = N_HEADS * HEAD_DK
VDIM = N_HEADS * HEAD_DV
GATE_RANK = 16
GATE_NORM = 16.0
CHUNK = 64
N_FGROUPS = 4
FGROUP_CH = 128
FOURIER_DIM = N_FGROUPS * FGROUP_CH
GRID_W = 64
ROPE_BASE = 10000.0
EPS = 1e-6
LOG2_E = float(np.log2(np.e))

LANES = 128
SUBLANES = 8
MXU_COLS = 256
TOKEN_TILE = 256
CHUNKS_PER_TILE = TOKEN_TILE // CHUNK
IN_TILES_PER_STEP = 2
OUT_TILES_PER_STEP = 4
GLA_SHORT_SEQS_PER_STEP = 4
GLA_LONG_SEQ_HEADS_PER_STEP = 2
GLA_TILE_UNROLL = 4
FOURIER_ROWS_PER_STEP = 512
FOURIER_SHORT_SEQS_PER_STEP = 8
FOURIER_SPLIT_MIN_LEN = 2048
VMEM_LIMIT_BYTES = 56 * 1024 * 1024

_OFF_QK = 0
_OFF_V = _OFF_QK + 2 * KDIM
_OFF_GA = _OFF_V + VDIM
_W_HEAD_COLS = _OFF_GA + VDIM
_OFF_U = 0
_OFF_GB = _OFF_U + FOURIER_DIM
_OFF_MA = _OFF_GB + FOURIER_DIM
_OFF_MB = _OFF_MA + D_MODEL
_W_TAIL_COLS = _OFF_MB + D_MODEL


def _mm(a, b):
    return jnp.dot(a, b, preferred_element_type=F32)


def _mm_ta(a, b):
    return lax.dot_general(a, b, (((0,), (0,)), ((), ())), preferred_element_type=F32)


def _mm_tb(a, b):
    return lax.dot_general(a, b, (((1,), (1,)), ((), ())), preferred_element_type=F32)


def _mm_ta_tb(a, b):
    return lax.dot_general(a, b, (((0,), (1,)), ((), ())), preferred_element_type=F32)


def _split_bf16(x):
    hi = x.astype(BF16)
    lo = (x - hi.astype(F32)).astype(BF16)
    return hi, lo


def _const_spec(shape):
    nd = len(shape)
    return pl.BlockSpec(shape, lambda *_: (0,) * nd)


_ADA_SHIFT, _ADA_SCALE, _ADA_GATE = range(3)


def _mod_spec(mod, col_block):
    row0, per_seq = mod
    if per_seq:
        return pl.BlockSpec((SUBLANES, D_MODEL), lambda b, t: ((row0 + b) // SUBLANES, col_block))
    return pl.BlockSpec((SUBLANES, D_MODEL), lambda b, t: (row0 // SUBLANES, col_block))


def _mod_row(mod, ref):
    row0, per_seq = mod
    row = (row0 + pl.program_id(0)) % SUBLANES if per_seq else row0 % SUBLANES
    return ref.at[pl.ds(row, 1)]


def _ada_kernel(c_ref, w_ref, b_ref, o_ref):
    c = c_ref[...]
    s = c * jax.nn.sigmoid(c)
    o_ref[...] = _mm(s.astype(BF16), w_ref[...].astype(BF16)) + b_ref[...]


def _ada(cond, w, b):
    rows = cond.shape[0]
    n_blocks = w.shape[1] // D_MODEL
    return pl.pallas_call(
        _ada_kernel,
        grid=(n_blocks,),
        in_specs=[pl.BlockSpec((rows, D_MODEL), lambda n: (0, 0)),
                  pl.BlockSpec((D_MODEL, D_MODEL), lambda n: (0, n)),
                  pl.BlockSpec((1, D_MODEL), lambda n: (0, n))],
        out_specs=pl.BlockSpec((rows, D_MODEL), lambda n: (0, n)),
        out_shape=jax.ShapeDtypeStruct((rows, n_blocks * D_MODEL), F32),
        compiler_params=pltpu.CompilerParams(
            dimension_semantics=("arbitrary",), vmem_limit_bytes=VMEM_LIMIT_BYTES),
        name="ada",
    )(cond, w, b)


def _in_proj_kernel(use_rope, mod, *refs):
    n_shared = 8
    n_in = 1 + n_shared + (2 if use_rope else 0)
    per_tile_slots = (n_in + 2, n_in + 5, n_in + 6, n_in + 7)
    tiles = []
    for i in range(IN_TILES_PER_STEP):
        rows = pl.ds(i * TOKEN_TILE, TOKEN_TILE)
        tile_refs = []
        for idx, ref in enumerate(refs):
            if idx in (1, 2):
                tile_refs.append(_mod_row(mod, ref))
            elif 1 <= idx <= n_shared:
                tile_refs.append(ref)
            elif idx in per_tile_slots:
                tile_refs.append(ref.at[i])
            else:
                tile_refs.append(ref.at[rows])
        tiles.append(_in_proj_tile(use_rope, *tile_refs))
    for _ in range(_IN_PROJ_STAGES):
        for tile in tiles:
            next(tile)


def _project(hb, w_ref, off, width, act, out_ref):
    for n in range(0, width, MXU_COLS):
        z = _mm(hb, w_ref[:, off + n:off + n + MXU_COLS])
        out_ref[:, n:n + MXU_COLS] = act(z).astype(BF16)


_IN_PROJ_STAGES = 3


def _in_proj_tile(use_rope, *refs):
    if use_rope:
        (x_ref, sc_ref, sh_ref, ng_ref, wa_ref, wb_ref, wr_ref, wab_ref, bab_ref, cos_ref, sin_ref,
         qdf_ref, kif_ref, drf_ref, qdb_ref, kib_ref, drb_ref, dtf_ref, dtb_ref,
         v_ref, ga_ref, u_ref, gb_ref, sa_ref, sb_ref) = refs
    else:
        (x_ref, sc_ref, sh_ref, ng_ref, wa_ref, wb_ref, wr_ref, wab_ref, bab_ref,
         qdf_ref, kif_ref, drf_ref, qdb_ref, kib_ref, drb_ref, dtf_ref, dtb_ref,
         v_ref, ga_ref, u_ref, gb_ref, sa_ref, sb_ref) = refs

    x = x_ref[...]
    xn = x * lax.rsqrt(jnp.mean(x * x, axis=-1, keepdims=True) + EPS)
    h = xn * (ng_ref[...] * (1.0 + sc_ref[...])) + sh_ref[...]
    hb = h.astype(BF16)

    r_t = _mm_ta_tb(wr_ref[:, :2 * GATE_RANK], hb)
    qk = _mm(hb, wa_ref[:, _OFF_QK:_OFF_QK + 2 * KDIM])
    xg = _mm_ta(r_t.astype(BF16), wab_ref[...]) + bab_ref[...]
    yield

    _project(hb, wa_ref, _OFF_V, VDIM, lambda z: z, v_ref)
    g_all = (jnp.minimum(xg, 0.0) - jnp.log(1.0 + jnp.exp(-jnp.abs(xg)))) * (LOG2_E / GATE_NORM)
    row = lax.broadcasted_iota(jnp.int32, (CHUNK, 2 * CHUNK), 0)
    col = lax.broadcasted_iota(jnp.int32, (CHUNK, 2 * CHUNK), 1) % CHUNK
    bcs = []
    for direction in range(2):
        g_hi, g_lo = _split_bf16(g_all[:, direction * KDIM:(direction + 1) * KDIM])
        tri = jnp.where((col <= row) if direction == 0 else (col >= row), 1.0, 0.0).astype(BF16)
        bc_chunks = []
        for c in range(CHUNKS_PER_TILE):
            cr = slice(c * CHUNK, (c + 1) * CHUNK)
            bc_chunks.append(_mm(tri, jnp.concatenate([g_hi[cr], g_lo[cr]], axis=0)))
        bcs.append(jnp.concatenate(bc_chunks, axis=0))
    yield

    q = qk[:, :KDIM] * (HEAD_DK ** -0.5)
    k = qk[:, KDIM:]
    if use_rope:
        cos = cos_ref[...]
        sin = sin_ref[...]
        lane = lax.broadcasted_iota(jnp.int32, (TOKEN_TILE, HEAD_DK), 1)
        first_half = (lane // (HEAD_DK // 4)) % 2 == 0

        def rope(t):
            outs = []
            for hh in range(N_HEADS):
                th = t[:, hh * HEAD_DK:(hh + 1) * HEAD_DK]
                partner = jnp.where(first_half,
                                    pltpu.roll(th, HEAD_DK - HEAD_DK // 4, axis=1),
                                    pltpu.roll(th, HEAD_DK // 4, axis=1))
                outs.append(th * cos + partner * sin)
            return jnp.concatenate(outs, axis=1)

        q = rope(q)
        k = rope(k)

    for direction, (qd_ref, ki_ref, dr_ref, dt_ref) in enumerate(
            ((qdf_ref, kif_ref, drf_ref, dtf_ref), (qdb_ref, kib_ref, drb_ref, dtb_ref))):
        bc = bcs[direction]
        edge = CHUNK - 1 if direction == 0 else 0
        bl_rows = bc.reshape(CHUNKS_PER_TILE, CHUNK, KDIM)[:, edge, :]
        qd_ref[...] = (q * jnp.exp2(bc)).astype(BF16)
        ki_ref[...] = (k * jnp.exp2(-bc)).astype(BF16)
        dr = jnp.exp2(jnp.concatenate([bl_rows, jnp.zeros((SUBLANES - CHUNKS_PER_TILE, KDIM), F32)], axis=0))
        dr_ref[...] = dr
        dt_ref[...] = jnp.concatenate([dr, jnp.zeros((LANES - SUBLANES, KDIM), F32)], axis=0).T

    _project(hb, wa_ref, _OFF_GA, VDIM, jax.nn.silu, ga_ref)
    _project(hb, wb_ref, _OFF_U, FOURIER_DIM, lambda z: z, u_ref)
    _project(hb, wb_ref, _OFF_GB, FOURIER_DIM, jax.nn.silu, gb_ref)
    _project(hb, wb_ref, _OFF_MA, D_MODEL, jax.nn.sigmoid, sa_ref)
    _project(hb, wb_ref, _OFF_MB, D_MODEL, jax.nn.sigmoid, sb_ref)
    yield


def _in_proj(x, ada, mod, norm_g, w_in, w_tail, w_ab, b_ab, rope):
    B, T, _ = x.shape
    use_rope = rope is not None
    step_rows = IN_TILES_PER_STEP * TOKEN_TILE
    if T < step_rows:
        assert not mod[1] and not use_rope and (B * T) % step_rows == 0
        outs = _in_proj(x.reshape(B * T // step_rows, step_rows, D_MODEL), ada, mod, norm_g,
                        w_in, w_tail, w_ab, b_ab, rope)
        return [o.reshape((B, T // TOKEN_TILE) + o.shape[2:]) if o.ndim == 4 else o.reshape(B, T, o.shape[-1])
                for o in outs]
    nt = T // TOKEN_TILE

    def tok_spec(cols):
        return pl.BlockSpec((None, step_rows, cols), lambda b, t: (b, t, 0))

    assert _W_HEAD_COLS % LANES == 0 and 2 * GATE_RANK <= LANES
    in_specs = [
        tok_spec(D_MODEL), _mod_spec(mod, _ADA_SCALE), _mod_spec(mod, _ADA_SHIFT), _const_spec((1, D_MODEL)),
        _const_spec((D_MODEL, _W_HEAD_COLS)), _const_spec((D_MODEL, _W_TAIL_COLS)),
        pl.BlockSpec((D_MODEL, LANES), lambda b, t: (0, _W_HEAD_COLS // LANES)),
        _const_spec((2 * GATE_RANK, 2 * KDIM)), _const_spec((1, 2 * KDIM)),
    ]
    args = [x, ada, ada, norm_g, w_in, w_tail, w_in, w_ab, b_ab]
    if use_rope:
        in_specs += [pl.BlockSpec((step_rows, HEAD_DK), lambda b, t: (t, 0))] * 2
        args += list(rope)

    dt_spec = pl.BlockSpec((None, IN_TILES_PER_STEP, KDIM, LANES), lambda b, t: (b, t, 0, 0))
    tok_bf = lambda cols: jax.ShapeDtypeStruct((B, T, cols), BF16)
    dt_shape = jax.ShapeDtypeStruct((B, nt, KDIM, LANES), F32)
    dr_spec = pl.BlockSpec((None, IN_TILES_PER_STEP, SUBLANES, KDIM), lambda b, t: (b, t, 0, 0))
    dr_shape = jax.ShapeDtypeStruct((B, nt, SUBLANES, KDIM), F32)
    out_specs = [tok_spec(KDIM), tok_spec(KDIM), dr_spec] * 2 + [dt_spec] * 2 + [
        tok_spec(VDIM), tok_spec(VDIM), tok_spec(FOURIER_DIM),
        tok_spec(FOURIER_DIM), tok_spec(D_MODEL), tok_spec(D_MODEL)]
    out_shape = [tok_bf(KDIM), tok_bf(KDIM), dr_shape] * 2 + [dt_shape] * 2 + [
        tok_bf(VDIM), tok_bf(VDIM), tok_bf(FOURIER_DIM),
        tok_bf(FOURIER_DIM), tok_bf(D_MODEL), tok_bf(D_MODEL)]
    return pl.pallas_call(
        functools.partial(_in_proj_kernel, use_rope, mod),
        grid=(B, T // step_rows),
        in_specs=in_specs,
        out_specs=out_specs,
        out_shape=out_shape,
        compiler_params=pltpu.CompilerParams(
            dimension_semantics=("arbitrary", "arbitrary"), vmem_limit_bytes=VMEM_LIMIT_BYTES),
        name="in_proj_rope" if use_rope else "in_proj",
    )(*args)


def _gla_kernel(heads, n_tiles, has_init, seqs, *refs):
    gain_pos = 10 + (2 if has_init else 0)
    for bb in range(seqs):
        _gla_seq(heads, n_tiles, has_init, *[r if i == gain_pos else r.at[bb] for i, r in enumerate(refs)])


def _gla_seq(heads, n_tiles, has_init, *refs):
    (qdf_ref, kif_ref, drf_ref, qdb_ref, kib_ref, drb_ref, dtf_ref, dtb_ref, v_ref, ga_ref) = refs[:10]
    refs = refs[10:]
    if has_init:
        s0f_ref, s0b_ref = refs[:2]
        refs = refs[2:]
    gn_ref, og_ref, sf_ref, sb_ref, state, d_state, of_acc, ob_acc = refs
    if has_init:
        state[0] = s0f_ref[...]
        state[1] = s0b_ref[...]
    else:
        state[...] = jnp.zeros_like(state)
    dir_refs = ((qdf_ref, kif_ref, drf_ref, dtf_ref, of_acc), (qdb_ref, kib_ref, drb_ref, dtb_ref, ob_acc))

    def tile_body(finalize, j, carry):
        row = lax.broadcasted_iota(jnp.int32, (TOKEN_TILE, TOKEN_TILE), 0)
        col = lax.broadcasted_iota(jnp.int32, (TOKEN_TILE, TOKEN_TILE), 1)
        same_chunk = (row // CHUNK) == (col // CHUNK)
        causal = (same_chunk & (col <= row), same_chunk & (col >= row))
        tiles = (j, n_tiles - 1 - j)
        bases = tuple(pl.multiple_of(t * TOKEN_TILE, TOKEN_TILE) for t in tiles)

        scores = {}
        for direction in range(2):
            qd_ref, ki_ref, dr_ref, _, o_acc = dir_refs[direction]
            trows = pl.ds(bases[direction], TOKEN_TILE)
            for hh in range(heads):
                kc = slice(hh * HEAD_DK, (hh + 1) * HEAD_DK)
                vc = slice(hh * HEAD_DV, (hh + 1) * HEAD_DV)
                scores[direction, hh] = jnp.where(
                    causal[direction], _mm_tb(qd_ref[trows, kc], ki_ref[trows, kc]), 0.0).astype(BF16)
                chunk_decay = dr_ref[tiles[direction], :, kc]
                for c in range(CHUNKS_PER_TILE):
                    rows = pl.ds(bases[direction] + c * CHUNK, CHUNK)
                    ke = (ki_ref[rows, kc].astype(F32) * chunk_decay[c:c + 1, :]).astype(BF16)
                    d_state[direction, hh, c] = _mm_ta(ke, v_ref[rows, vc])

        steps = [(step, direction) for step in range(CHUNKS_PER_TILE) for direction in range(2)]
        if finalize[0] != finalize[1]:
            steps.sort(key=lambda sd: finalize[sd[1]])
        for step, direction in steps:
            qd_ref, _, _, dt_ref, o_acc = dir_refs[direction]
            other_acc = dir_refs[1 - direction][4]
            c = step if direction == 0 else CHUNKS_PER_TILE - 1 - step
            rows = pl.ds(bases[direction] + c * CHUNK, CHUNK)
            for hh in range(heads):
                kc = slice(hh * HEAD_DK, (hh + 1) * HEAD_DK)
                vc = slice(hh * HEAD_DV, (hh + 1) * HEAD_DV)
                s = state[direction, hh]
                blk = (c * CHUNK) // LANES
                sc = scores[direction, hh][c * CHUNK:(c + 1) * CHUNK, blk * LANES:(blk + 1) * LANES]
                v_blk = v_ref[pl.ds(bases[direction] + blk * LANES, LANES), vc]
                o = _mm(jnp.concatenate([qd_ref[rows, kc], sc], axis=1),
                        jnp.concatenate([s.astype(BF16), v_blk], axis=0))
                if finalize[direction]:
                    o = o + other_acc[rows, vc]
                    on = o * lax.rsqrt(jnp.mean(o * o, axis=-1, keepdims=True) + EPS) * gn_ref[...]
                    og_ref[rows, vc] = (on * ga_ref[rows, vc].astype(F32)).astype(BF16)
                else:
                    o_acc[rows, vc] = o
                dec = dt_ref[tiles[direction], kc, c:c + 1]
                state[direction, hh] = dec * s + d_state[direction, hh, c]
        return carry

    first_half = n_tiles // 2
    unroll = GLA_TILE_UNROLL if first_half % GLA_TILE_UNROLL == 0 else 1
    lax.fori_loop(0, first_half, functools.partial(tile_body, (False, False)), 0, unroll=unroll)
    if n_tiles % 2:
        tile_body((False, True), first_half, 0)
    lax.fori_loop(n_tiles - first_half, n_tiles, functools.partial(tile_body, (True, True)), 0, unroll=unroll)
    sf_ref[...] = state[0]
    sb_ref[...] = state[1]


def _gla(qdf, kif, drf, qdb, kib, drb, dtf, dtb, v, ga, init_states, gla_norm_g):
    B, T, _ = v.shape
    nt = T // TOKEN_TILE
    has_init = init_states is not None
    seqs = GLA_SHORT_SEQS_PER_STEP if nt == 1 else 1
    assert B % seqs == 0
    heads = N_HEADS if nt == 1 else GLA_LONG_SEQ_HEADS_PER_STEP
    hg = N_HEADS // heads
    k_spec = pl.BlockSpec((seqs, T, heads * HEAD_DK), lambda b, h: (b, 0, h))
    v_spec = pl.BlockSpec((seqs, T, heads * HEAD_DV), lambda b, h: (b, 0, h))
    dt_spec = pl.BlockSpec((seqs, nt, heads * HEAD_DK, LANES), lambda b, h: (b, 0, h, 0))
    dr_spec = pl.BlockSpec((seqs, nt, SUBLANES, heads * HEAD_DK), lambda b, h: (b, 0, 0, h))
    s_spec = pl.BlockSpec((seqs, heads, HEAD_DK, HEAD_DV), lambda b, h: (b, h, 0, 0))
    s_shape = jax.ShapeDtypeStruct((B, N_HEADS, HEAD_DK, HEAD_DV), F32)
    return pl.pallas_call(
        functools.partial(_gla_kernel, heads, nt, has_init, seqs),
        grid=(B // seqs, hg),
        in_specs=([k_spec, k_spec, dr_spec] * 2 + [dt_spec] * 2 + [v_spec, v_spec] + [s_spec] * (2 if has_init else 0)
                  + [_const_spec((1, HEAD_DV))]),
        out_specs=[v_spec, s_spec, s_spec],
        out_shape=[jax.ShapeDtypeStruct((B, T, VDIM), BF16), s_shape, s_shape],
        scratch_shapes=[pltpu.VMEM((seqs, 2, heads, HEAD_DK, HEAD_DV), F32),
                        pltpu.VMEM((seqs, 2, heads, CHUNKS_PER_TILE, HEAD_DK, HEAD_DV), F32),
                        pltpu.VMEM((seqs, T, heads * HEAD_DV), F32),
                        pltpu.VMEM((seqs, T, heads * HEAD_DV), F32)],
        compiler_params=pltpu.CompilerParams(
            dimension_semantics=("arbitrary", "arbitrary"), vmem_limit_bytes=VMEM_LIMIT_BYTES),
        name="gla",
    )(qdf, kif, drf, qdb, kib, drb, dtf, dtb, v, ga, *(init_states or ()), gla_norm_g)


def _mix_weights_kernel(tab_ref, wf_ref, o_ref):
    for g in range(N_FGROUPS):
        o_ref[g] = jnp.dot(tab_ref[...], wf_ref[g], preferred_element_type=F32,
                           precision=lax.Precision.HIGHEST).astype(BF16)


def _mix_weights(wf):
    n = np.arange(FGROUP_CH)
    ang = 2.0 * np.pi * ((n[:, None] * n[None, :]) % FGROUP_CH) / FGROUP_CH
    tab = jnp.asarray(np.concatenate([np.cos(ang), np.sin(ang)], axis=0) / np.sqrt(FGROUP_CH), F32)
    return pl.pallas_call(
        _mix_weights_kernel,
        out_shape=jax.ShapeDtypeStruct((N_FGROUPS, 2 * FGROUP_CH, FGROUP_CH), BF16),
        name="mix_weights",
    )(tab, wf)


def _mix_and_gate(fp, fq, mix_ref, gb, store):
    fp = fp.astype(BF16)
    fq = fq.astype(BF16)
    for g in range(N_FGROUPS):
        cols = slice(g * FGROUP_CH, (g + 1) * FGROUP_CH)
        z = _mm(jnp.concatenate([fp[:, cols], fq[:, cols]], axis=1), mix_ref[g])
        store(cols, (z * gb(cols).astype(F32)).astype(BF16))


def _fourier_kernel(seqs, ct_ref, st_ref, u_ref, gb_ref, mix_ref, o_ref):
    for bb in range(seqs):
        u = u_ref[bb]

        def store(cols, val, bb=bb):
            o_ref[bb, :, cols] = val

        _mix_and_gate(_mm(ct_ref[...], u), _mm(st_ref[...], u), mix_ref,
                      lambda cols, bb=bb: gb_ref[bb, :, cols], store)


def _fourier(ct, stn, u, gb, mix):
    B, T, _ = u.shape
    tf = min(T, FOURIER_ROWS_PER_STEP)
    seqs = FOURIER_SHORT_SEQS_PER_STEP if T < FOURIER_ROWS_PER_STEP else 1
    assert B % seqs == 0
    tab_spec = pl.BlockSpec((tf, T), lambda b, t: (t, 0))
    seq_spec = pl.BlockSpec((seqs, T, FOURIER_DIM), lambda b, t: (b, 0, 0))
    tile_spec = pl.BlockSpec((seqs, tf, FOURIER_DIM), lambda b, t: (b, t, 0))
    return pl.pallas_call(
        functools.partial(_fourier_kernel, seqs),
        grid=(B // seqs, T // tf),
        in_specs=[tab_spec, tab_spec, seq_spec, tile_spec,
                  _const_spec((N_FGROUPS, 2 * FGROUP_CH, FGROUP_CH))],
        out_specs=tile_spec,
        out_shape=jax.ShapeDtypeStruct((B, T, FOURIER_DIM), BF16),
        compiler_params=pltpu.CompilerParams(
            dimension_semantics=("arbitrary", "arbitrary"), vmem_limit_bytes=VMEM_LIMIT_BYTES),
        name="fourier",
    )(ct, stn, u, gb, mix)


def _fourier_split_kernel(cee_ref, see_ref, ceo_ref, seo_ref, co_ref, so_ref, u_ref, gb_ref, mix_ref, o_ref,
                          wide, u_odd, u_even):
    half, quarter = u_odd.shape[0], u_even.shape[1]
    for g in range(N_FGROUPS):
        cols = slice(g * FGROUP_CH, (g + 1) * FGROUP_CH)
        wide[g] = u_ref[:, cols].astype(F32)
        u_odd[:, cols] = wide[g, pl.ds(1, half, stride=2), :].astype(BF16)
        for r in range(2):
            u_even[r, :, cols] = wide[g, pl.ds(2 * r, quarter, stride=4), :].astype(BF16)

    eep, eeq = _mm(cee_ref[...], u_even[0]), _mm(see_ref[...], u_even[0])
    eop, eoq = _mm(ceo_ref[...], u_even[1]), _mm(seo_ref[...], u_even[1])
    e_blocks = ((eep + eop, eeq + eoq), (eep - eop, eeq - eoq))
    for kb, (ep, eq) in enumerate(e_blocks):
        rows = slice(kb * quarter, (kb + 1) * quarter)
        op, oq = _mm(co_ref[rows, :], u_odd[...]), _mm(so_ref[rows, :], u_odd[...])
        for upper, (fp, fq) in enumerate(((ep + op, eq + oq), (ep - op, eq - oq))):
            def store(cols, val, upper=upper):
                o_ref[upper, rows, cols] = val

            _mix_and_gate(fp, fq, mix_ref, lambda cols, upper=upper: gb_ref[upper, rows, cols], store)


def _fourier_split(u, gb, mix):
    B, T, _ = u.shape
    half, quarter = T // 2, T // 4
    cee, seen = _time_dft_tables(T, quarter, 4 * np.arange(quarter))
    ceo, seon = _time_dft_tables(T, quarter, 4 * np.arange(quarter) + 2)
    co, son = _time_dft_tables(T, half, 2 * np.arange(half) + 1)
    seq_spec = pl.BlockSpec((None, T, FOURIER_DIM), lambda b: (b, 0, 0))
    halves_spec = pl.BlockSpec((None, 2, half, FOURIER_DIM), lambda b: (b, 0, 0, 0))
    out = pl.pallas_call(
        _fourier_split_kernel,
        grid=(B,),
        in_specs=[_const_spec((quarter, quarter))] * 4 + [_const_spec((half, half))] * 2 + [
            seq_spec, halves_spec, _const_spec((N_FGROUPS, 2 * FGROUP_CH, FGROUP_CH))],
        out_specs=halves_spec,
        out_shape=jax.ShapeDtypeStruct((B, 2, half, FOURIER_DIM), BF16),
        scratch_shapes=[pltpu.VMEM((N_FGROUPS, T, FGROUP_CH), F32),
                        pltpu.VMEM((half, FOURIER_DIM), BF16),
                        pltpu.VMEM((2, quarter, FOURIER_DIM), BF16)],
        compiler_params=pltpu.CompilerParams(
            dimension_semantics=("arbitrary",), vmem_limit_bytes=VMEM_LIMIT_BYTES),
        name="fourier_split",
    )(cee, seen, ceo, seon, co, son, u, gb.reshape(B, 2, half, FOURIER_DIM), mix)
    return out.reshape(B, T, FOURIER_DIM)


def _out_proj_kernel(mod, og_ref, fg_ref, sa_ref, sb_ref, x_ref, gate_ref, fng_ref, wpa_ref, wpb_ref, wo_ref,
                     y_ref):
    gate_ref = _mod_row(mod, gate_ref)
    for i in range(OUT_TILES_PER_STEP):
        rows = slice(i * TOKEN_TILE, (i + 1) * TOKEN_TILE)
        ya = _mm(og_ref[rows, :], wpa_ref[...])
        yb = _mm(fg_ref[rows, :], wpb_ref[...])
        merged = sa_ref[rows, :].astype(F32) * ya + sb_ref[rows, :].astype(F32) * yb
        xo = x_ref[rows, :] + gate_ref[...] * _mm(merged.astype(BF16), wo_ref[...])
        y_ref[rows, :] = xo * lax.rsqrt(jnp.mean(xo * xo, axis=-1, keepdims=True) + EPS) * fng_ref[...]


def _out_proj(og, fg, sa, sb, x, ada, mod, final_norm_g, wpa, wpb, wo):
    B, T, _ = x.shape
    step_rows = OUT_TILES_PER_STEP * TOKEN_TILE
    if T < step_rows:
        assert not mod[1] and (B * T) % step_rows == 0
        fold = lambda a: a.reshape(B * T // step_rows, step_rows, a.shape[-1])
        y = _out_proj(fold(og), fold(fg), fold(sa), fold(sb), fold(x), ada, mod, final_norm_g, wpa, wpb, wo)
        return y.reshape(B, T, D_MODEL)

    def tok_spec(cols):
        return pl.BlockSpec((None, step_rows, cols), lambda b, t: (b, t, 0))

    return pl.pallas_call(
        functools.partial(_out_proj_kernel, mod),
        grid=(B, T // step_rows),
        in_specs=[tok_spec(VDIM), tok_spec(FOURIER_DIM), tok_spec(D_MODEL), tok_spec(D_MODEL), tok_spec(D_MODEL),
                  _mod_spec(mod, _ADA_GATE), _const_spec((1, D_MODEL)),
                  _const_spec((VDIM, D_MODEL)), _const_spec((FOURIER_DIM, D_MODEL)), _const_spec((D_MODEL, D_MODEL))],
        out_specs=tok_spec(D_MODEL),
        out_shape=jax.ShapeDtypeStruct((B, T, D_MODEL), F32),
        compiler_params=pltpu.CompilerParams(
            dimension_semantics=("arbitrary", "arbitrary"), vmem_limit_bytes=VMEM_LIMIT_BYTES),
        name="out_proj",
    )(og, fg, sa, sb, x, ada, final_norm_g, wpa, wpb, wo)


def _time_dft_tables(T, n_rows, positions):
    n = np.asarray(positions)
    ang = 2.0 * np.pi * ((np.arange(n_rows)[:, None] * n[None, :]) % T) / T
    scale = 1.0 / np.sqrt(T)
    return jnp.asarray(np.cos(ang) * scale, F32).astype(BF16), jnp.asarray(-np.sin(ang) * scale, F32).astype(BF16)


def _rope_tables(T):
    rows = T // GRID_W
    r = np.repeat(np.arange(rows), GRID_W).astype(np.float64)
    c = np.tile(np.arange(GRID_W), rows).astype(np.float64)
    n_freq = HEAD_DK // 4
    freqs = ROPE_BASE ** (-np.arange(n_freq, dtype=np.float64) / n_freq)
    ang_r = r[:, None] * freqs
    ang_c = c[:, None] * freqs
    cos = np.concatenate([np.cos(ang_r), np.cos(ang_r), np.cos(ang_c), np.cos(ang_c)], axis=-1)
    sin = np.concatenate([-np.sin(ang_r), np.sin(ang_r), -np.sin(ang_c), np.sin(ang_c)], axis=-1)
    return jnp.asarray(cos, F32), jnp.asarray(sin, F32)


def _path(x, ada, mod, init_states, rope, wts):
    T = x.shape[1]
    (qdf, kif, drf, qdb, kib, drb, dtf, dtb, v, ga, u, gb, sa, sb) = _in_proj(
        x, ada, mod, wts["norm_g"], wts["w_in"], wts["w_tail"], wts["w_ab"], wts["b_ab"], rope)
    og, sf, sbw = _gla(qdf, kif, drf, qdb, kib, drb, dtf, dtb, v, ga, init_states, wts["gla_norm_g"])
    if T >= FOURIER_SPLIT_MIN_LEN:
        fg = _fourier_split(u, gb, wts["four_mix"])
    else:
        ct, stn = _time_dft_tables(T, T, np.arange(T))
        fg = _fourier(ct, stn, u, gb, wts["four_mix"])
    y = _out_proj(og, fg, sa, sb, x, ada, mod, wts["final_norm_g"], wts["w_proj_a"], wts["w_proj_b"], wts["w_out"])
    return y, sf, sbw


def kernel(x_prompt, x_sample, state_gla_fwd, state_gla_bwd, c, c_ctx, w_ada, b_ada, norm_g, w_in,
           w_alpha_fwd, b_alpha_fwd, w_alpha_bwd, b_alpha_bwd, gla_norm_g, w_four, w_proj_a, w_proj_b,
           w_out, final_norm_g):
    depth = w_in.shape[0]
    assert depth == 1, "single trunk layer"
    bs = x_sample.shape[0]

    n_cond = bs + 1
    cond_rows = -(-n_cond // SUBLANES) * SUBLANES
    cond = jnp.concatenate([c, c_ctx[None, :], jnp.zeros((cond_rows - n_cond, D_MODEL), F32)], axis=0)
    ada = _ada(cond, w_ada[0], b_ada[0][None, :])

    wi = w_in[0]
    o_u = _W_HEAD_COLS + 2 * GATE_RANK
    w_ab = jnp.zeros((2 * GATE_RANK, 2 * KDIM), F32)
    w_ab = w_ab.at[:GATE_RANK, :KDIM].set(w_alpha_fwd[0]).at[GATE_RANK:, KDIM:].set(w_alpha_bwd[0])
    wts = dict(
        norm_g=norm_g[0][None, :], w_in=wi.astype(BF16), w_tail=wi[:, o_u:].astype(BF16),
        w_ab=w_ab.astype(BF16),
        b_ab=jnp.concatenate([b_alpha_fwd[0], b_alpha_bwd[0]])[None, :],
        gla_norm_g=gla_norm_g[0][None, :], four_mix=_mix_weights(w_four[0]),
        w_proj_a=w_proj_a[0].astype(BF16), w_proj_b=w_proj_b[0].astype(BF16), w_out=w_out[0].astype(BF16),
        final_norm_g=final_norm_g[None, :])

    y_prompt, sf, sb = _path(x_prompt, ada, (bs, False), None, None, wts)
    y_sample, _, _ = _path(x_sample, ada, (0, True),
                           (state_gla_fwd[:, 0], state_gla_bwd[:, 0]), _rope_tables(x_sample.shape[1]), wts)
    return (y_prompt, y_sample, sf[:, None].astype(x_prompt.dtype), sb[:, None].astype(x_prompt.dtype))
```

```python
import functools

import numpy as np
import jax
import jax.numpy as jnp
from jax import lax
from jax.experimental import pallas as pl
from jax.experimental.pallas import tpu as pltpu

F32 = jnp.float32
BF16 = jnp.bfloat16

D_MODEL = 1024
N_HEADS = 4
HEAD_DK = 128
HEAD_DV = 256
KDIM = N_HEADS * HEAD_DK
VDIM = N_HEADS * HEAD_DV
GATE_RANK = 16
GATE_NORM = 16.0
CHUNK = 64
N_FGROUPS = 4
FGROUP_CH = 128
FOURIER_DIM = N_FGROUPS * FGROUP_CH
GRID_W = 64
ROPE_BASE = 10000.0
EPS = 1e-6
LOG2_E = float(np.log2(np.e))

LANES = 128
SUBLANES = 8
MXU_COLS = 256
TOKEN_TILE = 256
CHUNKS_PER_TILE = TOKEN_TILE // CHUNK
IN_TILES_PER_STEP = 2
OUT_TILES_PER_STEP = 4
GLA_SHORT_SEQS_PER_STEP = 4
GLA_LONG_SEQ_HEADS_PER_STEP = 2
GLA_TILE_UNROLL = 4
FOURIER_ROWS_PER_STEP = 512
FOURIER_SHORT_SEQS_PER_STEP = 8
FOURIER_SPLIT_MIN_LEN = 2048
VMEM_LIMIT_BYTES = 56 * 1024 * 1024

_OFF_QK = 0
_OFF_V = _OFF_QK + 2 * KDIM
_OFF_GA = _OFF_V + VDIM
_W_HEAD_COLS = _OFF_GA + VDIM
_OFF_U = 0
_OFF_GB = _OFF_U + FOURIER_DIM
_OFF_MA = _OFF_GB + FOURIER_DIM
_OFF_MB = _OFF_MA + D_MODEL
_W_TAIL_COLS = _OFF_MB + D_MODEL


def _mm(a, b):
    return jnp.dot(a, b, preferred_element_type=F32)


def _mm_ta(a, b):
    return lax.dot_general(a, b, (((0,), (0,)), ((), ())), preferred_element_type=F32)


def _mm_tb(a, b):
    return lax.dot_general(a, b, (((1,), (1,)), ((), ())), preferred_element_type=F32)


def _mm_ta_tb(a, b):
    return lax.dot_general(a, b, (((0,), (1,)), ((), ())), preferred_element_type=F32)


def _split_bf16(x):
    hi = x.astype(BF16)
    lo = (x - hi.astype(F32)).astype(BF16)
    return hi, lo


def _const_spec(shape):
    nd = len(shape)
    return pl.BlockSpec(shape, lambda *_: (0,) * nd)


_ADA_SHIFT, _ADA_SCALE, _ADA_GATE = range(3)


def _mod_spec(mod, col_block):
    row0, per_seq = mod
    if per_seq:
        return pl.BlockSpec((SUBLANES, D_MODEL), lambda b, t: ((row0 + b) // SUBLANES, col_block))
    return pl.BlockSpec((SUBLANES, D_MODEL), lambda b, t: (row0 // SUBLANES, col_block))


def _mod_row(mod, ref):
    row0, per_seq = mod
    row = (row0 + pl.program_id(0)) % SUBLANES if per_seq else row0 % SUBLANES
    return ref.at[pl.ds(row, 1)]


def _ada_kernel(c_ref, w_ref, b_ref, o_ref):
    c = c_ref[...]
    s = c * jax.nn.sigmoid(c)
    o_ref[...] = _mm(s.astype(BF16), w_ref[...].astype(BF16)) + b_ref[...]


def _ada(cond, w, b):
    rows = cond.shape[0]
    n_blocks = w.shape[1] // D_MODEL
    return pl.pallas_call(
        _ada_kernel,
        grid=(n_blocks,),
        in_specs=[pl.BlockSpec((rows, D_MODEL), lambda n: (0, 0)),
                  pl.BlockSpec((D_MODEL, D_MODEL), lambda n: (0, n)),
                  pl.BlockSpec((1, D_MODEL), lambda n: (0, n))],
        out_specs=pl.BlockSpec((rows, D_MODEL), lambda n: (0, n)),
        out_shape=jax.ShapeDtypeStruct((rows, n_blocks * D_MODEL), F32),
        compiler_params=pltpu.CompilerParams(
            dimension_semantics=("arbitrary",), vmem_limit_bytes=VMEM_LIMIT_BYTES),
        name="ada",
    )(cond, w, b)


def _in_proj_kernel(use_rope, mod, *refs):
    n_shared = 8
    n_in = 1 + n_shared + (2 if use_rope else 0)
    per_tile_slots = (n_in + 2, n_in + 5, n_in + 6, n_in + 7)
    tiles = []
    for i in range(IN_TILES_PER_STEP):
        rows = pl.ds(i * TOKEN_TILE, TOKEN_TILE)
        tile_refs = []
        for idx, ref in enumerate(refs):
            if idx in (1, 2):
                tile_refs.append(_mod_row(mod, ref))
            elif 1 <= idx <= n_shared:
                tile_refs.append(ref)
            elif idx in per_tile_slots:
                tile_refs.append(ref.at[i])
            else:
                tile_refs.append(ref.at[rows])
        tiles.append(_in_proj_tile(use_rope, *tile_refs))
    for _ in range(_IN_PROJ_STAGES):
        for tile in tiles:
            next(tile)


def _project(hb, w_ref, off, width, act, out_ref):
    for n in range(0, width, MXU_COLS):
        z = _mm(hb, w_ref[:, off + n:off + n + MXU_COLS])
        out_ref[:, n:n + MXU_COLS] = act(z).astype(BF16)


_IN_PROJ_STAGES = 3


def _in_proj_tile(use_rope, *refs):
    if use_rope:
        (x_ref, sc_ref, sh_ref, ng_ref, wa_ref, wb_ref, wr_ref, wab_ref, bab_ref, cos_ref, sin_ref,
         qdf_ref, kif_ref, drf_ref, qdb_ref, kib_ref, drb_ref, dtf_ref, dtb_ref,
         v_ref, ga_ref, u_ref, gb_ref, sa_ref, sb_ref) = refs
    else:
        (x_ref, sc_ref, sh_ref, ng_ref, wa_ref, wb_ref, wr_ref, wab_ref, bab_ref,
         qdf_ref, kif_ref, drf_ref, qdb_ref, kib_ref, drb_ref, dtf_ref, dtb_ref,
         v_ref, ga_ref, u_ref, gb_ref, sa_ref, sb_ref) = refs

    x = x_ref[...]
    xn = x * lax.rsqrt(jnp.mean(x * x, axis=-1, keepdims=True) + EPS)
    h = xn * (ng_ref[...] * (1.0 + sc_ref[...])) + sh_ref[...]
    hb = h.astype(BF16)

    r_t = _mm_ta_tb(wr_ref[:, :2 * GATE_RANK], hb)
    qk = _mm(hb, wa_ref[:, _OFF_QK:_OFF_QK + 2 * KDIM])
    xgs = [_mm_ta(r_t[d * GATE_RANK:(d + 1) * GATE_RANK].astype(BF16), wab_ref[d])
           + bab_ref[:, d * KDIM:(d + 1) * KDIM] for d in range(2)]
    yield

    _project(hb, wa_ref, _OFF_V, VDIM, lambda z: z, v_ref)
    row = lax.broadcasted_iota(jnp.int32, (CHUNK, 2 * CHUNK), 0)
    col = lax.broadcasted_iota(jnp.int32, (CHUNK, 2 * CHUNK), 1) % CHUNK
    bcs = []
    for direction in range(2):
        xg = xgs[direction]
        g = (jnp.minimum(xg, 0.0) - jnp.log(1.0 + jnp.exp(-jnp.abs(xg)))) * (LOG2_E / GATE_NORM)
        g_hi, g_lo = _split_bf16(g)
        tri = jnp.where((col <= row) if direction == 0 else (col >= row), 1.0, 0.0).astype(BF16)
        bc_chunks = []
        for c in range(CHUNKS_PER_TILE):
            cr = slice(c * CHUNK, (c + 1) * CHUNK)
            bc_chunks.append(_mm(tri, jnp.concatenate([g_hi[cr], g_lo[cr]], axis=0)))
        bcs.append(jnp.concatenate(bc_chunks, axis=0))
    yield

    q = qk[:, :KDIM] * (HEAD_DK ** -0.5)
    k = qk[:, KDIM:]
    if use_rope:
        cos = cos_ref[...]
        sin = sin_ref[...]
        lane = lax.broadcasted_iota(jnp.int32, (TOKEN_TILE, HEAD_DK), 1)
        first_half = (lane // (HEAD_DK // 4)) % 2 == 0

        def rope(t):
            outs = []
            for hh in range(N_HEADS):
                th = t[:, hh * HEAD_DK:(hh + 1) * HEAD_DK]
                partner = jnp.where(first_half,
                                    pltpu.roll(th, HEAD_DK - HEAD_DK // 4, axis=1),
                                    pltpu.roll(th, HEAD_DK // 4, axis=1))
                outs.append(th * cos + partner * sin)
            return jnp.concatenate(outs, axis=1)

        q = rope(q)
        k = rope(k)

    for direction, (qd_ref, ki_ref, dr_ref, dt_ref) in enumerate(
            ((qdf_ref, kif_ref, drf_ref, dtf_ref), (qdb_ref, kib_ref, drb_ref, dtb_ref))):
        bc = bcs[direction]
        edge = CHUNK - 1 if direction == 0 else 0
        bl_rows = bc.reshape(CHUNKS_PER_TILE, CHUNK, KDIM)[:, edge, :]
        qd_ref[...] = (q * jnp.exp2(bc)).astype(BF16)
        ki_ref[...] = (k * jnp.exp2(-bc)).astype(BF16)
        dr = jnp.exp2(jnp.concatenate([bl_rows, jnp.zeros((SUBLANES - CHUNKS_PER_TILE, KDIM), F32)], axis=0))
        dr_ref[...] = dr
        dt_ref[...] = jnp.concatenate([dr, jnp.zeros((LANES - SUBLANES, KDIM), F32)], axis=0).T

    _project(hb, wa_ref, _OFF_GA, VDIM, jax.nn.silu, ga_ref)
    _project(hb, wb_ref, _OFF_U, FOURIER_DIM, lambda z: z, u_ref)
    _project(hb, wb_ref, _OFF_GB, FOURIER_DIM, jax.nn.silu, gb_ref)
    _project(hb, wb_ref, _OFF_MA, D_MODEL, jax.nn.sigmoid, sa_ref)
    _project(hb, wb_ref, _OFF_MB, D_MODEL, jax.nn.sigmoid, sb_ref)
    yield


def _in_proj(x, ada, mod, norm_g, w_in, w_tail, w_ab, b_ab, rope):
    B, T, _ = x.shape
    use_rope = rope is not None
    step_rows = IN_TILES_PER_STEP * TOKEN_TILE
    if T < step_rows:
        assert not mod[1] and not use_rope and (B * T) % step_rows == 0
        outs = _in_proj(x.reshape(B * T // step_rows, step_rows, D_MODEL), ada, mod, norm_g,
                        w_in, w_tail, w_ab, b_ab, rope)
        return [o.reshape((B, T // TOKEN_TILE) + o.shape[2:]) if o.ndim == 4 else o.reshape(B, T, o.shape[-1])
                for o in outs]
    nt = T // TOKEN_TILE

    def tok_spec(cols):
        return pl.BlockSpec((None, step_rows, cols), lambda b, t: (b, t, 0))

    assert _W_HEAD_COLS % LANES == 0 and 2 * GATE_RANK <= LANES
    in_specs = [
        tok_spec(D_MODEL), _mod_spec(mod, _ADA_SCALE), _mod_spec(mod, _ADA_SHIFT), _const_spec((1, D_MODEL)),
        _const_spec((D_MODEL, _W_HEAD_COLS)), _const_spec((D_MODEL, _W_TAIL_COLS)),
        pl.BlockSpec((D_MODEL, LANES), lambda b, t: (0, _W_HEAD_COLS // LANES)),
        _const_spec((2, GATE_RANK, KDIM)), _const_spec((1, 2 * KDIM)),
    ]
    args = [x, ada, ada, norm_g, w_in, w_tail, w_in, w_ab, b_ab]
    if use_rope:
        in_specs += [pl.BlockSpec((step_rows, HEAD_DK), lambda b, t: (t, 0))] * 2
        args += list(rope)

    dt_spec = pl.BlockSpec((None, IN_TILES_PER_STEP, KDIM, LANES), lambda b, t: (b, t, 0, 0))
    tok_bf = lambda cols: jax.ShapeDtypeStruct((B, T, cols), BF16)
    dt_shape = jax.ShapeDtypeStruct((B, nt, KDIM, LANES), F32)
    dr_spec = pl.BlockSpec((None, IN_TILES_PER_STEP, SUBLANES, KDIM), lambda b, t: (b, t, 0, 0))
    dr_shape = jax.ShapeDtypeStruct((B, nt, SUBLANES, KDIM), F32)
    out_specs = [tok_spec(KDIM), tok_spec(KDIM), dr_spec] * 2 + [dt_spec] * 2 + [
        tok_spec(VDIM), tok_spec(VDIM), tok_spec(FOURIER_DIM),
        tok_spec(FOURIER_DIM), tok_spec(D_MODEL), tok_spec(D_MODEL)]
    out_shape = [tok_bf(KDIM), tok_bf(KDIM), dr_shape] * 2 + [dt_shape] * 2 + [
        tok_bf(VDIM), tok_bf(VDIM), tok_bf(FOURIER_DIM),
        tok_bf(FOURIER_DIM), tok_bf(D_MODEL), tok_bf(D_MODEL)]
    return pl.pallas_call(
        functools.partial(_in_proj_kernel, use_rope, mod),
        grid=(B, T // step_rows),
        in_specs=in_specs,
        out_specs=out_specs,
        out_shape=out_shape,
        compiler_params=pltpu.CompilerParams(
            dimension_semantics=("arbitrary", "arbitrary"), vmem_limit_bytes=VMEM_LIMIT_BYTES),
        name="in_proj_rope" if use_rope else "in_proj",
    )(*args)


def _gla_kernel(heads, n_tiles, has_init, seqs, *refs):
    gain_pos = 10 + (2 if has_init else 0)
    for bb in range(seqs):
        _gla_seq(heads, n_tiles, has_init, *[r if i == gain_pos else r.at[bb] for i, r in enumerate(refs)])


def _gla_seq(heads, n_tiles, has_init, *refs):
    (qdf_ref, kif_ref, drf_ref, qdb_ref, kib_ref, drb_ref, dtf_ref, dtb_ref, v_ref, ga_ref) = refs[:10]
    refs = refs[10:]
    if has_init:
        s0f_ref, s0b_ref = refs[:2]
        refs = refs[2:]
    gn_ref, og_ref, sf_ref, sb_ref, state, d_state, of_acc, ob_acc = refs
    if has_init:
        state[0] = s0f_ref[...]
        state[1] = s0b_ref[...]
    else:
        state[...] = jnp.zeros_like(state)
    dir_refs = ((qdf_ref, kif_ref, drf_ref, dtf_ref, of_acc), (qdb_ref, kib_ref, drb_ref, dtb_ref, ob_acc))

    def tile_body(finalize, j, carry):
        row = lax.broadcasted_iota(jnp.int32, (TOKEN_TILE, TOKEN_TILE), 0)
        col = lax.broadcasted_iota(jnp.int32, (TOKEN_TILE, TOKEN_TILE), 1)
        same_chunk = (row // CHUNK) == (col // CHUNK)
        causal = (same_chunk & (col <= row), same_chunk & (col >= row))
        tiles = (j, n_tiles - 1 - j)
        bases = tuple(pl.multiple_of(t * TOKEN_TILE, TOKEN_TILE) for t in tiles)

        scores = {}
        for direction in range(2):
            qd_ref, ki_ref, dr_ref, _, o_acc = dir_refs[direction]
            trows = pl.ds(bases[direction], TOKEN_TILE)
            for hh in range(heads):
                kc = slice(hh * HEAD_DK, (hh + 1) * HEAD_DK)
                vc = slice(hh * HEAD_DV, (hh + 1) * HEAD_DV)
                scores[direction, hh] = jnp.where(
                    causal[direction], _mm_tb(qd_ref[trows, kc], ki_ref[trows, kc]), 0.0).astype(BF16)
                chunk_decay = dr_ref[tiles[direction], :, kc]
                for c in range(CHUNKS_PER_TILE):
                    rows = pl.ds(bases[direction] + c * CHUNK, CHUNK)
                    ke = (ki_ref[rows, kc].astype(F32) * chunk_decay[c:c + 1, :]).astype(BF16)
                    d_state[direction, hh, c] = _mm_ta(ke, v_ref[rows, vc])

        steps = [(step, direction) for step in range(CHUNKS_PER_TILE) for direction in range(2)]
        if finalize[0] != finalize[1]:
            steps.sort(key=lambda sd: finalize[sd[1]])
        for step, direction in steps:
            qd_ref, _, _, dt_ref, o_acc = dir_refs[direction]
            other_acc = dir_refs[1 - direction][4]
            c = step if direction == 0 else CHUNKS_PER_TILE - 1 - step
            rows = pl.ds(bases[direction] + c * CHUNK, CHUNK)
            for hh in range(heads):
                kc = slice(hh * HEAD_DK, (hh + 1) * HEAD_DK)
                vc = slice(hh * HEAD_DV, (hh + 1) * HEAD_DV)
                s = state[direction, hh]
                blk = (c * CHUNK) // LANES
                sc = scores[direction, hh][c * CHUNK:(c + 1) * CHUNK, blk * LANES:(blk + 1) * LANES]
                v_blk = v_ref[pl.ds(bases[direction] + blk * LANES, LANES), vc]
                o = _mm(jnp.concatenate([qd_ref[rows, kc], sc], axis=1),
                        jnp.concatenate([s.astype(BF16), v_blk], axis=0))
                if finalize[direction]:
                    o = o + other_acc[rows, vc]
                    on = o * lax.rsqrt(jnp.mean(o * o, axis=-1, keepdims=True) + EPS) * gn_ref[...]
                    og_ref[rows, vc] = (on * ga_ref[rows, vc].astype(F32)).astype(BF16)
                else:
                    o_acc[rows, vc] = o
                dec = dt_ref[tiles[direction], kc, c:c + 1]
                state[direction, hh] = dec * s + d_state[direction, hh, c]
        return carry

    first_half = n_tiles // 2
    unroll = GLA_TILE_UNROLL if first_half % GLA_TILE_UNROLL == 0 else 1
    lax.fori_loop(0, first_half, functools.partial(tile_body, (False, False)), 0, unroll=unroll)
    if n_tiles % 2:
        tile_body((False, True), first_half, 0)
    lax.fori_loop(n_tiles - first_half, n_tiles, functools.partial(tile_body, (True, True)), 0, unroll=unroll)
    sf_ref[...] = state[0]
    sb_ref[...] = state[1]


def _gla(qdf, kif, drf, qdb, kib, drb, dtf, dtb, v, ga, init_states, gla_norm_g):
    B, T, _ = v.shape
    nt = T // TOKEN_TILE
    has_init = init_states is not None
    seqs = GLA_SHORT_SEQS_PER_STEP if nt == 1 else 1
    assert B % seqs == 0
    heads = N_HEADS if nt == 1 else GLA_LONG_SEQ_HEADS_PER_STEP
    hg = N_HEADS // heads
    k_spec = pl.BlockSpec((seqs, T, heads * HEAD_DK), lambda b, h: (b, 0, h))
    v_spec = pl.BlockSpec((seqs, T, heads * HEAD_DV), lambda b, h: (b, 0, h))
    dt_spec = pl.BlockSpec((seqs, nt, heads * HEAD_DK, LANES), lambda b, h: (b, 0, h, 0))
    dr_spec = pl.BlockSpec((seqs, nt, SUBLANES, heads * HEAD_DK), lambda b, h: (b, 0, 0, h))
    s_spec = pl.BlockSpec((seqs, heads, HEAD_DK, HEAD_DV), lambda b, h: (b, h, 0, 0))
    s_shape = jax.ShapeDtypeStruct((B, N_HEADS, HEAD_DK, HEAD_DV), F32)
    return pl.pallas_call(
        functools.partial(_gla_kernel, heads, nt, has_init, seqs),
        grid=(B // seqs, hg),
        in_specs=([k_spec, k_spec, dr_spec] * 2 + [dt_spec] * 2 + [v_spec, v_spec] + [s_spec] * (2 if has_init else 0)
                  + [_const_spec((1, HEAD_DV))]),
        out_specs=[v_spec, s_spec, s_spec],
        out_shape=[jax.ShapeDtypeStruct((B, T, VDIM), BF16), s_shape, s_shape],
        scratch_shapes=[pltpu.VMEM((seqs, 2, heads, HEAD_DK, HEAD_DV), F32),
                        pltpu.VMEM((seqs, 2, heads, CHUNKS_PER_TILE, HEAD_DK, HEAD_DV), F32),
                        pltpu.VMEM((seqs, T, heads * HEAD_DV), F32),
                        pltpu.VMEM((seqs, T, heads * HEAD_DV), F32)],
        compiler_params=pltpu.CompilerParams(
            dimension_semantics=("arbitrary", "arbitrary"), vmem_limit_bytes=VMEM_LIMIT_BYTES),
        name="gla",
    )(qdf, kif, drf, qdb, kib, drb, dtf, dtb, v, ga, *(init_states or ()), gla_norm_g)


def _mix_weights_kernel(tab_ref, wf_ref, o_ref):
    for g in range(N_FGROUPS):
        o_ref[g] = jnp.dot(tab_ref[...], wf_ref[g], preferred_element_type=F32,
                           precision=lax.Precision.HIGHEST).astype(BF16)


def _mix_weights(wf):
    n = np.arange(FGROUP_CH)
    ang = 2.0 * np.pi * ((n[:, None] * n[None, :]) % FGROUP_CH) / FGROUP_CH
    tab = jnp.asarray(np.concatenate([np.cos(ang), np.sin(ang)], axis=0) / np.sqrt(FGROUP_CH), F32)
    return pl.pallas_call(
        _mix_weights_kernel,
        out_shape=jax.ShapeDtypeStruct((N_FGROUPS, 2 * FGROUP_CH, FGROUP_CH), BF16),
        name="mix_weights",
    )(tab, wf)


def _mix_and_gate(fp, fq, mix_ref, gb, store):
    fp = fp.astype(BF16)
    fq = fq.astype(BF16)
    for g in range(N_FGROUPS):
        cols = slice(g * FGROUP_CH, (g + 1) * FGROUP_CH)
        z = _mm(jnp.concatenate([fp[:, cols], fq[:, cols]], axis=1), mix_ref[g])
        store(cols, (z * gb(cols).astype(F32)).astype(BF16))


def _fourier_kernel(seqs, ct_ref, st_ref, u_ref, gb_ref, mix_ref, o_ref):
    for bb in range(seqs):
        u = u_ref[bb]

        def store(cols, val, bb=bb):
            o_ref[bb, :, cols] = val

        _mix_and_gate(_mm(ct_ref[...], u), _mm(st_ref[...], u), mix_ref,
                      lambda cols, bb=bb: gb_ref[bb, :, cols], store)


def _fourier(ct, stn, u, gb, mix):
    B, T, _ = u.shape
    tf = min(T, FOURIER_ROWS_PER_STEP)
    seqs = FOURIER_SHORT_SEQS_PER_STEP if T < FOURIER_ROWS_PER_STEP else 1
    assert B % seqs == 0
    tab_spec = pl.BlockSpec((tf, T), lambda b, t: (t, 0))
    seq_spec = pl.BlockSpec((seqs, T, FOURIER_DIM), lambda b, t: (b, 0, 0))
    tile_spec = pl.BlockSpec((seqs, tf, FOURIER_DIM), lambda b, t: (b, t, 0))
    return pl.pallas_call(
        functools.partial(_fourier_kernel, seqs),
        grid=(B // seqs, T // tf),
        in_specs=[tab_spec, tab_spec, seq_spec, tile_spec,
                  _const_spec((N_FGROUPS, 2 * FGROUP_CH, FGROUP_CH))],
        out_specs=tile_spec,
        out_shape=jax.ShapeDtypeStruct((B, T, FOURIER_DIM), BF16),
        compiler_params=pltpu.CompilerParams(
            dimension_semantics=("arbitrary", "arbitrary"), vmem_limit_bytes=VMEM_LIMIT_BYTES),
        name="fourier",
    )(ct, stn, u, gb, mix)


def _fourier_split_kernel(cee_ref, see_ref, ceo_ref, seo_ref, co_ref, so_ref, u_ref, gb_ref, mix_ref, o_ref,
                          wide, u_odd, u_even):
    half, quarter = u_odd.shape[0], u_even.shape[1]
    for g in range(N_FGROUPS):
        cols = slice(g * FGROUP_CH, (g + 1) * FGROUP_CH)
        wide[g] = u_ref[:, cols].astype(F32)
        u_odd[:, cols] = wide[g, pl.ds(1, half, stride=2), :].astype(BF16)
        for r in range(2):
            u_even[r, :, cols] = wide[g, pl.ds(2 * r, quarter, stride=4), :].astype(BF16)

    eep, eeq = _mm(cee_ref[...], u_even[0]), _mm(see_ref[...], u_even[0])
    eop, eoq = _mm(ceo_ref[...], u_even[1]), _mm(seo_ref[...], u_even[1])
    e_blocks = ((eep + eop, eeq + eoq), (eep - eop, eeq - eoq))
    for kb, (ep, eq) in enumerate(e_blocks):
        rows = slice(kb * quarter, (kb + 1) * quarter)
        op, oq = _mm(co_ref[rows, :], u_odd[...]), _mm(so_ref[rows, :], u_odd[...])
        for upper, (fp, fq) in enumerate(((ep + op, eq + oq), (ep - op, eq - oq))):
            def store(cols, val, upper=upper):
                o_ref[upper, rows, cols] = val

            _mix_and_gate(fp, fq, mix_ref, lambda cols, upper=upper: gb_ref[upper, rows, cols], store)


def _fourier_split(u, gb, mix):
    B, T, _ = u.shape
    half, quarter = T // 2, T // 4
    cee, seen = _time_dft_tables(T, quarter, 4 * np.arange(quarter))
    ceo, seon = _time_dft_tables(T, quarter, 4 * np.arange(quarter) + 2)
    co, son = _time_dft_tables(T, half, 2 * np.arange(half) + 1)
    seq_spec = pl.BlockSpec((None, T, FOURIER_DIM), lambda b: (b, 0, 0))
    halves_spec = pl.BlockSpec((None, 2, half, FOURIER_DIM), lambda b: (b, 0, 0, 0))
    out = pl.pallas_call(
        _fourier_split_kernel,
        grid=(B,),
        in_specs=[_const_spec((quarter, quarter))] * 4 + [_const_spec((half, half))] * 2 + [
            seq_spec, halves_spec, _const_spec((N_FGROUPS, 2 * FGROUP_CH, FGROUP_CH))],
        out_specs=halves_spec,
        out_shape=jax.ShapeDtypeStruct((B, 2, half, FOURIER_DIM), BF16),
        scratch_shapes=[pltpu.VMEM((N_FGROUPS, T, FGROUP_CH), F32),
                        pltpu.VMEM((half, FOURIER_DIM), BF16),
                        pltpu.VMEM((2, quarter, FOURIER_DIM), BF16)],
        compiler_params=pltpu.CompilerParams(
            dimension_semantics=("arbitrary",), vmem_limit_bytes=VMEM_LIMIT_BYTES),
        name="fourier_split",
    )(cee, seen, ceo, seon, co, son, u, gb.reshape(B, 2, half, FOURIER_DIM), mix)
    return out.reshape(B, T, FOURIER_DIM)


def _out_proj_kernel(mod, og_ref, fg_ref, sa_ref, sb_ref, x_ref, gate_ref, fng_ref, wpa_ref, wpb_ref, wo_ref,
                     y_ref):
    gate_ref = _mod_row(mod, gate_ref)
    for i in range(OUT_TILES_PER_STEP):
        rows = slice(i * TOKEN_TILE, (i + 1) * TOKEN_TILE)
        ya = _mm(og_ref[rows, :], wpa_ref[...])
        yb = _mm(fg_ref[rows, :], wpb_ref[...])
        merged = sa_ref[rows, :].astype(F32) * ya + sb_ref[rows, :].astype(F32) * yb
        xo = x_ref[rows, :] + gate_ref[...] * _mm(merged.astype(BF16), wo_ref[...])
        y_ref[rows, :] = xo * lax.rsqrt(jnp.mean(xo * xo, axis=-1, keepdims=True) + EPS) * fng_ref[...]


def _out_proj(og, fg, sa, sb, x, ada, mod, final_norm_g, wpa, wpb, wo):
    B, T, _ = x.shape
    step_rows = OUT_TILES_PER_STEP * TOKEN_TILE
    if T < step_rows:
        assert not mod[1] and (B * T) % step_rows == 0
        fold = lambda a: a.reshape(B * T // step_rows, step_rows, a.shape[-1])
        y = _out_proj(fold(og), fold(fg), fold(sa), fold(sb), fold(x), ada, mod, final_norm_g, wpa, wpb, wo)
        return y.reshape(B, T, D_MODEL)

    def tok_spec(cols):
        return pl.BlockSpec((None, step_rows, cols), lambda b, t: (b, t, 0))

    return pl.pallas_call(
        functools.partial(_out_proj_kernel, mod),
        grid=(B, T // step_rows),
        in_specs=[tok_spec(VDIM), tok_spec(FOURIER_DIM), tok_spec(D_MODEL), tok_spec(D_MODEL), tok_spec(D_MODEL),
                  _mod_spec(mod, _ADA_GATE), _const_spec((1, D_MODEL)),
                  _const_spec((VDIM, D_MODEL)), _const_spec((FOURIER_DIM, D_MODEL)), _const_spec((D_MODEL, D_MODEL))],
        out_specs=tok_spec(D_MODEL),
        out_shape=jax.ShapeDtypeStruct((B, T, D_MODEL), F32),
        compiler_params=pltpu.CompilerParams(
            dimension_semantics=("arbitrary", "arbitrary"), vmem_limit_bytes=VMEM_LIMIT_BYTES),
        name="out_proj",
    )(og, fg, sa, sb, x, ada, final_norm_g, wpa, wpb, wo)


def _time_dft_tables(T, n_rows, positions):
    n = np.asarray(positions)
    ang = 2.0 * np.pi * ((np.arange(n_rows)[:, None] * n[None, :]) % T) / T
    scale = 1.0 / np.sqrt(T)
    return jnp.asarray(np.cos(ang) * scale, F32).astype(BF16), jnp.asarray(-np.sin(ang) * scale, F32).astype(BF16)


def _rope_tables(T):
    rows = T // GRID_W
    r = np.repeat(np.arange(rows), GRID_W).astype(np.float64)
    c = np.tile(np.arange(GRID_W), rows).astype(np.float64)
    n_freq = HEAD_DK // 4
    freqs = ROPE_BASE ** (-np.arange(n_freq, dtype=np.float64) / n_freq)
    ang_r = r[:, None] * freqs
    ang_c = c[:, None] * freqs
    cos = np.concatenate([np.cos(ang_r), np.cos(ang_r), np.cos(ang_c), np.cos(ang_c)], axis=-1)
    sin = np.concatenate([-np.sin(ang_r), np.sin(ang_r), -np.sin(ang_c), np.sin(ang_c)], axis=-1)
    return jnp.asarray(cos, F32), jnp.asarray(sin, F32)


def _path(x, ada, mod, init_states, rope, wts):
    T = x.shape[1]
    (qdf, kif, drf, qdb, kib, drb, dtf, dtb, v, ga, u, gb, sa, sb) = _in_proj(
        x, ada, mod, wts["norm_g"], wts["w_in"], wts["w_tail"], wts["w_ab"], wts["b_ab"], rope)
    og, sf, sbw = _gla(qdf, kif, drf, qdb, kib, drb, dtf, dtb, v, ga, init_states, wts["gla_norm_g"])
    if T >= FOURIER_SPLIT_MIN_LEN:
        fg = _fourier_split(u, gb, wts["four_mix"])
    else:
        ct, stn = _time_dft_tables(T, T, np.arange(T))
        fg = _fourier(ct, stn, u, gb, wts["four_mix"])
    y = _out_proj(og, fg, sa, sb, x, ada, mod, wts["final_norm_g"], wts["w_proj_a"], wts["w_proj_b"], wts["w_out"])
    return y, sf, sbw


def kernel(x_prompt, x_sample, state_gla_fwd, state_gla_bwd, c, c_ctx, w_ada, b_ada, norm_g, w_in,
           w_alpha_fwd, b_alpha_fwd, w_alpha_bwd, b_alpha_bwd, gla_norm_g, w_four, w_proj_a, w_proj_b,
           w_out, final_norm_g):
    depth = w_in.shape[0]
    assert depth == 1, "single trunk layer"
    bs = x_sample.shape[0]

    n_cond = bs + 1
    cond_rows = -(-n_cond // SUBLANES) * SUBLANES
    cond = jnp.concatenate([c, c_ctx[None, :], jnp.zeros((cond_rows - n_cond, D_MODEL), F32)], axis=0)
    ada = _ada(cond, w_ada[0], b_ada[0][None, :])

    wi = w_in[0]
    o_u = _W_HEAD_COLS + 2 * GATE_RANK
    w_ab = jnp.stack([w_alpha_fwd[0], w_alpha_bwd[0]])
    wts = dict(
        norm_g=norm_g[0][None, :], w_in=wi.astype(BF16), w_tail=wi[:, o_u:].astype(BF16),
        w_ab=w_ab.astype(BF16),
        b_ab=jnp.concatenate([b_alpha_fwd[0], b_alpha_bwd[0]])[None, :],
        gla_norm_g=gla_norm_g[0][None, :], four_mix=_mix_weights(w_four[0]),
        w_proj_a=w_proj_a[0].astype(BF16), w_proj_b=w_proj_b[0].astype(BF16), w_out=w_out[0].astype(BF16),
        final_norm_g=final_norm_g[None, :])

    y_prompt, sf, sb = _path(x_prompt, ada, (bs, False), None, None, wts)
    y_sample, _, _ = _path(x_sample, ada, (0, True),
                           (state_gla_fwd[:, 0], state_gla_bwd[:, 0]), _rope_tables(x_sample.shape[1]), wts)
    return (y_prompt, y_sample, sf[:, None].astype(x_prompt.dtype), sb[:, None].astype(x_prompt.dtype))
```
